```python
import jax
import jax.numpy as jnp
from jax import lax
import numpy as np

D_MODEL = 1024
BATCH = 8
SEQ = 4096
DEPTH = 2

CHUNK = 64
N_META = 16
MIX_WIDTH = D_MODEL
RMS_EPS = 1e-6

MLA_HEADS = 4
MLA_NOPE = 128
MLA_ROPE = 64
MLA_V = 128
MLA_Q_LORA = 512
MLA_KV_LORA = 256
ROPE_THETA = 10000.0
Q_BLOCK = 128
MLA_WIDTH = MLA_HEADS * MLA_V

RWKV_WIDTH = MIX_WIDTH - MLA_WIDTH
RWKV_HEAD = 64
RWKV_HEADS = RWKV_WIDTH // RWKV_HEAD
W_LORA = 64
A_LORA = 64
V_LORA = 32
G_LORA = 128
GN_EPS = 64e-5

MLA_COLS = MLA_Q_LORA + MLA_KV_LORA + MLA_ROPE
RWKV_COLS = 3 * RWKV_WIDTH + W_LORA + A_LORA + G_LORA
IN_COLS = MLA_COLS + RWKV_COLS

DENSE_FF = 2816
N_EXPERTS = 8
TOP_K = 2
EXPERT_FF = 3584
MOE_BLOCK = 256
N_DENSE = (DEPTH + 1) // 2
N_MOE = DEPTH // 2

kernel_name = 'hybrid_mla_rwkv7_moe_stream_encoder'


def rms_norm(x, g):
    xf = x.astype(jnp.float32)
    y = xf * lax.rsqrt(jnp.mean(xf * xf, -1, keepdims=True) + RMS_EPS)
    return (y * g.astype(jnp.float32)).astype(x.dtype)


def rope(x, pos):
    half = MLA_ROPE // 2
    inv = ROPE_THETA ** (-jnp.arange(half, dtype=jnp.float32) / half)
    ang = pos.astype(jnp.float32)[:, None] * inv[None, :]
    cos, sin = jnp.cos(ang), jnp.sin(ang)
    xf = x.astype(jnp.float32)
    x1, x2 = xf[..., :half], xf[..., half:]
    return jnp.concatenate([x1 * cos - x2 * sin, x1 * sin + x2 * cos], -1).astype(x.dtype)


def chunk_ids(n_pad, n_valid):
    idx = jnp.arange(n_pad)
    cid = jnp.where(idx < N_META, 0, 1 + (idx - N_META) // CHUNK)
    return jnp.where(idx < n_valid, cid, n_pad)


def mla_mixer(pm, q_norm, kv_norm, w_uq, w_ukv):
    B, L, _ = pm.shape
    c_q = pm[..., :MLA_Q_LORA]
    c_kv = pm[..., MLA_Q_LORA:MLA_Q_LORA + MLA_KV_LORA]
    k_rope_in = pm[..., MLA_Q_LORA + MLA_KV_LORA:]
    pos = jnp.arange(L)
    q = (rms_norm(c_q, q_norm) @ w_uq).reshape(B, L, MLA_HEADS, MLA_NOPE + MLA_ROPE).transpose(0, 2, 1, 3)
    kv = (rms_norm(c_kv, kv_norm) @ w_ukv).reshape(B, L, MLA_HEADS, MLA_NOPE + MLA_V).transpose(0, 2, 1, 3)
    q_nope, q_rope = q[..., :MLA_NOPE], rope(q[..., MLA_NOPE:], pos)
    k_nope, v = kv[..., :MLA_NOPE], kv[..., MLA_NOPE:]
    k_rope = rope(k_rope_in, pos)
    n_blk = -(-L // Q_BLOCK)
    Lp = n_blk * Q_BLOCK
    pad4 = ((0, 0), (0, 0), (0, Lp - L), (0, 0))
    q_nope, q_rope = jnp.pad(q_nope, pad4), jnp.pad(q_rope, pad4)
    k_nope, v = jnp.pad(k_nope, pad4), jnp.pad(v, pad4)
    k_rope = jnp.pad(k_rope, ((0, 0), (0, Lp - L), (0, 0)))
    cid = chunk_ids(Lp, L)
    scale = (MLA_NOPE + MLA_ROPE) ** -0.5

    def to_blocks(t):
        return t.reshape(B, MLA_HEADS, n_blk, Q_BLOCK, t.shape[-1]).transpose(2, 0, 1, 3, 4)

    def query_block(args):
        qn, qr, qc = args
        s = jnp.einsum('bhqd,bhkd->bhqk', qn, k_nope) + jnp.einsum('bhqr,bkr->bhqk', qr, k_rope)
        s = s.astype(jnp.float32) * scale
        s = jnp.where(cid[None, :] <= qc[:, None], s, -jnp.inf)
        p = jax.nn.softmax(s, axis=-1).astype(v.dtype)
        return jnp.einsum('bhqk,bhkd->bhqd', p, v)

    o = lax.map(query_block, (to_blocks(q_nope), to_blocks(q_rope), cid.reshape(n_blk, Q_BLOCK)))
    o = o.transpose(1, 0, 3, 2, 4).reshape(B, Lp, MLA_WIDTH)
    return o[:, :L]


def token_shift(p, mu):
    prev = jnp.pad(p, ((0, 0), (1, 0), (0, 0)))[:, :-1]
    return p + (prev - p) * mu


def rwkv7_scan(r, w, k, v, a, b):
    B, L, H, N = r.shape

    def step(S, xs):
        r_t, w_t, k_t, v_t, a_t, b_t = xs
        sa = jnp.einsum('bhij,bhj->bhi', S, a_t)
        S = S * w_t[:, :, None, :] + sa[..., None] * b_t[:, :, None, :] + v_t[..., :, None] * k_t[:, :, None, :]
        return S, jnp.einsum('bhij,bhj->bhi', S, r_t)

    S0 = jnp.zeros((B, H, N, N), jnp.float32)
    xs = tuple(t.transpose(1, 0, 2, 3) for t in (r, w, k, v, a, b))
    _, y = lax.scan(step, S0, xs)
    return y.transpose(1, 0, 2, 3)


def rwkv7_mixer(p, mu, w0, w2, a0, a2, g2, k_k, k_a, r_k, ln_w, ln_b, v_first, v_res):
    B, L, _ = p.shape
    C, H, N = RWKV_WIDTH, RWKV_HEADS, RWKV_HEAD
    f32 = jnp.float32
    p = token_shift(p, mu)
    r = p[..., :C]
    k = p[..., C:2 * C]
    v = p[..., 2 * C:3 * C]
    o = 3 * C
    xw = p[..., o:o + W_LORA]
    xa = p[..., o + W_LORA:o + W_LORA + A_LORA]
    xg = p[..., o + W_LORA + A_LORA:]
    w_log = -jax.nn.softplus(-(w0 + jnp.tanh(xw) @ w2)) - 0.5
    decay = jnp.exp(-jnp.exp(w_log.astype(f32)))
    if v_res is None:
        v_first = v
    else:
        v0, v1, v2 = v_res
        v = v + (v_first - v) * jax.nn.sigmoid(v0 + (v @ v1) @ v2)
    a = jax.nn.sigmoid(a0 + xa @ a2)
    g = jax.nn.sigmoid(xg) @ g2

    def heads(t):
        return t.astype(f32).reshape(B, L, H, N)

    kk = heads(k * k_k)
    kk = kk * lax.rsqrt(jnp.maximum(jnp.sum(kk * kk, -1, keepdims=True), 1e-24))
    k = k * (1 + (a - 1) * k_a)
    rh, kh, vh, ah = heads(r), heads(k), heads(v), heads(a)
    y = rwkv7_scan(rh, decay.reshape(B, L, H, N), kh, vh, -kk, kk * ah)
    mean = jnp.mean(y, -1, keepdims=True)
    var = jnp.mean(jnp.square(y - mean), -1, keepdims=True)
    y = ((y - mean) * lax.rsqrt(var + GN_EPS)).reshape(B, L, C) * ln_w.astype(f32) + ln_b.astype(f32)
    bonus = jnp.sum(rh * kh * r_k.astype(f32), -1, keepdims=True) * vh
    y = (y + bonus.reshape(B, L, C)).astype(p.dtype) * g
    return y, v_first


def swiglu(h, w_gate, w_up, w_down):
    return (jax.nn.silu(h @ w_gate) * (h @ w_up)) @ w_down


def moe_swiglu(h, router, router_bias, w_gate, w_up, w_down):
    B, L, D = h.shape
    T = B * L
    x = h.reshape(T, D)
    logits = (x @ router).astype(jnp.float32) + router_bias.astype(jnp.float32)
    top_logit, top_idx = lax.top_k(logits, TOP_K)
    gates = jax.nn.softmax(top_logit, axis=-1)
    n_assign = T * TOP_K
    flat_e = top_idx.reshape(-1)
    flat_tok = jnp.repeat(jnp.arange(T, dtype=jnp.int32), TOP_K)
    flat_g = gates.reshape(-1)
    order = jnp.argsort(flat_e)
    se, stok, sg = flat_e[order], flat_tok[order], flat_g[order]
    counts = jnp.zeros((N_EXPERTS,), jnp.int32).at[flat_e].add(1)
    padded = (counts + MOE_BLOCK - 1) // MOE_BLOCK * MOE_BLOCK
    start = jnp.cumsum(counts) - counts
    pend = jnp.cumsum(padded)
    pstart = pend - padded
    slot = pstart[se] + (jnp.arange(n_assign, dtype=jnp.int32) - start[se])
    n_blocks = (n_assign + N_EXPERTS * (MOE_BLOCK - 1)) // MOE_BLOCK + 1
    P = n_blocks * MOE_BLOCK
    slot_tok = jnp.full((P,), T, jnp.int32).at[slot].set(stok)
    slot_gate = jnp.zeros((P,), jnp.float32).at[slot].set(sg)
    block_e = jnp.minimum(jnp.searchsorted(pend, jnp.arange(n_blocks, dtype=jnp.int32) * MOE_BLOCK, side='right'), N_EXPERTS - 1)
    x_pad = jnp.concatenate([x, jnp.zeros((1, D), x.dtype)], 0)
    xb = x_pad[slot_tok].reshape(n_blocks, MOE_BLOCK, D)

    def expert_block(args):
        xe, e = args
        return swiglu(xe, w_gate[e], w_up[e], w_down[e])

    yb = lax.map(expert_block, (xb, block_e)).reshape(P, D)
    out = jnp.zeros((T + 1, D), jnp.float32).at[slot_tok].add(yb.astype(jnp.float32) * slot_gate[:, None])
    return out[:T].astype(h.dtype).reshape(B, L, D)


def setup_inputs(seed: int = 0) -> dict:
    key = jax.random.key(seed)
    keys = iter(jax.random.split(key, 64))
    f32 = jnp.float32

    def normal(shape, scale):
        return scale * jax.random.normal(next(keys), shape, f32)

    def gain(shape):
        return 1.0 + 0.01 * jax.random.normal(next(keys), shape, f32)

    D, C = D_MODEL, RWKV_WIDTH
    return {
        'x': normal((BATCH, SEQ, D), 1.0),
        'meta_tokens': normal((N_META, D), 1.0),
        'attn_norm': gain((DEPTH, D)),
        'ffn_norm': gain((DEPTH, D)),
        'final_norm': gain((D,)),
        'w_in': normal((DEPTH, D, IN_COLS), D ** -0.5),
        'w_out': normal((DEPTH, MIX_WIDTH, D), MIX_WIDTH ** -0.5),
        'mla_q_norm': gain((DEPTH, MLA_Q_LORA)),
        'mla_kv_norm': gain((DEPTH, MLA_KV_LORA)),
        'mla_w_uq': normal((DEPTH, MLA_Q_LORA, MLA_HEADS * (MLA_NOPE + MLA_ROPE)), MLA_Q_LORA ** -0.5),
        'mla_w_ukv': normal((DEPTH, MLA_KV_LORA, MLA_HEADS * (MLA_NOPE + MLA_V)), MLA_KV_LORA ** -0.5),
        'rwkv_mu': jax.random.uniform(next(keys), (DEPTH, RWKV_COLS), f32),
        'rwkv_w0': -2.5 + normal((DEPTH, C), 0.5),
        'rwkv_w2': normal((DEPTH, W_LORA, C), W_LORA ** -0.5),
        'rwkv_a0': normal((DEPTH, C), 0.5),
        'rwkv_a2': normal((DEPTH, A_LORA, C), A_LORA ** -0.5),
        'rwkv_g2': normal((DEPTH, G_LORA, C), G_LORA ** -0.5),
        'rwkv_k_k': 0.85 + normal((DEPTH, C), 0.05),
        'rwkv_k_a': 1.0 + normal((DEPTH, C), 0.05),
        'rwkv_r_k': normal((DEPTH, RWKV_HEADS, RWKV_HEAD), 0.1),
        'rwkv_ln_w': gain((DEPTH, C)),
        'rwkv_ln_b': normal((DEPTH, C), 0.01),
        'rwkv_v0': normal((DEPTH - 1, C), 0.5),
        'rwkv_v1': normal((DEPTH - 1, C, V_LORA), C ** -0.5),
        'rwkv_v2': normal((DEPTH - 1, V_LORA, C), V_LORA ** -0.5),
        'ffn_w_gate': normal((N_DENSE, D, DENSE_FF), D ** -0.5),
        'ffn_w_up': normal((N_DENSE, D, DENSE_FF), D ** -0.5),
        'ffn_w_down': normal((N_DENSE, DENSE_FF, D), DENSE_FF ** -0.5),
        'moe_router': normal((N_MOE, D, N_EXPERTS), D ** -0.5),
        'moe_router_bias': normal((N_MOE, N_EXPERTS), 0.01),
        'moe_w_gate': normal((N_MOE, N_EXPERTS, D, EXPERT_FF), D ** -0.5),
        'moe_w_up': normal((N_MOE, N_EXPERTS, D, EXPERT_FF), D ** -0.5),
        'moe_w_down': normal((N_MOE, N_EXPERTS, EXPERT_FF, D), EXPERT_FF ** -0.5),
    }


def reference(x, meta_tokens, attn_norm, ffn_norm, final_norm, w_in, w_out, mla_q_norm, mla_kv_norm, mla_w_uq, mla_w_ukv, rwkv_mu, rwkv_w0, rwkv_w2, rwkv_a0, rwkv_a2, rwkv_g2, rwkv_k_k, rwkv_k_a, rwkv_r_k, rwkv_ln_w, rwkv_ln_b, rwkv_v0, rwkv_v1, rwkv_v2, ffn_w_gate, ffn_w_up, ffn_w_down, moe_router, moe_router_bias, moe_w_gate, moe_w_up, moe_w_down):
    B = x.shape[0]
    meta = jnp.broadcast_to(meta_tokens[None].astype(x.dtype), (B, N_META, D_MODEL))
    h = jnp.concatenate([meta, x], axis=1)
    v_first = None
    for i in range(DEPTH):
        u = rms_norm(h, attn_norm[i])
        proj = u @ w_in[i]
        y_mla = mla_mixer(proj[..., :MLA_COLS], mla_q_norm[i], mla_kv_norm[i], mla_w_uq[i], mla_w_ukv[i])
        v_res = None if i == 0 else (rwkv_v0[i - 1], rwkv_v1[i - 1], rwkv_v2[i - 1])
        y_rwkv, v_first = rwkv7_mixer(proj[..., MLA_COLS:], rwkv_mu[i], rwkv_w0[i], rwkv_w2[i], rwkv_a0[i], rwkv_a2[i], rwkv_g2[i], rwkv_k_k[i], rwkv_k_a[i], rwkv_r_k[i], rwkv_ln_w[i], rwkv_ln_b[i], v_first, v_res)
        h = h + jnp.concatenate([y_mla, y_rwkv], axis=-1) @ w_out[i]
        u = rms_norm(h, ffn_norm[i])
        j = i // 2
        if i % 2 == 0:
            f = swiglu(u, ffn_w_gate[j], ffn_w_up[j], ffn_w_down[j])
        else:
            f = moe_swiglu(u, moe_router[j], moe_router_bias[j], moe_w_gate[j], moe_w_up[j], moe_w_down[j])
        h = h + f
    return rms_norm(h, final_norm)[:, N_META:]
```

```python
import functools

import numpy as np
import jax
import jax.numpy as jnp
from jax import lax
from jax.experimental import pallas as pl
from jax.experimental.pallas import tpu as pltpu

F32 = jnp.float32
BF16 = jnp.bfloat16

D_MODEL = 1024
CHUNK = 64
N_META = 16
RMS_EPS = 1e-6

MLA_HEADS = 4
MLA_NOPE = 128
MLA_ROPE = 64
MLA_V = 128
MLA_Q_LORA = 512
MLA_KV_LORA = 256
ROPE_THETA = 10000.0
MLA_WIDTH = MLA_HEADS * MLA_V
MLA_QK_PAD = 256
MLA_COLS_PAD = MLA_Q_LORA + MLA_KV_LORA + 2 * MLA_ROPE

RWKV_WIDTH = 512
RWKV_HEAD = 64
RWKV_HEADS = RWKV_WIDTH // RWKV_HEAD
RWKV_PAIRS = RWKV_HEADS // 2
W_LORA = 64
A_LORA = 64
V_LORA = 32
G_LORA = 128
GN_EPS = 64e-5
RWKV_COLS = 3 * RWKV_WIDTH + W_LORA + A_LORA + G_LORA
SCAN_CHUNK = 64

N_EXPERTS = 8
TOP_K = 2
MOE_BLOCK = 256

LANE = 128
SEQ_TILE = 384
ROW_TILE = 512
VMEM_LIMIT = 56 * 1024 * 1024

HIGHEST = lax.Precision.HIGHEST


def _cparams(sem, vmem=VMEM_LIMIT):
    return pltpu.CompilerParams(dimension_semantics=sem, vmem_limit_bytes=vmem)


def _rms(x, g):
    return x * lax.rsqrt(jnp.mean(x * x, -1, keepdims=True) + RMS_EPS) * g


def _sigmoid(x):
    return 1.0 / (1.0 + jnp.exp(-x))


def _split_dot(x, w):
    hi = x.astype(BF16)
    lo = (x - hi.astype(F32)).astype(BF16)
    return jnp.dot(hi, w, preferred_element_type=F32) + jnp.dot(lo, w, preferred_element_type=F32)


def _const_spec(shape):
    nd = len(shape)
    return pl.BlockSpec(shape, lambda *_: (0,) * nd)


def _norm_inproj_kernel(h_ref, g_ref, wr_ref, wm_ref, pr_ref, pm_ref):
    u = _rms(h_ref[...], g_ref[...]).astype(BF16)
    pr_ref[...] = jnp.dot(u, wr_ref[...], preferred_element_type=F32)
    pm_ref[...] = jnp.dot(u, wm_ref[...], preferred_element_type=F32)


def _norm_inproj(h, g, w_r, w_m):
    tp = h.shape[0]
    return pl.pallas_call(
        _norm_inproj_kernel,
        grid=(tp // ROW_TILE,),
        in_specs=[pl.BlockSpec((ROW_TILE, D_MODEL), lambda i: (i, 0)),
                  _const_spec((1, D_MODEL)),
                  _const_spec(w_r.shape), _const_spec(w_m.shape)],
        out_specs=[pl.BlockSpec((ROW_TILE, w_r.shape[1]), lambda i: (i, 0)),
                   pl.BlockSpec((ROW_TILE, w_m.shape[1]), lambda i: (i, 0))],
        out_shape=[jax.ShapeDtypeStruct((tp, w_r.shape[1]), F32),
                   jax.ShapeDtypeStruct((tp, w_m.shape[1]), F32)],
        compiler_params=_cparams(("parallel",)),
        name="norm_inproj",
    )(h, g, w_r, w_m)


def _mla_up_kernel(pm_ref, qn_ref, kvn_ref, wq_ref, wkv_ref, cq_ref, sq_ref, ck_ref, sk_ref,
                   q_ref, k_ref, v_ref):
    pm = pm_ref[0]
    c_q = pm[:, :MLA_Q_LORA]
    c_kv = pm[:, MLA_Q_LORA:MLA_Q_LORA + MLA_KV_LORA]
    k_r = pm[:, MLA_Q_LORA + MLA_KV_LORA:]
    q = jnp.dot(_rms(c_q, qn_ref[...]).astype(BF16), wq_ref[...], preferred_element_type=F32)
    kv = jnp.dot(_rms(c_kv, kvn_ref[...]).astype(BF16), wkv_ref[...], preferred_element_type=F32)
    k_rope = (k_r * ck_ref[...] + pltpu.roll(k_r, MLA_ROPE, 1) * sk_ref[...]).astype(BF16)
    cq = cq_ref[...]
    sq = sq_ref[...]
    for hd in range(MLA_HEADS):
        qh = q[:, hd * MLA_QK_PAD:(hd + 1) * MLA_QK_PAD]
        qh = qh * cq + pltpu.roll(qh, MLA_QK_PAD - MLA_ROPE, 1) * sq
        q_ref[0, :, hd * MLA_QK_PAD:(hd + 1) * MLA_QK_PAD] = qh.astype(BF16)
        k_ref[0, :, hd * MLA_QK_PAD:hd * MLA_QK_PAD + MLA_NOPE] = (
            kv[:, hd * MLA_NOPE:(hd + 1) * MLA_NOPE].astype(BF16))
        k_ref[0, :, hd * MLA_QK_PAD + MLA_NOPE:(hd + 1) * MLA_QK_PAD] = k_rope
    v_ref[0] = kv[:, MLA_HEADS * MLA_NOPE:].astype(BF16)


def _mla_up(pm, q_norm, kv_norm, wq, wkv, tabs):
    b, lp, _ = pm.shape
    cq, sq, ck, sk = tabs
    qk_w = MLA_HEADS * MLA_QK_PAD
    row = lambda bi, i: (bi, i, 0)
    tab = lambda bi, i: (i, 0)
    return pl.pallas_call(
        _mla_up_kernel,
        grid=(b, lp // SEQ_TILE),
        in_specs=[pl.BlockSpec((1, SEQ_TILE, MLA_COLS_PAD), row),
                  _const_spec((1, MLA_Q_LORA)), _const_spec((1, MLA_KV_LORA)),
                  _const_spec(wq.shape), _const_spec(wkv.shape),
                  pl.BlockSpec((SEQ_TILE, MLA_QK_PAD), tab), pl.BlockSpec((SEQ_TILE, MLA_QK_PAD), tab),
                  pl.BlockSpec((SEQ_TILE, LANE), tab), pl.BlockSpec((SEQ_TILE, LANE), tab)],
        out_specs=[pl.BlockSpec((1, SEQ_TILE, qk_w), row),
                   pl.BlockSpec((1, SEQ_TILE, qk_w), row),
                   pl.BlockSpec((1, SEQ_TILE, MLA_WIDTH), row)],
        out_shape=[jax.ShapeDtypeStruct((b, lp, qk_w), BF16),
                   jax.ShapeDtypeStruct((b, lp, qk_w), BF16),
                   jax.ShapeDtypeStruct((b, lp, MLA_WIDTH), BF16)],
        compiler_params=_cparams(("parallel", "parallel")),
        name="mla_up",
    )(pm, q_norm, kv_norm, wq, wkv, cq, sq, ck, sk)


def _rope_tables(lp):
    half = MLA_ROPE // 2
    inv = (np.float32(ROPE_THETA) ** (-np.arange(half, dtype=np.float32) / np.float32(half))).astype(np.float32)
    ang = (np.arange(lp, dtype=np.float32)[:, None] * inv[None, :]).astype(np.float64)
    cos = np.concatenate([np.cos(ang), np.cos(ang)], -1)
    sin = np.concatenate([np.sin(ang), np.sin(ang)], -1)
    scale = (MLA_NOPE + MLA_ROPE) ** -0.5
    zeros = np.zeros((lp, MLA_ROPE))
    cq = np.concatenate([np.full((lp, MLA_NOPE), scale), cos * scale, zeros], -1)
    sq = np.concatenate([np.zeros((lp, MLA_NOPE)), sin * scale, zeros], -1)
    ck = np.concatenate([cos, zeros], -1)
    sk = np.concatenate([sin, zeros], -1)
    return tuple(jnp.asarray(t, F32) for t in (cq, sq, ck, sk))


def _chunk_id(pos):
    return jnp.where(pos < N_META, 0, 1 + ((pos - N_META) >> 6))


def _flash_kernel(q_ref, k_ref, v_ref, o_ref, m_sc, l_sc, acc_sc, *, n_valid, n_kv):
    i = pl.program_id(2)
    tq = q_ref.shape[1]
    q = q_ref[0]
    m_sc[...] = jnp.full(m_sc.shape, -jnp.inf, F32)
    l_sc[...] = jnp.zeros(l_sc.shape, F32)
    acc_sc[...] = jnp.zeros(acc_sc.shape, F32)
    q_cid = _chunk_id(i * tq + lax.broadcasted_iota(jnp.int32, (tq, 1), 0))

    def body(j, carry):
        start = pl.multiple_of(j * tq, tq)
        ks = k_ref[0, pl.ds(start, tq), :]
        vs = v_ref[0, pl.ds(start, tq), :]
        s = lax.dot_general(q, ks, (((1,), (1,)), ((), ())), preferred_element_type=F32)
        k_pos = j * tq + lax.broadcasted_iota(jnp.int32, (1, tq), 1)
        k_cid = jnp.where(k_pos < n_valid, _chunk_id(k_pos), jnp.int32(2 ** 30))
        s = jnp.where(k_cid <= q_cid, s, -jnp.inf)
        m_prev = m_sc[...]
        m_new = jnp.maximum(m_prev, jnp.max(s, -1, keepdims=True))
        alpha = jnp.exp(m_prev - m_new)
        p = jnp.exp(s - m_new)
        l_sc[...] = alpha * l_sc[...] + jnp.sum(p, -1, keepdims=True)
        acc_sc[...] = alpha * acc_sc[...] + jnp.dot(p.astype(BF16), vs, preferred_element_type=F32)
        m_sc[...] = m_new
        return carry

    lax.fori_loop(0, jnp.minimum(i + 2, n_kv), body, 0)
    o_ref[0] = (acc_sc[...] / l_sc[...]).astype(o_ref.dtype)


def _flash(q, k, v, n_valid):
    b, lp, _ = q.shape
    n_q = lp // SEQ_TILE
    return pl.pallas_call(
        functools.partial(_flash_kernel, n_valid=n_valid, n_kv=n_q),
        grid=(b, MLA_HEADS, n_q),
        in_specs=[pl.BlockSpec((1, SEQ_TILE, MLA_QK_PAD), lambda bi, hd, i: (bi, i, hd)),
                  pl.BlockSpec((1, lp, MLA_QK_PAD), lambda bi, hd, i: (bi, 0, hd)),
                  pl.BlockSpec((1, lp, MLA_V), lambda bi, hd, i: (bi, 0, hd))],
        out_specs=pl.BlockSpec((1, SEQ_TILE, MLA_V), lambda bi, hd, i: (bi, i, hd)),
        out_shape=jax.ShapeDtypeStruct((b, lp, MLA_WIDTH), BF16),
        scratch_shapes=[pltpu.VMEM((SEQ_TILE, 1), F32), pltpu.VMEM((SEQ_TILE, 1), F32),
                        pltpu.VMEM((SEQ_TILE, MLA_V), F32)],
        compiler_params=_cparams(("parallel", "parallel", "arbitrary")),
        name="mla_flash",
    )(q, k, v)


def _rwkv_prep_kernel(*refs, has_vres):
    (p_ref, prev_ref, mu_ref, w0_ref, w2_ref, a0_ref, a2_ref, g2_ref, kk_ref, ka_ref, rk_ref,
     bd_ref) = refs[:12]
    n_in = 16 if has_vres else 12
    r_out, k_out, v_out, kk_out, bb_out, lw_out, bonus_out, g_out = refs[n_in:]
    c = RWKV_WIDTH
    p = p_ref[0]
    tm = p.shape[0]
    first = pl.program_id(1) == 0
    prev_last = jnp.where(first, 0.0, prev_ref[0, 7:8, :])
    row = lax.broadcasted_iota(jnp.int32, (tm, 1), 0)
    prev = jnp.where(row == 0, prev_last, pltpu.roll(p, 1, 0))
    ps = p + (prev - p) * mu_ref[...]
    r = ps[:, :c]
    k = ps[:, c:2 * c]
    v = ps[:, 2 * c:3 * c]
    xwa = ps[:, 3 * c:3 * c + W_LORA + A_LORA]
    xg = ps[:, 3 * c + W_LORA + A_LORA:]
    zw = w0_ref[...] + jnp.dot(jnp.tanh(xwa).astype(BF16), w2_ref[...], preferred_element_type=F32)
    nz = -zw
    softplus = jnp.maximum(nz, 0.0) + jnp.log(1.0 + jnp.exp(-jnp.abs(nz)))
    log_decay = -jnp.exp(-softplus - 0.5)
    if has_vres:
        vf_ref, v0_ref, v1_ref, v2_ref = refs[12:16]
        lo = jnp.dot(v.astype(BF16), v1_ref[...], preferred_element_type=F32)
        gate = _sigmoid(v0_ref[...] + jnp.dot(lo.astype(BF16), v2_ref[...], preferred_element_type=F32))
        v = v + (vf_ref[0] - v) * gate
    a = _sigmoid(a0_ref[...] + jnp.dot(xwa.astype(BF16), a2_ref[...], preferred_element_type=F32))
    g = jnp.dot(_sigmoid(xg).astype(BF16), g2_ref[...], preferred_element_type=F32)
    bd = bd_ref[...]
    kk = k * kk_ref[...]
    kk = kk * lax.rsqrt(jnp.maximum(_split_dot(kk * kk, bd), 1e-24))
    k = k * (1.0 + (a - 1.0) * ka_ref[...])
    bonus = _split_dot(r * k * rk_ref[...], bd) * v
    r_out[0] = r
    k_out[0] = k
    v_out[0] = v
    kk_out[0] = kk
    bb_out[0] = kk * a
    lw_out[0] = log_decay
    bonus_out[0] = bonus
    g_out[0] = g


def _rwkv_prep(pr, prm, v_first):
    b, lp, cols = pr.shape
    has_vres = v_first is not None
    c = RWKV_WIDTH
    row = lambda bi, i: (bi, i, 0)
    prev = lambda bi, i: (bi, jnp.maximum(i * (SEQ_TILE // 8) - 1, 0), 0)
    names = ["mu", "w0", "w2", "a0", "a2", "g2", "k_k", "k_a", "r_k", "bd"]
    args = [pr, pr] + [prm[n] for n in names]
    in_specs = [pl.BlockSpec((1, SEQ_TILE, cols), row), pl.BlockSpec((1, 8, cols), prev)]
    in_specs += [_const_spec(prm[n].shape) for n in names]
    if has_vres:
        args += [v_first, prm["v0"], prm["v1"], prm["v2"]]
        in_specs += [pl.BlockSpec((1, SEQ_TILE, c), row)]
        in_specs += [_const_spec(prm[n].shape) for n in ("v0", "v1", "v2")]
    out_spec = pl.BlockSpec((1, SEQ_TILE, c), row)
    return pl.pallas_call(
        functools.partial(_rwkv_prep_kernel, has_vres=has_vres),
        grid=(b, lp // SEQ_TILE),
        in_specs=in_specs,
        out_specs=[out_spec] * 8,
        out_shape=[jax.ShapeDtypeStruct((b, lp, c), F32)] * 8,
        compiler_params=_cparams(("parallel", "parallel")),
        name="rwkv_prep",
    )(*args)


def _scan_pair(r, k, v, kk, bb, lw, s_prev, tri, strict, incl, eye):
    c = r.shape[0]
    lw_hi = lw.astype(BF16)
    lw_lo = (lw - lw_hi.astype(F32)).astype(BF16)
    cum = (jnp.dot(tri, lw_hi, preferred_element_type=F32)
           + jnp.dot(tri, lw_lo, preferred_element_type=F32))
    w_incl = jnp.exp(cum)
    w_inv = jnp.exp(-cum)
    w_excl = jnp.exp(cum - lw)
    lane = lax.broadcasted_iota(jnp.int32, (1, LANE), 1)
    head0 = lane < RWKV_HEAD

    def stack(x):
        return jnp.concatenate([jnp.where(head0, x, 0.0), jnp.where(head0, 0.0, x)], axis=0)

    a_hat = stack(-kk * w_excl)
    r_hat = stack(r * w_incl)
    b_hat = stack(bb * w_inv)
    k_hat = stack(k * w_inv)
    v_st = stack(v)
    nt = (((1,), (1,)), ((), ()))
    tn = (((0,), (0,)), ((), ()))
    ab = lax.dot_general(a_hat, b_hat, nt, precision=HIGHEST, preferred_element_type=F32)
    ak = lax.dot_general(a_hat, k_hat, nt, precision=HIGHEST, preferred_element_type=F32)
    rb = lax.dot_general(r_hat, b_hat, nt, precision=HIGHEST, preferred_element_type=F32)
    rk = lax.dot_general(r_hat, k_hat, nt, precision=HIGHEST, preferred_element_type=F32)
    l_mat = jnp.where(strict, ab, 0.0)
    a_k = jnp.where(strict, ak, 0.0)
    q_b = jnp.where(incl, rb, 0.0)
    q_k = jnp.where(incl, rk, 0.0)
    t_inv = eye + l_mat
    pw = l_mat
    for _ in range(5):
        pw = jnp.dot(pw, pw, precision=HIGHEST, preferred_element_type=F32)
        t_inv = t_inv + jnp.dot(t_inv, pw, precision=HIGHEST, preferred_element_type=F32)
    rhs = (lax.dot_general(a_hat, s_prev, nt, precision=HIGHEST, preferred_element_type=F32)
           + jnp.dot(a_k, v_st, precision=HIGHEST, preferred_element_type=F32))
    u = jnp.dot(t_inv, rhs, precision=HIGHEST, preferred_element_type=F32)
    y_st = (lax.dot_general(r_hat, s_prev, nt, precision=HIGHEST, preferred_element_type=F32)
            + jnp.dot(q_b, u, precision=HIGHEST, preferred_element_type=F32)
            + jnp.dot(q_k, v_st, precision=HIGHEST, preferred_element_type=F32))
    y = y_st[:c] + y_st[c:]
    w_end = w_incl[c - 1:c, :]
    s_new = (s_prev
             + lax.dot_general(u, b_hat, tn, precision=HIGHEST, preferred_element_type=F32)
             + lax.dot_general(v_st, k_hat, tn, precision=HIGHEST, preferred_element_type=F32)) * w_end
    return y, s_new


def _rwkv_scan_kernel(r_ref, k_ref, v_ref, kk_ref, bb_ref, lw_ref, y_ref, s_sc):
    @pl.when(pl.program_id(1) == 0)
    def _():
        s_sc[...] = jnp.zeros(s_sc.shape, F32)

    c = r_ref.shape[1]
    ri = lax.broadcasted_iota(jnp.int32, (c, c), 0)
    ci = lax.broadcasted_iota(jnp.int32, (c, c), 1)
    tri = jnp.where(ri >= ci, 1.0, 0.0).astype(BF16)
    r2 = lax.broadcasted_iota(jnp.int32, (2 * c, 2 * c), 0)
    c2 = lax.broadcasted_iota(jnp.int32, (2 * c, 2 * c), 1)
    strict = (r2 & (c - 1)) > (c2 & (c - 1))
    incl = (r2 & (c - 1)) >= (c2 & (c - 1))
    eye = jnp.where(r2 == c2, 1.0, 0.0).astype(F32)
    for pr in range(RWKV_PAIRS):
        sl = slice(pr * LANE, (pr + 1) * LANE)
        y, s_new = _scan_pair(r_ref[0, :, sl], k_ref[0, :, sl], v_ref[0, :, sl], kk_ref[0, :, sl],
                              bb_ref[0, :, sl], lw_ref[0, :, sl], s_sc[pr], tri, strict, incl, eye)
        y_ref[0, :, sl] = y
        s_sc[pr] = s_new


def _rwkv_scan(r, k, v, kk, bb, lw):
    b, lp, c = r.shape
    spec = pl.BlockSpec((1, SCAN_CHUNK, c), lambda bi, i: (bi, i, 0))
    return pl.pallas_call(
        _rwkv_scan_kernel,
        grid=(b, lp // SCAN_CHUNK),
        in_specs=[spec] * 6,
        out_specs=spec,
        out_shape=jax.ShapeDtypeStruct((b, lp, c), F32),
        scratch_shapes=[pltpu.VMEM((RWKV_PAIRS, LANE, LANE), F32)],
        compiler_params=_cparams(("parallel", "arbitrary")),
        name="rwkv_scan",
    )(r, k, v, kk, bb, lw)


def _outproj_kernel(ys_ref, bonus_ref, g_ref, ymla_ref, lnw_ref, lnb_ref, bd_ref, wo_ref, h_ref, o_ref):
    y = ys_ref[...]
    bd = bd_ref[...]
    inv_n = 1.0 / RWKV_HEAD
    d = y - _split_dot(y, bd) * inv_n
    var = _split_dot(d * d, bd) * inv_n
    yn = d * lax.rsqrt(var + GN_EPS) * lnw_ref[...] + lnb_ref[...]
    yr = ((yn + bonus_ref[...]) * g_ref[...]).astype(BF16)
    o_ref[...] = (h_ref[...]
                  + jnp.dot(ymla_ref[...], wo_ref[:MLA_WIDTH, :], preferred_element_type=F32)
                  + jnp.dot(yr, wo_ref[MLA_WIDTH:, :], preferred_element_type=F32))


def _outproj(ys, bonus, g, ymla, ln_w, ln_b, bd, wo, h):
    tp = h.shape[0]
    c = RWKV_WIDTH
    rc = pl.BlockSpec((ROW_TILE, c), lambda i: (i, 0))
    rd = pl.BlockSpec((ROW_TILE, D_MODEL), lambda i: (i, 0))
    return pl.pallas_call(
        _outproj_kernel,
        grid=(tp // ROW_TILE,),
        in_specs=[rc, rc, rc, rc, _const_spec((1, c)), _const_spec((1, c)), _const_spec(bd.shape),
                  _const_spec(wo.shape), rd],
        out_specs=rd,
        out_shape=jax.ShapeDtypeStruct((tp, D_MODEL), F32),
        compiler_params=_cparams(("parallel",)),
        name="outproj",
    )(ys, bonus, g, ymla, ln_w, ln_b, bd, wo, h)


def _ffn_chunks(u, wg_ref, wu_ref, wd_ref, idx, ff, chunk):
    acc = None
    for c0 in range(0, ff, chunk):
        sl = slice(c0, c0 + chunk)
        gate = jnp.dot(u, wg_ref[idx + (slice(None), sl)], preferred_element_type=F32)
        up = jnp.dot(u, wu_ref[idx + (slice(None), sl)], preferred_element_type=F32)
        act = (gate * _sigmoid(gate) * up).astype(BF16)
        part = jnp.dot(act, wd_ref[idx + (sl, slice(None))], preferred_element_type=F32)
        acc = part if acc is None else acc + part
    return acc


def _dense_ffn_kernel(h_ref, g_ref, wg_ref, wu_ref, wd_ref, o_ref, *, ff, chunk):
    h = h_ref[...]
    u = _rms(h, g_ref[...]).astype(BF16)
    o_ref[...] = h + _ffn_chunks(u, wg_ref, wu_ref, wd_ref, (), ff, chunk)


def _dense_ffn(h, g, wg, wu, wd):
    tp = h.shape[0]
    ff = wg.shape[1]
    rd = pl.BlockSpec((ROW_TILE, D_MODEL), lambda i: (i, 0))
    once = lambda shape: pl.BlockSpec(shape, lambda i: (0, 0), pipeline_mode=pl.Buffered(1))
    return pl.pallas_call(
        functools.partial(_dense_ffn_kernel, ff=ff, chunk=256),
        grid=(tp // ROW_TILE,),
        in_specs=[rd, _const_spec((1, D_MODEL)), once(wg.shape), once(wu.shape), once(wd.shape)],
        out_specs=rd,
        out_shape=jax.ShapeDtypeStruct((tp, D_MODEL), F32),
        compiler_params=_cparams(("parallel",)),
        name="dense_ffn",
    )(h, g, wg, wu, wd)


def _router_kernel(h_ref, g_ref, wr_ref, br_ref, u_ref, meta_ref, cnt_ref, carry_sc, *, n_valid):
    @pl.when((pl.program_id(0) == 0) & (pl.program_id(1) == 0))
    def _():
        carry_sc[...] = jnp.zeros(carry_sc.shape, F32)

    tm = h_ref.shape[1]
    u = _rms(h_ref[0], g_ref[...])
    u_ref[0] = u
    logits = jnp.dot(u, wr_ref[...], precision=HIGHEST, preferred_element_type=F32) + br_ref[...]
    lane = lax.broadcasted_iota(jnp.int32, (tm, LANE), 1).astype(F32)
    logits = jnp.where(lane < N_EXPERTS, logits, -jnp.inf)
    top0 = jnp.max(logits, -1, keepdims=True)
    e0 = jnp.min(jnp.where(logits == top0, lane, float(LANE)), -1, keepdims=True)
    rest = jnp.where(lane == e0, -jnp.inf, logits)
    top1 = jnp.max(rest, -1, keepdims=True)
    e1 = jnp.min(jnp.where(rest == top1, lane, float(LANE)), -1, keepdims=True)
    ex = jnp.exp(top1 - top0)
    g0 = 1.0 / (1.0 + ex)
    g1 = ex / (1.0 + ex)
    pos = pl.program_id(1) * tm + lax.broadcasted_iota(jnp.int32, (tm, 1), 0)
    valid = pos < n_valid
    oh0 = jnp.where((lane == e0) & valid, 1.0, 0.0)
    oh1 = jnp.where((lane == e1) & valid, 1.0, 0.0)
    ri = lax.broadcasted_iota(jnp.int32, (tm, tm), 0)
    ci = lax.broadcasted_iota(jnp.int32, (tm, tm), 1)
    before = jnp.where(ri > ci, 1.0, 0.0).astype(BF16)
    both = oh0 + oh1
    seen = carry_sc[...] + jnp.dot(before, both.astype(BF16), preferred_element_type=F32)
    rank0 = jnp.sum(jnp.where(lane == e0, seen, 0.0), -1, keepdims=True)
    rank1 = jnp.sum(jnp.where(lane == e1, seen, 0.0), -1, keepdims=True)
    carry_sc[...] = carry_sc[...] + jnp.sum(both, 0, keepdims=True)
    cnt_ref[...] = jnp.broadcast_to(carry_sc[...], cnt_ref.shape)
    vf = jnp.where(valid, 1.0, 0.0)
    meta = jnp.where(lane == 0, e0, 0.0)
    meta = jnp.where(lane == 1, e1, meta)
    meta = jnp.where(lane == 2, rank0, meta)
    meta = jnp.where(lane == 3, rank1, meta)
    meta = jnp.where(lane == 4, g0 * vf, meta)
    meta = jnp.where(lane == 5, g1 * vf, meta)
    meta = jnp.where(lane == 6, vf, meta)
    meta_ref[0] = meta


def _router(h, g, wr, br, n_valid):
    b, lp, _ = h.shape
    row = lambda bi, i: (bi, i, 0)
    return pl.pallas_call(
        functools.partial(_router_kernel, n_valid=n_valid),
        grid=(b, lp // SEQ_TILE),
        in_specs=[pl.BlockSpec((1, SEQ_TILE, D_MODEL), row), _const_spec((1, D_MODEL)),
                  _const_spec(wr.shape), _const_spec(br.shape)],
        out_specs=[pl.BlockSpec((1, SEQ_TILE, D_MODEL), row), pl.BlockSpec((1, SEQ_TILE, LANE), row),
                   _const_spec((8, LANE))],
        out_shape=[jax.ShapeDtypeStruct((b, lp, D_MODEL), F32), jax.ShapeDtypeStruct((b, lp, LANE), F32),
                   jax.ShapeDtypeStruct((8, LANE), F32)],
        scratch_shapes=[pltpu.VMEM((1, LANE), F32)],
        compiler_params=_cparams(("arbitrary", "arbitrary")),
        name="moe_router",
    )(h, g, wr, br)


def _dispatch_kernel(s0_ref, s1_ref, u_ref, xb_in_ref, xb_ref, sem):
    del xb_in_ref
    tm = u_ref.shape[0]

    def copy(rw, slot):
        return pltpu.make_async_copy(u_ref.at[pl.ds(rw, 1), :], xb_ref.at[pl.ds(slot, 1), :], sem)

    def start(rw, carry):
        for s_ref in (s0_ref, s1_ref):
            slot = s_ref[0, 0, rw]

            @pl.when(slot >= 0)
            def _():
                copy(rw, slot).start()
        return carry

    def wait(rw, carry):
        for s_ref in (s0_ref, s1_ref):
            slot = s_ref[0, 0, rw]

            @pl.when(slot >= 0)
            def _():
                copy(rw, slot).wait()
        return carry

    lax.fori_loop(0, tm, start, 0)
    lax.fori_loop(0, tm, wait, 0)


def _dispatch(slot0, slot1, u, n_slots, tm):
    tp = u.shape[0]
    sspec = pl.BlockSpec((1, 1, tm), lambda i: (i, 0, 0), memory_space=pltpu.SMEM)
    xb0 = jnp.zeros((n_slots, D_MODEL), F32)
    return pl.pallas_call(
        _dispatch_kernel,
        grid=(tp // tm,),
        in_specs=[sspec, sspec, pl.BlockSpec((tm, D_MODEL), lambda i: (i, 0)),
                  pl.BlockSpec(memory_space=pl.ANY)],
        out_specs=pl.BlockSpec(memory_space=pl.ANY),
        out_shape=jax.ShapeDtypeStruct((n_slots, D_MODEL), F32),
        scratch_shapes=[pltpu.SemaphoreType.DMA(())],
        input_output_aliases={3: 0},
        compiler_params=_cparams(("arbitrary",)),
        name="moe_dispatch",
    )(slot0.reshape(tp // tm, 1, tm), slot1.reshape(tp // tm, 1, tm), u, xb0)


def _expert_kernel(be_ref, x_ref, wg_ref, wu_ref, wd_ref, o_ref, *, ff, chunk):
    del be_ref
    u = x_ref[...].astype(BF16)
    o_ref[...] = _ffn_chunks(u, wg_ref, wu_ref, wd_ref, (0,), ff, chunk)


def _experts(block_e, xb, wg, wu, wd):
    n_slots = xb.shape[0]
    ff = wg.shape[2]
    rows = pl.BlockSpec((MOE_BLOCK, D_MODEL), lambda i, be: (i, 0))
    return pl.pallas_call(
        functools.partial(_expert_kernel, ff=ff, chunk=512),
        grid_spec=pltpu.PrefetchScalarGridSpec(
            num_scalar_prefetch=1,
            grid=(n_slots // MOE_BLOCK,),
            in_specs=[rows,
                      pl.BlockSpec((1, D_MODEL, ff), lambda i, be: (be[i], 0, 0)),
                      pl.BlockSpec((1, D_MODEL, ff), lambda i, be: (be[i], 0, 0)),
                      pl.BlockSpec((1, ff, D_MODEL), lambda i, be: (be[i], 0, 0))],
            out_specs=rows),
        out_shape=jax.ShapeDtypeStruct((n_slots, D_MODEL), F32),
        compiler_params=_cparams(("arbitrary",), 60 * 1024 * 1024),
        name="moe_experts",
    )(block_e, xb, wg, wu, wd)


def _combine_kernel(s0_ref, s1_ref, h_ref, gates_ref, fn_ref, yb_ref, o_ref, y0_sc, y1_sc, sem):
    tm = h_ref.shape[0]

    def copy(rw, s_ref, buf):
        return pltpu.make_async_copy(yb_ref.at[pl.ds(s_ref[0, 0, rw], 1), :], buf.at[pl.ds(rw, 1), :], sem)

    def start(rw, carry):
        copy(rw, s0_ref, y0_sc).start()
        copy(rw, s1_ref, y1_sc).start()
        return carry

    def wait(rw, carry):
        copy(rw, s0_ref, y0_sc).wait()
        copy(rw, s1_ref, y1_sc).wait()
        return carry

    lax.fori_loop(0, tm, start, 0)
    lax.fori_loop(0, tm, wait, 0)
    gates = gates_ref[...]
    f = y0_sc[...] * gates[:, 4:5] + y1_sc[...] * gates[:, 5:6]
    o_ref[...] = _rms(h_ref[...] + f, fn_ref[...])


def _combine(slot0, slot1, h, meta, fn, yb, tm):
    tp = h.shape[0]
    sspec = pl.BlockSpec((1, 1, tm), lambda i: (i, 0, 0), memory_space=pltpu.SMEM)
    rd = pl.BlockSpec((tm, D_MODEL), lambda i: (i, 0))
    return pl.pallas_call(
        _combine_kernel,
        grid=(tp // tm,),
        in_specs=[sspec, sspec, rd, pl.BlockSpec((tm, LANE), lambda i: (i, 0)), _const_spec((1, D_MODEL)),
                  pl.BlockSpec(memory_space=pl.ANY)],
        out_specs=rd,
        out_shape=jax.ShapeDtypeStruct((tp, D_MODEL), F32),
        scratch_shapes=[pltpu.VMEM((tm, D_MODEL), F32), pltpu.VMEM((tm, D_MODEL), F32),
                        pltpu.SemaphoreType.DMA(())],
        compiler_params=_cparams(("arbitrary",)),
        name="moe_combine",
    )(slot0.reshape(tp // tm, 1, tm), slot1.reshape(tp // tm, 1, tm), h, meta, fn, yb)


def _final_norm_kernel(h_ref, g_ref, o_ref):
    o_ref[...] = _rms(h_ref[...], g_ref[...])


def _final_norm(h, g):
    tp = h.shape[0]
    rd = pl.BlockSpec((ROW_TILE, D_MODEL), lambda i: (i, 0))
    return pl.pallas_call(
        _final_norm_kernel,
        grid=(tp // ROW_TILE,),
        in_specs=[rd, _const_spec((1, D_MODEL))],
        out_specs=rd,
        out_shape=jax.ShapeDtypeStruct((tp, D_MODEL), F32),
        compiler_params=_cparams(("parallel",)),
        name="final_norm",
    )(h, g)


def _rot_cols(w):
    half = MLA_ROPE // 2
    return jnp.concatenate([-w[..., half:], w[..., :half]], -1)


def _prep_inproj(w_in):
    m_q = w_in[:, :MLA_Q_LORA + MLA_KV_LORA]
    k_rope = w_in[:, MLA_Q_LORA + MLA_KV_LORA:MLA_Q_LORA + MLA_KV_LORA + MLA_ROPE]
    w_m = jnp.concatenate([m_q, k_rope, _rot_cols(k_rope)], -1)
    w_r = w_in[:, MLA_Q_LORA + MLA_KV_LORA + MLA_ROPE:]
    return w_r.astype(BF16), w_m.astype(BF16)


def _prep_wq(w_uq):
    w = w_uq.reshape(MLA_Q_LORA, MLA_HEADS, MLA_NOPE + MLA_ROPE)
    rope = w[..., MLA_NOPE:]
    return jnp.concatenate([w, _rot_cols(rope)], -1).reshape(MLA_Q_LORA, MLA_HEADS * MLA_QK_PAD).astype(BF16)


def _prep_wkv(w_ukv):
    w = w_ukv.reshape(MLA_KV_LORA, MLA_HEADS, MLA_NOPE + MLA_V)
    k_nope = w[..., :MLA_NOPE].reshape(MLA_KV_LORA, MLA_HEADS * MLA_NOPE)
    v = w[..., MLA_NOPE:].reshape(MLA_KV_LORA, MLA_HEADS * MLA_V)
    return jnp.concatenate([k_nope, v], -1).astype(BF16)


def _row(x):
    return x.reshape(1, -1).astype(F32)


def _block_diag_ones():
    idx = np.arange(RWKV_WIDTH) // RWKV_HEAD
    return jnp.asarray(idx[:, None] == idx[None, :], BF16)


def kernel(x, meta_tokens, attn_norm, ffn_norm, final_norm, w_in, w_out, mla_q_norm, mla_kv_norm, mla_w_uq, mla_w_ukv, rwkv_mu, rwkv_w0, rwkv_w2, rwkv_a0, rwkv_a2, rwkv_g2, rwkv_k_k, rwkv_k_a, rwkv_r_k, rwkv_ln_w, rwkv_ln_b, rwkv_v0, rwkv_v1, rwkv_v2, ffn_w_gate, ffn_w_up, ffn_w_down, moe_router, moe_router_bias, moe_w_gate, moe_w_up, moe_w_down):
    b, seq, d = x.shape
    depth = attn_norm.shape[0]
    n_valid = N_META + seq
    lp = -(-n_valid // SEQ_TILE) * SEQ_TILE
    tp = b * lp
    assert d == D_MODEL and tp % ROW_TILE == 0 and lp % SCAN_CHUNK == 0

    meta = jnp.broadcast_to(meta_tokens[None].astype(x.dtype), (b, N_META, d))
    h = jnp.concatenate([meta, x, jnp.zeros((b, lp - n_valid, d), x.dtype)], axis=1).reshape(tp, d)

    tabs = _rope_tables(lp)
    bd = _block_diag_ones()
    zeros_wa = jnp.zeros((W_LORA, RWKV_WIDTH), F32)
    v_first = None
    for i in range(depth):
        w_r, w_m = _prep_inproj(w_in[i])
        pr, pm = _norm_inproj(h, _row(attn_norm[i]), w_r, w_m)

        q, k, v = _mla_up(pm.reshape(b, lp, -1), _row(mla_q_norm[i]), _row(mla_kv_norm[i]),
                          _prep_wq(mla_w_uq[i]), _prep_wkv(mla_w_ukv[i]), tabs)
        y_mla = _flash(q, k, v, n_valid)

        prm = {
            "mu": _row(rwkv_mu[i]), "w0": _row(rwkv_w0[i]), "a0": _row(rwkv_a0[i]),
            "w2": jnp.concatenate([rwkv_w2[i], zeros_wa], 0).astype(BF16),
            "a2": jnp.concatenate([zeros_wa, rwkv_a2[i]], 0).astype(BF16),
            "g2": rwkv_g2[i].astype(BF16),
            "k_k": _row(rwkv_k_k[i]), "k_a": _row(rwkv_k_a[i]), "r_k": _row(rwkv_r_k[i]), "bd": bd,
        }
        if i > 0:
            prm["v0"] = _row(rwkv_v0[i - 1])
            prm["v1"] = jnp.pad(rwkv_v1[i - 1], ((0, 0), (0, LANE - V_LORA))).astype(BF16)
            prm["v2"] = jnp.pad(rwkv_v2[i - 1], ((0, LANE - V_LORA), (0, 0))).astype(BF16)
        r_, k_, v_, kk_, bb_, lw_, bonus, gate = _rwkv_prep(pr.reshape(b, lp, -1), prm, v_first)
        if i == 0:
            v_first = v_
        ys = _rwkv_scan(r_, k_, v_, kk_, bb_, lw_)

        flat = lambda t: t.reshape(tp, -1)
        h = _outproj(flat(ys), flat(bonus), flat(gate), flat(y_mla), _row(rwkv_ln_w[i]), _row(rwkv_ln_b[i]),
                     bd, w_out[i].astype(BF16), h)

        j = i // 2
        last = i == depth - 1
        if i % 2 == 0:
            h = _dense_ffn(h, _row(ffn_norm[i]), ffn_w_gate[j].astype(BF16), ffn_w_up[j].astype(BF16),
                           ffn_w_down[j].astype(BF16))
            if last:
                h = _final_norm(h, _row(final_norm))
        else:
            h = _moe(h, b, lp, n_valid, _row(ffn_norm[i]), moe_router[j], moe_router_bias[j],
                     moe_w_gate[j], moe_w_up[j], moe_w_down[j], _row(final_norm) if last else None)
            if not last:
                raise NotImplementedError("an MoE layer that is not the last layer")
    return h.reshape(b, lp, d)[:, N_META:n_valid]


def _moe(h, b, lp, n_valid, g, router, router_bias, w_gate, w_up, w_down, final_g):
    tp = b * lp
    wr = jnp.pad(router, ((0, 0), (0, LANE - N_EXPERTS))).astype(F32)
    br = jnp.pad(router_bias, (0, LANE - N_EXPERTS)).reshape(1, LANE).astype(F32)
    u, meta, cnt = _router(h.reshape(b, lp, D_MODEL), g, wr, br, n_valid)
    meta = meta.reshape(tp, LANE)

    counts = cnt[0, :N_EXPERTS].astype(jnp.int32)
    padded = (counts + MOE_BLOCK - 1) // MOE_BLOCK * MOE_BLOCK
    pend = jnp.cumsum(padded)
    pstart = pend - padded
    n_assign = b * n_valid * TOP_K
    n_blocks = (n_assign + N_EXPERTS * (MOE_BLOCK - 1)) // MOE_BLOCK + 1
    n_slots = n_blocks * MOE_BLOCK
    block_e = jnp.minimum(
        jnp.searchsorted(pend, jnp.arange(n_blocks, dtype=jnp.int32) * MOE_BLOCK, side="right"),
        N_EXPERTS - 1).astype(jnp.int32)
    valid = meta[:, 6] > 0.5
    e0 = meta[:, 0].astype(jnp.int32)
    e1 = meta[:, 1].astype(jnp.int32)
    slot0 = pstart[e0] + meta[:, 2].astype(jnp.int32)
    slot1 = pstart[e1] + meta[:, 3].astype(jnp.int32)
    d0 = jnp.where(valid, slot0, -1)
    d1 = jnp.where(valid, slot1, -1)
    c0 = jnp.where(valid, slot0, 0)
    c1 = jnp.where(valid, slot1, 0)

    xb = _dispatch(d0, d1, u.reshape(tp, D_MODEL), n_slots, LANE)
    yb = _experts(block_e, xb, w_gate.astype(BF16), w_up.astype(BF16), w_down.astype(BF16))
    return _combine(c0, c1, h, meta, final_g, yb, LANE)
```

```python
import functools

import numpy as np
import jax
import jax.numpy as jnp
from jax import lax
from jax.experimental import pallas as pl
from jax.experimental.pallas import tpu as pltpu

F32 = jnp.float32
BF16 = jnp.bfloat16

D_MODEL = 1024
CHUNK = 64
N_META = 16
RMS_EPS = 1e-6

MLA_HEADS = 4
MLA_NOPE = 128
MLA_ROPE = 64
MLA_V = 128
MLA_Q_LORA = 512
MLA_KV_LORA = 256
ROPE_THETA = 10000.0
MLA_WIDTH = MLA_HEADS * MLA_V
MLA_QK_PAD = 256
MLA_COLS_PAD = MLA_Q_LORA + MLA_KV_LORA + 2 * MLA_ROPE

RWKV_WIDTH = 512
RWKV_HEAD = 64
RWKV_HEADS = RWKV_WIDTH // RWKV_HEAD
RWKV_PAIRS = RWKV_HEADS // 2
W_LORA = 64
A_LORA = 64
V_LORA = 32
G_LORA = 128
GN_EPS = 64e-5
RWKV_COLS = 3 * RWKV_WIDTH + W_LORA + A_LORA + G_LORA
SCAN_CHUNK = 64
SCAN_STEP_CHUNKS = 3

N_EXPERTS = 8
TOP_K = 2
MOE_BLOCK = 256

LANE = 128
SEQ_TILE = 384
FLASH_TAIL = 512
ROW_TILE = 512
VMEM_LIMIT = 56 * 1024 * 1024

HIGHEST = lax.Precision.HIGHEST


def _cparams(sem, vmem=VMEM_LIMIT):
    return pltpu.CompilerParams(dimension_semantics=sem, vmem_limit_bytes=vmem)


def _rms(x, g):
    return x * lax.rsqrt(jnp.mean(x * x, -1, keepdims=True) + RMS_EPS) * g


def _sigmoid(x):
    return 1.0 / (1.0 + jnp.exp(-x))


def _split_dot(x, w):
    hi = x.astype(BF16)
    lo = (x - hi.astype(F32)).astype(BF16)
    return jnp.dot(hi, w, preferred_element_type=F32) + jnp.dot(lo, w, preferred_element_type=F32)


def _const_spec(shape):
    nd = len(shape)
    return pl.BlockSpec(shape, lambda *_: (0,) * nd)


def _norm_inproj_kernel(h_ref, g_ref, wr_ref, wm_ref, pr_ref, pm_ref):
    u = _rms(h_ref[...], g_ref[...]).astype(BF16)
    pr_ref[...] = jnp.dot(u, wr_ref[...], preferred_element_type=F32)
    pm_ref[...] = jnp.dot(u, wm_ref[...], preferred_element_type=F32)


def _norm_inproj(h, g, w_r, w_m):
    tp = h.shape[0]
    return pl.pallas_call(
        _norm_inproj_kernel,
        grid=(tp // ROW_TILE,),
        in_specs=[pl.BlockSpec((ROW_TILE, D_MODEL), lambda i: (i, 0)),
                  _const_spec((1, D_MODEL)),
                  _const_spec(w_r.shape), _const_spec(w_m.shape)],
        out_specs=[pl.BlockSpec((ROW_TILE, w_r.shape[1]), lambda i: (i, 0)),
                   pl.BlockSpec((ROW_TILE, w_m.shape[1]), lambda i: (i, 0))],
        out_shape=[jax.ShapeDtypeStruct((tp, w_r.shape[1]), F32),
                   jax.ShapeDtypeStruct((tp, w_m.shape[1]), F32)],
        compiler_params=_cparams(("parallel",)),
        name="norm_inproj",
    )(h, g, w_r, w_m)


def _mla_up_kernel(pm_ref, qn_ref, kvn_ref, wq_ref, wkv_ref, cq_ref, sq_ref, ck_ref, sk_ref,
                   q_ref, k_ref, v_ref):
    pm = pm_ref[0]
    c_q = pm[:, :MLA_Q_LORA]
    c_kv = pm[:, MLA_Q_LORA:MLA_Q_LORA + MLA_KV_LORA]
    k_r = pm[:, MLA_Q_LORA + MLA_KV_LORA:]
    q = jnp.dot(_rms(c_q, qn_ref[...]).astype(BF16), wq_ref[...], preferred_element_type=F32)
    kv = jnp.dot(_rms(c_kv, kvn_ref[...]).astype(BF16), wkv_ref[...], preferred_element_type=F32)
    k_rope = (k_r * ck_ref[...] + pltpu.roll(k_r, MLA_ROPE, 1) * sk_ref[...]).astype(BF16)
    cq = cq_ref[...]
    sq = sq_ref[...]
    for hd in range(MLA_HEADS):
        qh = q[:, hd * MLA_QK_PAD:(hd + 1) * MLA_QK_PAD]
        qh = qh * cq + pltpu.roll(qh, MLA_QK_PAD - MLA_ROPE, 1) * sq
        q_ref[0, :, hd * MLA_QK_PAD:(hd + 1) * MLA_QK_PAD] = qh.astype(BF16)
        k_ref[0, :, hd * MLA_QK_PAD:hd * MLA_QK_PAD + MLA_NOPE] = (
            kv[:, hd * MLA_NOPE:(hd + 1) * MLA_NOPE].astype(BF16))
        k_ref[0, :, hd * MLA_QK_PAD + MLA_NOPE:(hd + 1) * MLA_QK_PAD] = k_rope
    v_ref[0] = kv[:, MLA_HEADS * MLA_NOPE:].astype(BF16)


def _mla_up(pm, q_norm, kv_norm, wq, wkv, tabs):
    b, lp, _ = pm.shape
    cq, sq, ck, sk = tabs
    qk_w = MLA_HEADS * MLA_QK_PAD
    row = lambda bi, i: (bi, i, 0)
    tab = lambda bi, i: (i, 0)
    return pl.pallas_call(
        _mla_up_kernel,
        grid=(b, lp // SEQ_TILE),
        in_specs=[pl.BlockSpec((1, SEQ_TILE, MLA_COLS_PAD), row),
                  _const_spec((1, MLA_Q_LORA)), _const_spec((1, MLA_KV_LORA)),
                  _const_spec(wq.shape), _const_spec(wkv.shape),
                  pl.BlockSpec((SEQ_TILE, MLA_QK_PAD), tab), pl.BlockSpec((SEQ_TILE, MLA_QK_PAD), tab),
                  pl.BlockSpec((SEQ_TILE, LANE), tab), pl.BlockSpec((SEQ_TILE, LANE), tab)],
        out_specs=[pl.BlockSpec((1, SEQ_TILE, qk_w), row),
                   pl.BlockSpec((1, SEQ_TILE, qk_w), row),
                   pl.BlockSpec((1, SEQ_TILE, MLA_WIDTH), row)],
        out_shape=[jax.ShapeDtypeStruct((b, lp, qk_w), BF16),
                   jax.ShapeDtypeStruct((b, lp, qk_w), BF16),
                   jax.ShapeDtypeStruct((b, lp, MLA_WIDTH), BF16)],
        compiler_params=_cparams(("parallel", "parallel")),
        name="mla_up",
    )(pm, q_norm, kv_norm, wq, wkv, cq, sq, ck, sk)


def _rope_tables(lp):
    half = MLA_ROPE // 2
    inv = (np.float32(ROPE_THETA) ** (-np.arange(half, dtype=np.float32) / np.float32(half))).astype(np.float32)
    ang = (np.arange(lp, dtype=np.float32)[:, None] * inv[None, :]).astype(np.float64)
    cos = np.concatenate([np.cos(ang), np.cos(ang)], -1)
    sin = np.concatenate([np.sin(ang), np.sin(ang)], -1)
    scale = (MLA_NOPE + MLA_ROPE) ** -0.5 * np.log2(np.e)
    zeros = np.zeros((lp, MLA_ROPE))
    cq = np.concatenate([np.full((lp, MLA_NOPE), scale), cos * scale, zeros], -1)
    sq = np.concatenate([np.zeros((lp, MLA_NOPE)), sin * scale, zeros], -1)
    ck = np.concatenate([cos, zeros], -1)
    sk = np.concatenate([sin, zeros], -1)
    return tuple(jnp.asarray(t, F32) for t in (cq, sq, ck, sk))


def _chunk_id(pos):
    return jnp.where(pos < N_META, 0, 1 + ((pos - N_META) >> 6))


def _flash_kernel(q_ref, k_ref, v_ref, o_ref, m_sc, acc_sc, *, n_valid):
    i = pl.program_id(1)
    tq = q_ref.shape[1]
    lp = k_ref.shape[1]
    heads = range(MLA_HEADS)
    q = [q_ref[0, :, h * MLA_QK_PAD:(h + 1) * MLA_QK_PAD] for h in heads]
    m_sc[...] = jnp.full(m_sc.shape, -jnp.inf, F32)
    acc_sc[...] = jnp.zeros(acc_sc.shape, F32)

    def update(start, width, mask):
        ones = jnp.where(lax.broadcasted_iota(jnp.int32, (width, LANE), 1) == 0, 1.0, 0.0).astype(BF16)
        rows = pl.ds(start, width)
        s = [lax.dot_general(q[h], k_ref[0, rows, h * MLA_QK_PAD:(h + 1) * MLA_QK_PAD], _NT,
                             preferred_element_type=F32) for h in heads]
        if mask is not None:
            s = [jnp.where(mask, s[h], -jnp.inf) for h in heads]
        m_prev = [m_sc[h] for h in heads]
        m_new = [jnp.maximum(m_prev[h], jnp.max(s[h], -1, keepdims=True)) for h in heads]
        p = [jnp.exp2(s[h] - m_new[h]).astype(BF16) for h in heads]
        for h in heads:
            v_ext = jnp.concatenate([v_ref[0, rows, h * MLA_V:(h + 1) * MLA_V], ones], 1)
            acc_sc[h] = (jnp.exp2(m_prev[h] - m_new[h]) * acc_sc[h]
                         + jnp.dot(p[h], v_ext, preferred_element_type=F32))
            m_sc[h] = m_new[h]

    def full_block(j, carry):
        update(pl.multiple_of(j * tq, tq), tq, None)
        return carry

    lax.fori_loop(0, i, full_block, 0)
    start = jnp.minimum(i * tq, lp - FLASH_TAIL)
    start = pl.multiple_of(start, LANE)
    q_cid = _chunk_id(i * tq + lax.broadcasted_iota(jnp.int32, (tq, 1), 0))
    k_pos = start + lax.broadcasted_iota(jnp.int32, (1, FLASH_TAIL), 1)
    visible = (k_pos >= i * tq) & (k_pos < n_valid) & (_chunk_id(k_pos) <= q_cid)
    update(start, FLASH_TAIL, visible)
    for h in heads:
        acc = acc_sc[h]
        o_ref[0, :, h * MLA_V:(h + 1) * MLA_V] = (acc[:, :MLA_V] / acc[:, MLA_V:MLA_V + 1]).astype(o_ref.dtype)


def _flash(q, k, v, n_valid):
    b, lp, _ = q.shape
    assert SEQ_TILE + N_META <= FLASH_TAIL <= lp and FLASH_TAIL % LANE == 0 and SEQ_TILE % LANE == 0
    return pl.pallas_call(
        functools.partial(_flash_kernel, n_valid=n_valid),
        grid=(b, lp // SEQ_TILE),
        in_specs=[pl.BlockSpec((1, SEQ_TILE, MLA_HEADS * MLA_QK_PAD), lambda bi, i: (bi, i, 0)),
                  pl.BlockSpec((1, lp, MLA_HEADS * MLA_QK_PAD), lambda bi, i: (bi, 0, 0)),
                  pl.BlockSpec((1, lp, MLA_WIDTH), lambda bi, i: (bi, 0, 0))],
        out_specs=pl.BlockSpec((1, SEQ_TILE, MLA_WIDTH), lambda bi, i: (bi, i, 0)),
        out_shape=jax.ShapeDtypeStruct((b, lp, MLA_WIDTH), BF16),
        scratch_shapes=[pltpu.VMEM((MLA_HEADS, SEQ_TILE, 1), F32),
                        pltpu.VMEM((MLA_HEADS, SEQ_TILE, MLA_V + LANE), F32)],
        compiler_params=_cparams(("parallel", "arbitrary")),
        name="mla_flash",
    )(q, k, v)


def _rwkv_prep_kernel(*refs, has_vres):
    (p_ref, prev_ref, mu_ref, w0_ref, w2_ref, a0_ref, a2_ref, g2_ref, kk_ref, ka_ref, rk_ref,
     bd_ref) = refs[:12]
    n_in = 16 if has_vres else 12
    r_out, k_out, v_out, kk_out, bb_out, lw_out, bonus_out, g_out = refs[n_in:]
    c = RWKV_WIDTH
    p = p_ref[0]
    tm = p.shape[0]
    first = pl.program_id(1) == 0
    prev_last = jnp.where(first, 0.0, prev_ref[0, 7:8, :])
    row = lax.broadcasted_iota(jnp.int32, (tm, 1), 0)
    prev = jnp.where(row == 0, prev_last, pltpu.roll(p, 1, 0))
    ps = p + (prev - p) * mu_ref[...]
    r = ps[:, :c]
    k = ps[:, c:2 * c]
    v = ps[:, 2 * c:3 * c]
    xwa = ps[:, 3 * c:3 * c + W_LORA + A_LORA]
    xg = ps[:, 3 * c + W_LORA + A_LORA:]
    zw = w0_ref[...] + jnp.dot(jnp.tanh(xwa).astype(BF16), w2_ref[...], preferred_element_type=F32)
    nz = -zw
    softplus = jnp.maximum(nz, 0.0) + jnp.log(1.0 + jnp.exp(-jnp.abs(nz)))
    log_decay = -jnp.exp(-softplus - 0.5)
    if has_vres:
        vf_ref, v0_ref, v1_ref, v2_ref = refs[12:16]
        lo = jnp.dot(v.astype(BF16), v1_ref[...], preferred_element_type=F32)
        gate = _sigmoid(v0_ref[...] + jnp.dot(lo.astype(BF16), v2_ref[...], preferred_element_type=F32))
        v = v + (vf_ref[0] - v) * gate
    a = _sigmoid(a0_ref[...] + jnp.dot(xwa.astype(BF16), a2_ref[...], preferred_element_type=F32))
    g = jnp.dot(_sigmoid(xg).astype(BF16), g2_ref[...], preferred_element_type=F32)
    bd = bd_ref[...]
    kk = k * kk_ref[...]
    kk = kk * lax.rsqrt(jnp.maximum(_split_dot(kk * kk, bd), 1e-24))
    k = k * (1.0 + (a - 1.0) * ka_ref[...])
    bonus = _split_dot(r * k * rk_ref[...], bd) * v
    r_out[0] = r
    k_out[0] = k
    v_out[0] = v
    kk_out[0] = kk
    bb_out[0] = kk * a
    lw_out[0] = log_decay
    bonus_out[0] = bonus
    g_out[0] = g


def _rwkv_prep(pr, prm, v_first):
    b, lp, cols = pr.shape
    has_vres = v_first is not None
    c = RWKV_WIDTH
    row = lambda bi, i: (bi, i, 0)
    prev = lambda bi, i: (bi, jnp.maximum(i * (SEQ_TILE // 8) - 1, 0), 0)
    names = ["mu", "w0", "w2", "a0", "a2", "g2", "k_k", "k_a", "r_k", "bd"]
    args = [pr, pr] + [prm[n] for n in names]
    in_specs = [pl.BlockSpec((1, SEQ_TILE, cols), row), pl.BlockSpec((1, 8, cols), prev)]
    in_specs += [_const_spec(prm[n].shape) for n in names]
    if has_vres:
        args += [v_first, prm["v0"], prm["v1"], prm["v2"]]
        in_specs += [pl.BlockSpec((1, SEQ_TILE, c), row)]
        in_specs += [_const_spec(prm[n].shape) for n in ("v0", "v1", "v2")]
    out_spec = pl.BlockSpec((1, SEQ_TILE, c), row)
    return pl.pallas_call(
        functools.partial(_rwkv_prep_kernel, has_vres=has_vres),
        grid=(b, lp // SEQ_TILE),
        in_specs=in_specs,
        out_specs=[out_spec] * 8,
        out_shape=[jax.ShapeDtypeStruct((b, lp, c), F32)] * 8,
        compiler_params=_cparams(("parallel", "parallel")),
        name="rwkv_prep",
    )(*args)


_NN = (((1,), (0,)), ((), ()))
_NT = (((1,), (1,)), ((), ()))


def _mm(a, b, dims):
    return lax.dot_general(a.astype(BF16), b.astype(BF16), dims, preferred_element_type=F32)


def _rwkv_scan_kernel(r_ref, k_ref, v_ref, kk_ref, bb_ref, lw_ref, y_ref, s_sc):
    @pl.when(pl.program_id(1) == 0)
    def _():
        s_sc[...] = jnp.zeros(s_sc.shape, F32)

    c = SCAN_CHUNK
    n = 2 * c
    n_chunks = r_ref.shape[1] // c
    ri = lax.broadcasted_iota(jnp.int32, (c, c), 0)
    ci = lax.broadcasted_iota(jnp.int32, (c, c), 1)
    tri = jnp.where(ri >= ci, 1.0, 0.0).astype(BF16)
    row = lax.broadcasted_iota(jnp.int32, (2 * n, 2 * n), 0)
    col = lax.broadcasted_iota(jnp.int32, (2 * n, 2 * n), 1)
    t_idx = row & (c - 1)
    s_idx = col & (c - 1)
    causal = (t_idx > s_idx) | ((t_idx == s_idx) & (row >= n))
    eye = jnp.where(lax.broadcasted_iota(jnp.int32, (n, n), 0) == lax.broadcasted_iota(jnp.int32, (n, n), 1),
                    1.0, 0.0).astype(F32)
    head0 = lax.broadcasted_iota(jnp.int32, (1, LANE), 1) < RWKV_HEAD

    def stack(x):
        return jnp.concatenate([jnp.where(head0, x, 0.0), jnp.where(head0, 0.0, x)], axis=0)

    ar, bk, v_t, w_end = [], [], [], []
    for ch in range(n_chunks):
        rows = slice(ch * c, (ch + 1) * c)
        lw = lw_ref[0, rows, :]
        lw_hi = lw.astype(BF16)
        lw_lo = (lw - lw_hi.astype(F32)).astype(BF16)
        cum = (jnp.dot(tri, lw_hi, preferred_element_type=F32)
               + jnp.dot(tri, lw_lo, preferred_element_type=F32))
        w_incl = jnp.exp(cum)
        w_inv = jnp.exp(-cum)
        a_hat = -kk_ref[0, rows, :] * jnp.exp(cum - lw)
        r_hat = r_ref[0, rows, :] * w_incl
        b_hat = bb_ref[0, rows, :] * w_inv
        k_hat = k_ref[0, rows, :] * w_inv
        v = v_ref[0, rows, :]
        for pr in range(RWKV_PAIRS):
            sl = slice(pr * LANE, (pr + 1) * LANE)
            ar.append(jnp.concatenate([stack(a_hat[:, sl]), stack(r_hat[:, sl])], 0).astype(BF16))
            bk.append(jnp.concatenate([stack(b_hat[:, sl]), stack(k_hat[:, sl])], 0).astype(BF16))
            v_t.append(stack(v[:, sl]).T.astype(BF16))
            w_end.append(w_incl[c - 1:c, sl])
    every = range(len(ar))
    gram = [jnp.where(causal, _mm(ar[g], bk[g], _NT), 0.0) for g in every]
    pw = [gram[g][:n, :n] for g in every]
    t_inv = [eye + pw[g] for g in every]
    pw = [_mm(pw[g], pw[g], _NN) for g in every]
    for _ in range(4):
        both = [_mm(pw[g], jnp.concatenate([pw[g], t_inv[g]], 1), _NN) for g in every]
        pw = [both[g][:, :n] for g in every]
        t_inv = [t_inv[g] + both[g][:, n:] for g in every]
    t_inv = [(t_inv[g] + _mm(pw[g], t_inv[g], _NN)).astype(BF16) for g in every]
    va = [_mm(v_t[g], gram[g][:n, n:], _NT) for g in every]
    q_bk = [gram[g][n:, :].astype(BF16) for g in every]

    state = [s_sc[pr] for pr in range(RWKV_PAIRS)]
    for ch in range(n_chunks):
        gs = [ch * RWKV_PAIRS + pr for pr in range(RWKV_PAIRS)]
        s_ar = [_mm(state[pr], ar[g], _NT) for pr, g in enumerate(gs)]
        u_t = [_mm(s_ar[pr][:, :n] + va[g], t_inv[g], _NT) for pr, g in enumerate(gs)]
        uv = [jnp.concatenate([u_t[pr].astype(BF16), v_t[g]], 1) for pr, g in enumerate(gs)]
        state = [(state[pr] + _mm(uv[pr], bk[g], _NN)) * w_end[g] for pr, g in enumerate(gs)]
        for pr, g in enumerate(gs):
            y = (s_ar[pr][:, n:] + _mm(uv[pr], q_bk[g], _NT)).T
            y_ref[0, ch * c:(ch + 1) * c, pr * LANE:(pr + 1) * LANE] = y[:c] + y[c:]
    for pr in range(RWKV_PAIRS):
        s_sc[pr] = state[pr]


def _rwkv_scan(r, k, v, kk, bb, lw):
    b, lp, c = r.shape
    rows = SCAN_STEP_CHUNKS * SCAN_CHUNK
    spec = pl.BlockSpec((1, rows, c), lambda bi, i: (bi, i, 0))
    return pl.pallas_call(
        _rwkv_scan_kernel,
        grid=(b, lp // rows),
        in_specs=[spec] * 6,
        out_specs=spec,
        out_shape=jax.ShapeDtypeStruct((b, lp, c), F32),
        scratch_shapes=[pltpu.VMEM((RWKV_PAIRS, LANE, LANE), F32)],
        compiler_params=_cparams(("parallel", "arbitrary")),
        name="rwkv_scan",
    )(r, k, v, kk, bb, lw)


def _outproj_kernel(ys_ref, bonus_ref, g_ref, ymla_ref, lnw_ref, lnb_ref, bd_ref, wo_ref, h_ref, o_ref):
    y = ys_ref[...]
    bd = bd_ref[...]
    inv_n = 1.0 / RWKV_HEAD
    d = y - _split_dot(y, bd) * inv_n
    var = _split_dot(d * d, bd) * inv_n
    yn = d * lax.rsqrt(var + GN_EPS) * lnw_ref[...] + lnb_ref[...]
    yr = ((yn + bonus_ref[...]) * g_ref[...]).astype(BF16)
    o_ref[...] = (h_ref[...]
                  + jnp.dot(ymla_ref[...], wo_ref[:MLA_WIDTH, :], preferred_element_type=F32)
                  + jnp.dot(yr, wo_ref[MLA_WIDTH:, :], preferred_element_type=F32))


def _outproj(ys, bonus, g, ymla, ln_w, ln_b, bd, wo, h):
    tp = h.shape[0]
    c = RWKV_WIDTH
    rc = pl.BlockSpec((ROW_TILE, c), lambda i: (i, 0))
    rd = pl.BlockSpec((ROW_TILE, D_MODEL), lambda i: (i, 0))
    return pl.pallas_call(
        _outproj_kernel,
        grid=(tp // ROW_TILE,),
        in_specs=[rc, rc, rc, rc, _const_spec((1, c)), _const_spec((1, c)), _const_spec(bd.shape),
                  _const_spec(wo.shape), rd],
        out_specs=rd,
        out_shape=jax.ShapeDtypeStruct((tp, D_MODEL), F32),
        compiler_params=_cparams(("parallel",)),
        name="outproj",
    )(ys, bonus, g, ymla, ln_w, ln_b, bd, wo, h)


def _ffn_chunks(u, wg_ref, wu_ref, wd_ref, idx, ff, chunk):
    acc = None
    for c0 in range(0, ff, chunk):
        sl = slice(c0, c0 + chunk)
        gate = jnp.dot(u, wg_ref[idx + (slice(None), sl)], preferred_element_type=F32)
        up = jnp.dot(u, wu_ref[idx + (slice(None), sl)], preferred_element_type=F32)
        act = (gate * _sigmoid(gate) * up).astype(BF16)
        part = jnp.dot(act, wd_ref[idx + (sl, slice(None))], preferred_element_type=F32)
        acc = part if acc is None else acc + part
    return acc


def _dense_ffn_kernel(h_ref, g_ref, wg_ref, wu_ref, wd_ref, o_ref, *, ff, chunk):
    h = h_ref[...]
    u = _rms(h, g_ref[...]).astype(BF16)
    o_ref[...] = h + _ffn_chunks(u, wg_ref, wu_ref, wd_ref, (), ff, chunk)


def _dense_ffn(h, g, wg, wu, wd):
    tp = h.shape[0]
    ff = wg.shape[1]
    rd = pl.BlockSpec((ROW_TILE, D_MODEL), lambda i: (i, 0))
    once = lambda shape: pl.BlockSpec(shape, lambda i: (0, 0), pipeline_mode=pl.Buffered(1))
    return pl.pallas_call(
        functools.partial(_dense_ffn_kernel, ff=ff, chunk=256),
        grid=(tp // ROW_TILE,),
        in_specs=[rd, _const_spec((1, D_MODEL)), once(wg.shape), once(wu.shape), once(wd.shape)],
        out_specs=rd,
        out_shape=jax.ShapeDtypeStruct((tp, D_MODEL), F32),
        compiler_params=_cparams(("parallel",)),
        name="dense_ffn",
    )(h, g, wg, wu, wd)


def _router_kernel(h_ref, g_ref, wr_ref, br_ref, u_ref, meta_ref, cnt_ref, carry_sc, *, n_valid):
    @pl.when((pl.program_id(0) == 0) & (pl.program_id(1) == 0))
    def _():
        carry_sc[...] = jnp.zeros(carry_sc.shape, F32)

    tm = h_ref.shape[1]
    u = _rms(h_ref[0], g_ref[...])
    u_ref[0] = u
    logits = jnp.dot(u, wr_ref[...], precision=HIGHEST, preferred_element_type=F32) + br_ref[...]
    lane = lax.broadcasted_iota(jnp.int32, (tm, LANE), 1).astype(F32)
    logits = jnp.where(lane < N_EXPERTS, logits, -jnp.inf)
    top0 = jnp.max(logits, -1, keepdims=True)
    e0 = jnp.min(jnp.where(logits == top0, lane, float(LANE)), -1, keepdims=True)
    rest = jnp.where(lane == e0, -jnp.inf, logits)
    top1 = jnp.max(rest, -1, keepdims=True)
    e1 = jnp.min(jnp.where(rest == top1, lane, float(LANE)), -1, keepdims=True)
    ex = jnp.exp(top1 - top0)
    g0 = 1.0 / (1.0 + ex)
    g1 = ex / (1.0 + ex)
    pos = pl.program_id(1) * tm + lax.broadcasted_iota(jnp.int32, (tm, 1), 0)
    valid = pos < n_valid
    oh0 = jnp.where((lane == e0) & valid, 1.0, 0.0)
    oh1 = jnp.where((lane == e1) & valid, 1.0, 0.0)
    ri = lax.broadcasted_iota(jnp.int32, (tm, tm), 0)
    ci = lax.broadcasted_iota(jnp.int32, (tm, tm), 1)
    before = jnp.where(ri > ci, 1.0, 0.0).astype(BF16)
    both = oh0 + oh1
    seen = carry_sc[...] + jnp.dot(before, both.astype(BF16), preferred_element_type=F32)
    rank0 = jnp.sum(jnp.where(lane == e0, seen, 0.0), -1, keepdims=True)
    rank1 = jnp.sum(jnp.where(lane == e1, seen, 0.0), -1, keepdims=True)
    carry_sc[...] = carry_sc[...] + jnp.sum(both, 0, keepdims=True)
    cnt_ref[...] = jnp.broadcast_to(carry_sc[...], cnt_ref.shape)
    vf = jnp.where(valid, 1.0, 0.0)
    meta = jnp.where(lane == 0, e0, 0.0)
    meta = jnp.where(lane == 1, e1, meta)
    meta = jnp.where(lane == 2, rank0, meta)
    meta = jnp.where(lane == 3, rank1, meta)
    meta = jnp.where(lane == 4, g0 * vf, meta)
    meta = jnp.where(lane == 5, g1 * vf, meta)
    meta = jnp.where(lane == 6, vf, meta)
    meta_ref[0] = meta


def _router(h, g, wr, br, n_valid):
    b, lp, _ = h.shape
    row = lambda bi, i: (bi, i, 0)
    return pl.pallas_call(
        functools.partial(_router_kernel, n_valid=n_valid),
        grid=(b, lp // SEQ_TILE),
        in_specs=[pl.BlockSpec((1, SEQ_TILE, D_MODEL), row), _const_spec((1, D_MODEL)),
                  _const_spec(wr.shape), _const_spec(br.shape)],
        out_specs=[pl.BlockSpec((1, SEQ_TILE, D_MODEL), row), pl.BlockSpec((1, SEQ_TILE, LANE), row),
                   _const_spec((8, LANE))],
        out_shape=[jax.ShapeDtypeStruct((b, lp, D_MODEL), F32), jax.ShapeDtypeStruct((b, lp, LANE), F32),
                   jax.ShapeDtypeStruct((8, LANE), F32)],
        scratch_shapes=[pltpu.VMEM((1, LANE), F32)],
        compiler_params=_cparams(("arbitrary", "arbitrary")),
        name="moe_router",
    )(h, g, wr, br)


def _dispatch_kernel(s0_ref, s1_ref, u_ref, xb_in_ref, xb_ref, sem):
    del xb_in_ref
    tm = u_ref.shape[0]

    def copy(rw, slot):
        return pltpu.make_async_copy(u_ref.at[pl.ds(rw, 1), :], xb_ref.at[pl.ds(slot, 1), :], sem)

    def start(rw, carry):
        for s_ref in (s0_ref, s1_ref):
            slot = s_ref[0, 0, rw]

            @pl.when(slot >= 0)
            def _():
                copy(rw, slot).start()
        return carry

    def wait(rw, carry):
        for s_ref in (s0_ref, s1_ref):
            slot = s_ref[0, 0, rw]

            @pl.when(slot >= 0)
            def _():
                copy(rw, slot).wait()
        return carry

    lax.fori_loop(0, tm, start, 0)
    lax.fori_loop(0, tm, wait, 0)


def _dispatch(slot0, slot1, u, n_slots, tm):
    tp = u.shape[0]
    sspec = pl.BlockSpec((1, 1, tm), lambda i: (i, 0, 0), memory_space=pltpu.SMEM)
    xb0 = jnp.zeros((n_slots, D_MODEL), F32)
    return pl.pallas_call(
        _dispatch_kernel,
        grid=(tp // tm,),
        in_specs=[sspec, sspec, pl.BlockSpec((tm, D_MODEL), lambda i: (i, 0)),
                  pl.BlockSpec(memory_space=pl.ANY)],
        out_specs=pl.BlockSpec(memory_space=pl.ANY),
        out_shape=jax.ShapeDtypeStruct((n_slots, D_MODEL), F32),
        scratch_shapes=[pltpu.SemaphoreType.DMA(())],
        input_output_aliases={3: 0},
        compiler_params=_cparams(("arbitrary",)),
        name="moe_dispatch",
    )(slot0.reshape(tp // tm, 1, tm), slot1.reshape(tp // tm, 1, tm), u, xb0)


def _expert_kernel(be_ref, x_ref, wg_ref, wu_ref, wd_ref, o_ref, *, ff, chunk):
    del be_ref
    u = x_ref[...].astype(BF16)
    o_ref[...] = _ffn_chunks(u, wg_ref, wu_ref, wd_ref, (0,), ff, chunk)


def _experts(block_e, xb, wg, wu, wd):
    n_slots = xb.shape[0]
    ff = wg.shape[2]
    rows = pl.BlockSpec((MOE_BLOCK, D_MODEL), lambda i, be: (i, 0))
    return pl.pallas_call(
        functools.partial(_expert_kernel, ff=ff, chunk=512),
        grid_spec=pltpu.PrefetchScalarGridSpec(
            num_scalar_prefetch=1,
            grid=(n_slots // MOE_BLOCK,),
            in_specs=[rows,
                      pl.BlockSpec((1, D_MODEL, ff), lambda i, be: (be[i], 0, 0)),
                      pl.BlockSpec((1, D_MODEL, ff), lambda i, be: (be[i], 0, 0)),
                      pl.BlockSpec((1, ff, D_MODEL), lambda i, be: (be[i], 0, 0))],
            out_specs=rows),
        out_shape=jax.ShapeDtypeStruct((n_slots, D_MODEL), F32),
        compiler_params=_cparams(("arbitrary",), 60 * 1024 * 1024),
        name="moe_experts",
    )(block_e, xb, wg, wu, wd)


def _combine_kernel(s0_ref, s1_ref, h_ref, gates_ref, fn_ref, yb_ref, o_ref, y0_sc, y1_sc, sem):
    tm = h_ref.shape[0]

    def copy(rw, s_ref, buf):
        return pltpu.make_async_copy(yb_ref.at[pl.ds(s_ref[0, 0, rw], 1), :], buf.at[pl.ds(rw, 1), :], sem)

    def start(rw, carry):
        copy(rw, s0_ref, y0_sc).start()
        copy(rw, s1_ref, y1_sc).start()
        return carry

    def wait(rw, carry):
        copy(rw, s0_ref, y0_sc).wait()
        copy(rw, s1_ref, y1_sc).wait()
        return carry

    lax.fori_loop(0, tm, start, 0)
    lax.fori_loop(0, tm, wait, 0)
    gates = gates_ref[...]
    f = y0_sc[...] * gates[:, 4:5] + y1_sc[...] * gates[:, 5:6]
    o_ref[...] = _rms(h_ref[...] + f, fn_ref[...])


def _combine(slot0, slot1, h, meta, fn, yb, tm):
    tp = h.shape[0]
    sspec = pl.BlockSpec((1, 1, tm), lambda i: (i, 0, 0), memory_space=pltpu.SMEM)
    rd = pl.BlockSpec((tm, D_MODEL), lambda i: (i, 0))
    return pl.pallas_call(
        _combine_kernel,
        grid=(tp // tm,),
        in_specs=[sspec, sspec, rd, pl.BlockSpec((tm, LANE), lambda i: (i, 0)), _const_spec((1, D_MODEL)),
                  pl.BlockSpec(memory_space=pl.ANY)],
        out_specs=rd,
        out_shape=jax.ShapeDtypeStruct((tp, D_MODEL), F32),
        scratch_shapes=[pltpu.VMEM((tm, D_MODEL), F32), pltpu.VMEM((tm, D_MODEL), F32),
                        pltpu.SemaphoreType.DMA(())],
        compiler_params=_cparams(("arbitrary",)),
        name="moe_combine",
    )(slot0.reshape(tp // tm, 1, tm), slot1.reshape(tp // tm, 1, tm), h, meta, fn, yb)


def _final_norm_kernel(h_ref, g_ref, o_ref):
    o_ref[...] = _rms(h_ref[...], g_ref[...])


def _final_norm(h, g):
    tp = h.shape[0]
    rd = pl.BlockSpec((ROW_TILE, D_MODEL), lambda i: (i, 0))
    return pl.pallas_call(
        _final_norm_kernel,
        grid=(tp // ROW_TILE,),
        in_specs=[rd, _const_spec((1, D_MODEL))],
        out_specs=rd,
        out_shape=jax.ShapeDtypeStruct((tp, D_MODEL), F32),
        compiler_params=_cparams(("parallel",)),
        name="final_norm",
    )(h, g)


def _rot_cols(w):
    half = MLA_ROPE // 2
    return jnp.concatenate([-w[..., half:], w[..., :half]], -1)


def _prep_inproj(w_in):
    m_q = w_in[:, :MLA_Q_LORA + MLA_KV_LORA]
    k_rope = w_in[:, MLA_Q_LORA + MLA_KV_LORA:MLA_Q_LORA + MLA_KV_LORA + MLA_ROPE]
    w_m = jnp.concatenate([m_q, k_rope, _rot_cols(k_rope)], -1)
    w_r = w_in[:, MLA_Q_LORA + MLA_KV_LORA + MLA_ROPE:]
    return w_r.astype(BF16), w_m.astype(BF16)


def _prep_wq(w_uq):
    w = w_uq.reshape(MLA_Q_LORA, MLA_HEADS, MLA_NOPE + MLA_ROPE)
    rope = w[..., MLA_NOPE:]
    return jnp.concatenate([w, _rot_cols(rope)], -1).reshape(MLA_Q_LORA, MLA_HEADS * MLA_QK_PAD).astype(BF16)


def _prep_wkv(w_ukv):
    w = w_ukv.reshape(MLA_KV_LORA, MLA_HEADS, MLA_NOPE + MLA_V)
    k_nope = w[..., :MLA_NOPE].reshape(MLA_KV_LORA, MLA_HEADS * MLA_NOPE)
    v = w[..., MLA_NOPE:].reshape(MLA_KV_LORA, MLA_HEADS * MLA_V)
    return jnp.concatenate([k_nope, v], -1).astype(BF16)


def _row(x):
    return x.reshape(1, -1).astype(F32)


def _block_diag_ones():
    idx = np.arange(RWKV_WIDTH) // RWKV_HEAD
    return jnp.asarray(idx[:, None] == idx[None, :], BF16)


def kernel(x, meta_tokens, attn_norm, ffn_norm, final_norm, w_in, w_out, mla_q_norm, mla_kv_norm, mla_w_uq, mla_w_ukv, rwkv_mu, rwkv_w0, rwkv_w2, rwkv_a0, rwkv_a2, rwkv_g2, rwkv_k_k, rwkv_k_a, rwkv_r_k, rwkv_ln_w, rwkv_ln_b, rwkv_v0, rwkv_v1, rwkv_v2, ffn_w_gate, ffn_w_up, ffn_w_down, moe_router, moe_router_bias, moe_w_gate, moe_w_up, moe_w_down):
    b, seq, d = x.shape
    depth = attn_norm.shape[0]
    n_valid = N_META + seq
    lp = -(-n_valid // SEQ_TILE) * SEQ_TILE
    tp = b * lp
    assert d == D_MODEL and tp % ROW_TILE == 0 and lp % SCAN_CHUNK == 0

    meta = jnp.broadcast_to(meta_tokens[None].astype(x.dtype), (b, N_META, d))
    h = jnp.concatenate([meta, x, jnp.zeros((b, lp - n_valid, d), x.dtype)], axis=1).reshape(tp, d)

    tabs = _rope_tables(lp)
    bd = _block_diag_ones()
    zeros_wa = jnp.zeros((W_LORA, RWKV_WIDTH), F32)
    v_first = None
    for i in range(depth):
        w_r, w_m = _prep_inproj(w_in[i])
        pr, pm = _norm_inproj(h, _row(attn_norm[i]), w_r, w_m)

        q, k, v = _mla_up(pm.reshape(b, lp, -1), _row(mla_q_norm[i]), _row(mla_kv_norm[i]),
                          _prep_wq(mla_w_uq[i]), _prep_wkv(mla_w_ukv[i]), tabs)
        y_mla = _flash(q, k, v, n_valid)

        prm = {
            "mu": _row(rwkv_mu[i]), "w0": _row(rwkv_w0[i]), "a0": _row(rwkv_a0[i]),
            "w2": jnp.concatenate([rwkv_w2[i], zeros_wa], 0).astype(BF16),
            "a2": jnp.concatenate([zeros_wa, rwkv_a2[i]], 0).astype(BF16),
            "g2": rwkv_g2[i].astype(BF16),
            "k_k": _row(rwkv_k_k[i]), "k_a": _row(rwkv_k_a[i]), "r_k": _row(rwkv_r_k[i]), "bd": bd,
        }
        if i > 0:
            prm["v0"] = _row(rwkv_v0[i - 1])
            prm["v1"] = jnp.pad(rwkv_v1[i - 1], ((0, 0), (0, LANE - V_LORA))).astype(BF16)
            prm["v2"] = jnp.pad(rwkv_v2[i - 1], ((0, LANE - V_LORA), (0, 0))).astype(BF16)
        r_, k_, v_, kk_, bb_, lw_, bonus, gate = _rwkv_prep(pr.reshape(b, lp, -1), prm, v_first)
        if i == 0:
            v_first = v_
        ys = _rwkv_scan(r_, k_, v_, kk_, bb_, lw_)

        flat = lambda t: t.reshape(tp, -1)
        h = _outproj(flat(ys), flat(bonus), flat(gate), flat(y_mla), _row(rwkv_ln_w[i]), _row(rwkv_ln_b[i]),
                     bd, w_out[i].astype(BF16), h)

        j = i // 2
        last = i == depth - 1
        if i % 2 == 0:
            h = _dense_ffn(h, _row(ffn_norm[i]), ffn_w_gate[j].astype(BF16), ffn_w_up[j].astype(BF16),
                           ffn_w_down[j].astype(BF16))
            if last:
                h = _final_norm(h, _row(final_norm))
        else:
            h = _moe(h, b, lp, n_valid, _row(ffn_norm[i]), moe_router[j], moe_router_bias[j],
                     moe_w_gate[j], moe_w_up[j], moe_w_down[j], _row(final_norm) if last else None)
            if not last:
                raise NotImplementedError("an MoE layer that is not the last layer")
    return h.reshape(b, lp, d)[:, N_META:n_valid]


def _moe(h, b, lp, n_valid, g, router, router_bias, w_gate, w_up, w_down, final_g):
    tp = b * lp
    wr = jnp.pad(router, ((0, 0), (0, LANE - N_EXPERTS))).astype(F32)
    br = jnp.pad(router_bias, (0, LANE - N_EXPERTS)).reshape(1, LANE).astype(F32)
    u, meta, cnt = _router(h.reshape(b, lp, D_MODEL), g, wr, br, n_valid)
    meta = meta.reshape(tp, LANE)

    counts = cnt[0, :N_EXPERTS].astype(jnp.int32)
    padded = (counts + MOE_BLOCK - 1) // MOE_BLOCK * MOE_BLOCK
    pend = jnp.cumsum(padded)
    pstart = pend - padded
    n_assign = b * n_valid * TOP_K
    n_blocks = (n_assign + N_EXPERTS * (MOE_BLOCK - 1)) // MOE_BLOCK + 1
    n_slots = n_blocks * MOE_BLOCK
    block_start = jnp.arange(n_blocks, dtype=jnp.int32) * MOE_BLOCK
    block_e = jnp.minimum(jnp.sum((pend[None, :] <= block_start[:, None]).astype(jnp.int32), -1),
                          N_EXPERTS - 1)
    valid = meta[:, 6] > 0.5
    e0 = meta[:, 0].astype(jnp.int32)
    e1 = meta[:, 1].astype(jnp.int32)
    slot0 = pstart[e0] + meta[:, 2].astype(jnp.int32)
    slot1 = pstart[e1] + meta[:, 3].astype(jnp.int32)
    d0 = jnp.where(valid, slot0, -1)
    d1 = jnp.where(valid, slot1, -1)
    c0 = jnp.where(valid, slot0, 0)
    c1 = jnp.where(valid, slot1, 0)

    xb = _dispatch(d0, d1, u.reshape(tp, D_MODEL), n_slots, LANE)
    yb = _experts(block_e, xb, w_gate.astype(BF16), w_up.astype(BF16), w_down.astype(BF16))
    return _combine(c0, c1, h, meta, final_g, yb, LANE)
```

```python
import functools
import math

import numpy as np
import jax
import jax.numpy as jnp
from jax import lax
from jax.experimental import pallas as pl
from jax.experimental.pallas import tpu as pltpu

F32 = jnp.float32
BF16 = jnp.bfloat16

D_MODEL = 1024
CHUNK = 64
N_META = 16
RMS_EPS = 1e-6

MLA_HEADS = 4
MLA_NOPE = 128
MLA_ROPE = 64
MLA_V = 128
MLA_Q_LORA = 512
MLA_KV_LORA = 256
ROPE_THETA = 10000.0
MLA_WIDTH = MLA_HEADS * MLA_V
MLA_QK_PAD = 256
MLA_COLS_PAD = MLA_Q_LORA + MLA_KV_LORA + 2 * MLA_ROPE

RWKV_WIDTH = 512
RWKV_HEAD = 64
RWKV_HEADS = RWKV_WIDTH // RWKV_HEAD
RWKV_PAIRS = RWKV_HEADS // 2
W_LORA = 64
A_LORA = 64
V_LORA = 32
G_LORA = 128
GN_EPS = 64e-5
RWKV_COLS = 3 * RWKV_WIDTH + W_LORA + A_LORA + G_LORA
SCAN_CHUNK = 64
SCAN_STEP_CHUNKS = 6

N_EXPERTS = 8
TOP_K = 2
MOE_BLOCK = 256

LANE = 128
SEQ_TILE = 384
FLASH_KEYS = 256
ROW_TILE = 512
PREV_ROWS = 16
DMA_UNROLL = 8
VMEM_LIMIT = 56 * 1024 * 1024

HIGHEST = lax.Precision.HIGHEST


def _cparams(sem, vmem=VMEM_LIMIT):
    return pltpu.CompilerParams(dimension_semantics=sem, vmem_limit_bytes=vmem)


def _rms(x, g):
    return x * lax.rsqrt(jnp.mean(x * x, -1, keepdims=True) + RMS_EPS) * g


def _sigmoid(x):
    return 1.0 / (1.0 + jnp.exp(-x))


def _split_dot(x, w):
    hi = x.astype(BF16)
    lo = (x - hi.astype(F32)).astype(BF16)
    return jnp.dot(hi, w, preferred_element_type=F32) + jnp.dot(lo, w, preferred_element_type=F32)


def _const_spec(shape):
    nd = len(shape)
    return pl.BlockSpec(shape, lambda *_: (0,) * nd)


def _norm_inproj_kernel(h_ref, g_ref, wr_ref, wm_ref, pr_ref, pm_ref):
    u = _rms(h_ref[...], g_ref[...]).astype(BF16)
    pr_ref[...] = jnp.dot(u, wr_ref[...], preferred_element_type=F32).astype(pr_ref.dtype)
    pm_ref[...] = jnp.dot(u, wm_ref[...], preferred_element_type=F32).astype(pm_ref.dtype)


def _norm_inproj(h, g, w_r, w_m):
    tp = h.shape[0]
    return pl.pallas_call(
        _norm_inproj_kernel,
        grid=(tp // ROW_TILE,),
        in_specs=[pl.BlockSpec((ROW_TILE, D_MODEL), lambda i: (i, 0)),
                  _const_spec((1, D_MODEL)),
                  _const_spec(w_r.shape), _const_spec(w_m.shape)],
        out_specs=[pl.BlockSpec((ROW_TILE, w_r.shape[1]), lambda i: (i, 0)),
                   pl.BlockSpec((ROW_TILE, w_m.shape[1]), lambda i: (i, 0))],
        out_shape=[jax.ShapeDtypeStruct((tp, w_r.shape[1]), BF16),
                   jax.ShapeDtypeStruct((tp, w_m.shape[1]), BF16)],
        compiler_params=_cparams(("parallel",)),
        name="norm_inproj",
    )(h, g, w_r, w_m)


def _mla_up_kernel(pm_ref, qn_ref, kvn_ref, wq_ref, wkv_ref, cq_ref, sq_ref, ck_ref, sk_ref,
                   q_ref, k_ref, vt_ref):
    pm = pm_ref[0].astype(F32)
    c_q = pm[:, :MLA_Q_LORA]
    c_kv = pm[:, MLA_Q_LORA:MLA_Q_LORA + MLA_KV_LORA]
    k_r = pm[:, MLA_Q_LORA + MLA_KV_LORA:]
    q = jnp.dot(_rms(c_q, qn_ref[...]).astype(BF16), wq_ref[...], preferred_element_type=F32)
    kv = jnp.dot(_rms(c_kv, kvn_ref[...]).astype(BF16), wkv_ref[...], preferred_element_type=F32)
    k_rope = (k_r * ck_ref[...] + pltpu.roll(k_r, MLA_ROPE, 1) * sk_ref[...]).astype(BF16)
    cq = cq_ref[...]
    sq = sq_ref[...]
    for hd in range(MLA_HEADS):
        qh = q[:, hd * MLA_QK_PAD:(hd + 1) * MLA_QK_PAD]
        qh = qh * cq + pltpu.roll(qh, MLA_QK_PAD - MLA_ROPE, 1) * sq
        q_ref[0, :, hd * MLA_QK_PAD:(hd + 1) * MLA_QK_PAD] = qh.astype(BF16)
        k_ref[0, :, hd * MLA_QK_PAD:hd * MLA_QK_PAD + MLA_NOPE] = (
            kv[:, hd * MLA_NOPE:(hd + 1) * MLA_NOPE].astype(BF16))
        k_ref[0, :, hd * MLA_QK_PAD + MLA_NOPE:(hd + 1) * MLA_QK_PAD] = k_rope
    vt_ref[0] = kv[:, MLA_HEADS * MLA_NOPE:].T.astype(BF16)


def _mla_up(pm, q_norm, kv_norm, wq, wkv, tabs):
    b, lp, _ = pm.shape
    cq, sq, ck, sk = tabs
    qk_w = MLA_HEADS * MLA_QK_PAD
    row = lambda bi, i: (bi, i, 0)
    tab = lambda bi, i: (i, 0)
    return pl.pallas_call(
        _mla_up_kernel,
        grid=(b, lp // SEQ_TILE),
        in_specs=[pl.BlockSpec((1, SEQ_TILE, MLA_COLS_PAD), row),
                  _const_spec((1, MLA_Q_LORA)), _const_spec((1, MLA_KV_LORA)),
                  _const_spec(wq.shape), _const_spec(wkv.shape),
                  pl.BlockSpec((SEQ_TILE, MLA_QK_PAD), tab), pl.BlockSpec((SEQ_TILE, MLA_QK_PAD), tab),
                  pl.BlockSpec((SEQ_TILE, LANE), tab), pl.BlockSpec((SEQ_TILE, LANE), tab)],
        out_specs=[pl.BlockSpec((1, SEQ_TILE, qk_w), row),
                   pl.BlockSpec((1, SEQ_TILE, qk_w), row),
                   pl.BlockSpec((1, MLA_WIDTH, SEQ_TILE), lambda bi, i: (bi, 0, i))],
        out_shape=[jax.ShapeDtypeStruct((b, lp, qk_w), BF16),
                   jax.ShapeDtypeStruct((b, lp, qk_w), BF16),
                   jax.ShapeDtypeStruct((b, MLA_WIDTH, lp), BF16)],
        compiler_params=_cparams(("parallel", "parallel")),
        name="mla_up",
    )(pm, q_norm, kv_norm, wq, wkv, cq, sq, ck, sk)


def _rope_tables(lp):
    half = MLA_ROPE // 2
    inv = (np.float32(ROPE_THETA) ** (-np.arange(half, dtype=np.float32) / np.float32(half))).astype(np.float32)
    ang = (np.arange(lp, dtype=np.float32)[:, None] * inv[None, :]).astype(np.float64)
    cos = np.concatenate([np.cos(ang), np.cos(ang)], -1)
    sin = np.concatenate([np.sin(ang), np.sin(ang)], -1)
    scale = (MLA_NOPE + MLA_ROPE) ** -0.5 * np.log2(np.e)
    zeros = np.zeros((lp, MLA_ROPE))
    cq = np.concatenate([np.full((lp, MLA_NOPE), scale), cos * scale, zeros], -1)
    sq = np.concatenate([np.zeros((lp, MLA_NOPE)), sin * scale, zeros], -1)
    ck = np.concatenate([cos, zeros], -1)
    sk = np.concatenate([sin, zeros], -1)
    return tuple(jnp.asarray(t, F32) for t in (cq, sq, ck, sk))


def _chunk_id(pos):
    return jnp.where(pos < N_META, 0, 1 + ((pos - N_META) >> 6))


def _flash_kernel(q_ref, k_ref, vt_ref, o_ref, m_sc, l_sc, acc_sc, *, n_valid):
    i = pl.program_id(1)
    tq = q_ref.shape[1]
    lp = k_ref.shape[1]
    heads = range(MLA_HEADS)
    q = [q_ref[0, :, h * MLA_QK_PAD:(h + 1) * MLA_QK_PAD] for h in heads]
    m_sc[...] = jnp.full(m_sc.shape, -jnp.inf, F32)
    l_sc[...] = jnp.zeros(l_sc.shape, F32)
    acc_sc[...] = jnp.zeros(acc_sc.shape, F32)
    n_full = (i * tq) // FLASH_KEYS
    full_end = n_full * FLASH_KEYS
    q_cid = _chunk_id(i * tq + lax.broadcasted_iota(jnp.int32, (1, tq), 1))

    def chunk(start, width, masked):
        keys = pl.ds(start, width)
        s = [lax.dot_general(k_ref[0, keys, h * MLA_QK_PAD:(h + 1) * MLA_QK_PAD], q[h], _NT,
                             preferred_element_type=F32) for h in heads]
        if masked:
            k_pos = start + lax.broadcasted_iota(jnp.int32, (width, 1), 0)
            visible = (k_pos >= full_end) & (k_pos < n_valid) & (_chunk_id(k_pos) <= q_cid)
            s = [jnp.where(visible, s[h], -jnp.inf) for h in heads]
        m_prev = [m_sc[h] for h in heads]
        m_new = [jnp.maximum(m_prev[h], jnp.max(s[h], 0, keepdims=True)) for h in heads]
        p = [jnp.exp2(s[h] - m_new[h]) for h in heads]
        for h in heads:
            alpha = jnp.exp2(m_prev[h] - m_new[h])
            l_sc[h] = alpha * l_sc[h] + jnp.sum(p[h], 0, keepdims=True)
            acc_sc[h] = alpha * acc_sc[h] + jnp.dot(vt_ref[0, h * MLA_V:(h + 1) * MLA_V, keys],
                                                    p[h].astype(BF16), preferred_element_type=F32)
            m_sc[h] = m_new[h]

    def chunk_pair(j, carry):
        chunk(pl.multiple_of(j * (2 * FLASH_KEYS), 2 * FLASH_KEYS), 2 * FLASH_KEYS, False)
        return carry

    lax.fori_loop(0, n_full // 2, chunk_pair, 0)

    @pl.when(n_full % 2 == 1)
    def _():
        chunk(pl.multiple_of(full_end - FLASH_KEYS, FLASH_KEYS), FLASH_KEYS, False)

    for offset in range(0, FLASH_KEYS, math.gcd(FLASH_KEYS, tq)):
        width = -(-(offset + tq + N_META) // LANE) * LANE

        @pl.when(i * tq - full_end == offset)
        def _(width=width):
            chunk(pl.multiple_of(jnp.minimum(full_end, lp - width), LANE), width, True)

    for h in heads:
        o_ref[0, :, h * MLA_V:(h + 1) * MLA_V] = (acc_sc[h] / l_sc[h]).T.astype(o_ref.dtype)


def _flash(q, k, vt, n_valid):
    b, lp, _ = q.shape
    assert lp % LANE == 0 and SEQ_TILE % LANE == 0 and FLASH_KEYS % LANE == 0 and lp >= FLASH_KEYS + SEQ_TILE + LANE
    return pl.pallas_call(
        functools.partial(_flash_kernel, n_valid=n_valid),
        grid=(b, lp // SEQ_TILE),
        in_specs=[pl.BlockSpec((1, SEQ_TILE, MLA_HEADS * MLA_QK_PAD), lambda bi, i: (bi, i, 0)),
                  pl.BlockSpec((1, lp, MLA_HEADS * MLA_QK_PAD), lambda bi, i: (bi, 0, 0)),
                  pl.BlockSpec((1, MLA_WIDTH, lp), lambda bi, i: (bi, 0, 0))],
        out_specs=pl.BlockSpec((1, SEQ_TILE, MLA_WIDTH), lambda bi, i: (bi, i, 0)),
        out_shape=jax.ShapeDtypeStruct((b, lp, MLA_WIDTH), BF16),
        scratch_shapes=[pltpu.VMEM((MLA_HEADS, 1, SEQ_TILE), F32), pltpu.VMEM((MLA_HEADS, 1, SEQ_TILE), F32),
                        pltpu.VMEM((MLA_HEADS, MLA_V, SEQ_TILE), F32)],
        compiler_params=_cparams(("parallel", "arbitrary")),
        name="mla_flash",
    )(q, k, vt)


def _rwkv_prep_kernel(*refs, has_vres):
    (p_ref, prev_ref, mu_ref, w0_ref, w2_ref, a0_ref, a2_ref, g2_ref, kk_ref, ka_ref, rk_ref,
     bd_ref) = refs[:12]
    n_in = 16 if has_vres else 12
    r_out, k_out, v_out, kk_out, bb_out, lw_out, bonus_out, g_out = refs[n_in:]
    c = RWKV_WIDTH
    p = p_ref[0].astype(F32)
    tm = p.shape[0]
    first = pl.program_id(1) == 0
    prev_last = jnp.where(first, 0.0, prev_ref[0, PREV_ROWS - 1:PREV_ROWS, :].astype(F32))
    row = lax.broadcasted_iota(jnp.int32, (tm, 1), 0)
    prev = jnp.where(row == 0, prev_last, pltpu.roll(p, 1, 0))
    ps = p + (prev - p) * mu_ref[...]
    r = ps[:, :c]
    k = ps[:, c:2 * c]
    v = ps[:, 2 * c:3 * c]
    xwa = ps[:, 3 * c:3 * c + W_LORA + A_LORA]
    xg = ps[:, 3 * c + W_LORA + A_LORA:]
    zw = w0_ref[...] + jnp.dot(jnp.tanh(xwa).astype(BF16), w2_ref[...], preferred_element_type=F32)
    nz = -zw
    softplus = jnp.maximum(nz, 0.0) + jnp.log(1.0 + jnp.exp(-jnp.abs(nz)))
    log_decay = -jnp.exp(-softplus - 0.5)
    if has_vres:
        vf_ref, v0_ref, v1_ref, v2_ref = refs[12:16]
        lo = jnp.dot(v.astype(BF16), v1_ref[...], preferred_element_type=F32)
        gate = _sigmoid(v0_ref[...] + jnp.dot(lo.astype(BF16), v2_ref[...], preferred_element_type=F32))
        v = v + (vf_ref[0].astype(F32) - v) * gate
    a = _sigmoid(a0_ref[...] + jnp.dot(xwa.astype(BF16), a2_ref[...], preferred_element_type=F32))
    g = jnp.dot(_sigmoid(xg).astype(BF16), g2_ref[...], preferred_element_type=F32)
    bd = bd_ref[...]
    kk = k * kk_ref[...]
    kk = kk * lax.rsqrt(jnp.maximum(_split_dot(kk * kk, bd), 1e-24))
    k = k * (1.0 + (a - 1.0) * ka_ref[...])
    bonus = _split_dot(r * k * rk_ref[...], bd) * v
    r_out[0] = r.astype(r_out.dtype)
    k_out[0] = k.astype(k_out.dtype)
    v_out[0] = v.astype(v_out.dtype)
    kk_out[0] = kk.astype(kk_out.dtype)
    bb_out[0] = (kk * a).astype(bb_out.dtype)
    lw_out[0] = log_decay
    bonus_out[0] = bonus.astype(bonus_out.dtype)
    g_out[0] = g.astype(g_out.dtype)


def _rwkv_prep(pr, prm, v_first):
    b, lp, cols = pr.shape
    has_vres = v_first is not None
    c = RWKV_WIDTH
    row = lambda bi, i: (bi, i, 0)
    prev = lambda bi, i: (bi, jnp.maximum(i * (SEQ_TILE // PREV_ROWS) - 1, 0), 0)
    names = ["mu", "w0", "w2", "a0", "a2", "g2", "k_k", "k_a", "r_k", "bd"]
    args = [pr, pr] + [prm[n] for n in names]
    in_specs = [pl.BlockSpec((1, SEQ_TILE, cols), row), pl.BlockSpec((1, PREV_ROWS, cols), prev)]
    in_specs += [_const_spec(prm[n].shape) for n in names]
    if has_vres:
        args += [v_first, prm["v0"], prm["v1"], prm["v2"]]
        in_specs += [pl.BlockSpec((1, SEQ_TILE, c), row)]
        in_specs += [_const_spec(prm[n].shape) for n in ("v0", "v1", "v2")]
    out_spec = pl.BlockSpec((1, SEQ_TILE, c), row)
    return pl.pallas_call(
        functools.partial(_rwkv_prep_kernel, has_vres=has_vres),
        grid=(b, lp // SEQ_TILE),
        in_specs=in_specs,
        out_specs=[out_spec] * 8,
        out_shape=[jax.ShapeDtypeStruct((b, lp, c), F32 if n == 5 else BF16) for n in range(8)],
        compiler_params=_cparams(("parallel", "parallel")),
        name="rwkv_prep",
    )(*args)


_NN = (((1,), (0,)), ((), ()))
_NT = (((1,), (1,)), ((), ()))


def _mm(a, b, dims):
    return lax.dot_general(a.astype(BF16), b.astype(BF16), dims, preferred_element_type=F32)


def _rwkv_scan_kernel(r_ref, k_ref, v_ref, kk_ref, bb_ref, lw_ref, y_ref, s_sc):
    @pl.when(pl.program_id(1) == 0)
    def _():
        s_sc[...] = jnp.zeros(s_sc.shape, F32)

    c = SCAN_CHUNK
    n = 2 * c
    n_chunks = r_ref.shape[1] // c
    ri = lax.broadcasted_iota(jnp.int32, (c, c), 0)
    ci = lax.broadcasted_iota(jnp.int32, (c, c), 1)
    tri = jnp.where(ri >= ci, 1.0, 0.0).astype(BF16)
    row = lax.broadcasted_iota(jnp.int32, (2 * n, 2 * n), 0)
    col = lax.broadcasted_iota(jnp.int32, (2 * n, 2 * n), 1)
    t_idx = row & (c - 1)
    s_idx = col & (c - 1)
    causal = (t_idx > s_idx) | ((t_idx == s_idx) & (row >= n))
    eye = jnp.where(lax.broadcasted_iota(jnp.int32, (n, n), 0) == lax.broadcasted_iota(jnp.int32, (n, n), 1),
                    1.0, 0.0).astype(F32)
    head0 = lax.broadcasted_iota(jnp.int32, (1, LANE), 1) < RWKV_HEAD

    def stack(x):
        return jnp.concatenate([jnp.where(head0, x, 0.0), jnp.where(head0, 0.0, x)], axis=0)

    ar, bk, v_t, w_end = [], [], [], []
    for ch in range(n_chunks):
        rows = slice(ch * c, (ch + 1) * c)
        lw = lw_ref[0, rows, :]
        lw_hi = lw.astype(BF16)
        lw_lo = (lw - lw_hi.astype(F32)).astype(BF16)
        cum = (jnp.dot(tri, lw_hi, preferred_element_type=F32)
               + jnp.dot(tri, lw_lo, preferred_element_type=F32))
        w_incl = jnp.exp(cum)
        w_inv = jnp.exp(-cum)
        a_hat = -kk_ref[0, rows, :].astype(F32) * jnp.exp(cum - lw)
        r_hat = r_ref[0, rows, :].astype(F32) * w_incl
        b_hat = bb_ref[0, rows, :].astype(F32) * w_inv
        k_hat = k_ref[0, rows, :].astype(F32) * w_inv
        v = v_ref[0, rows, :].astype(F32)
        for pr in range(RWKV_PAIRS):
            sl = slice(pr * LANE, (pr + 1) * LANE)
            ar.append(jnp.concatenate([stack(a_hat[:, sl]), stack(r_hat[:, sl])], 0).astype(BF16))
            bk.append(jnp.concatenate([stack(b_hat[:, sl]), stack(k_hat[:, sl])], 0).astype(BF16))
            v_t.append(stack(v[:, sl]).T.astype(BF16))
            w_end.append(w_incl[c - 1:c, sl])
    every = range(len(ar))
    gram = [jnp.where(causal, _mm(ar[g], bk[g], _NT), 0.0) for g in every]
    pw = [gram[g][:n, :n] for g in every]
    t_inv = [eye + pw[g] for g in every]
    pw = [_mm(pw[g], pw[g], _NN) for g in every]
    for _ in range(4):
        both = [_mm(pw[g], jnp.concatenate([pw[g], t_inv[g]], 1), _NN) for g in every]
        pw = [both[g][:, :n] for g in every]
        t_inv = [t_inv[g] + both[g][:, n:] for g in every]
    t_inv = [(t_inv[g] + _mm(pw[g], t_inv[g], _NN)).astype(BF16) for g in every]
    va = [_mm(v_t[g], gram[g][:n, n:], _NT) for g in every]
    q_bk = [gram[g][n:, :].astype(BF16) for g in every]

    state = [s_sc[pr] for pr in range(RWKV_PAIRS)]
    for ch in range(n_chunks):
        gs = [ch * RWKV_PAIRS + pr for pr in range(RWKV_PAIRS)]
        s_ar = [_mm(state[pr], ar[g], _NT) for pr, g in enumerate(gs)]
        u_t = [_mm(s_ar[pr][:, :n] + va[g], t_inv[g], _NT) for pr, g in enumerate(gs)]
        uv = [jnp.concatenate([u_t[pr].astype(BF16), v_t[g]], 1) for pr, g in enumerate(gs)]
        state = [(state[pr] + _mm(uv[pr], bk[g], _NN)) * w_end[g] for pr, g in enumerate(gs)]
        for pr, g in enumerate(gs):
            y = (s_ar[pr][:, n:] + _mm(uv[pr], q_bk[g], _NT)).T
            y_ref[0, ch * c:(ch + 1) * c, pr * LANE:(pr + 1) * LANE] = y[:c] + y[c:]
    for pr in range(RWKV_PAIRS):
        s_sc[pr] = state[pr]


def _rwkv_scan(r, k, v, kk, bb, lw):
    b, lp, c = r.shape
    rows = SCAN_STEP_CHUNKS * SCAN_CHUNK
    spec = pl.BlockSpec((1, rows, c), lambda bi, i: (bi, i, 0))
    return pl.pallas_call(
        _rwkv_scan_kernel,
        grid=(b, lp // rows),
        in_specs=[spec] * 6,
        out_specs=spec,
        out_shape=jax.ShapeDtypeStruct((b, lp, c), F32),
        scratch_shapes=[pltpu.VMEM((RWKV_PAIRS, LANE, LANE), F32)],
        compiler_params=_cparams(("parallel", "arbitrary")),
        name="rwkv_scan",
    )(r, k, v, kk, bb, lw)


def _outproj_kernel(ys_ref, bonus_ref, g_ref, ymla_ref, lnw_ref, lnb_ref, bd_ref, wo_ref, h_ref, o_ref):
    y = ys_ref[...]
    bd = bd_ref[...]
    inv_n = 1.0 / RWKV_HEAD
    d = y - _split_dot(y, bd) * inv_n
    var = _split_dot(d * d, bd) * inv_n
    yn = d * lax.rsqrt(var + GN_EPS) * lnw_ref[...] + lnb_ref[...]
    yr = ((yn + bonus_ref[...].astype(F32)) * g_ref[...].astype(F32)).astype(BF16)
    o_ref[...] = (h_ref[...]
                  + jnp.dot(ymla_ref[...], wo_ref[:MLA_WIDTH, :], preferred_element_type=F32)
                  + jnp.dot(yr, wo_ref[MLA_WIDTH:, :], preferred_element_type=F32))


def _outproj(ys, bonus, g, ymla, ln_w, ln_b, bd, wo, h):
    tp = h.shape[0]
    c = RWKV_WIDTH
    rc = pl.BlockSpec((ROW_TILE, c), lambda i: (i, 0))
    rd = pl.BlockSpec((ROW_TILE, D_MODEL), lambda i: (i, 0))
    return pl.pallas_call(
        _outproj_kernel,
        grid=(tp // ROW_TILE,),
        in_specs=[rc, rc, rc, rc, _const_spec((1, c)), _const_spec((1, c)), _const_spec(bd.shape),
                  _const_spec(wo.shape), rd],
        out_specs=rd,
        out_shape=jax.ShapeDtypeStruct((tp, D_MODEL), F32),
        compiler_params=_cparams(("parallel",)),
        name="outproj",
    )(ys, bonus, g, ymla, ln_w, ln_b, bd, wo, h)


def _ffn_chunks(u, wg_ref, wu_ref, wd_ref, idx, ff, chunk):
    acc = None
    for c0 in range(0, ff, chunk):
        sl = slice(c0, c0 + chunk)
        gate = jnp.dot(u, wg_ref[idx + (slice(None), sl)], preferred_element_type=F32)
        up = jnp.dot(u, wu_ref[idx + (slice(None), sl)], preferred_element_type=F32)
        act = (gate * _sigmoid(gate) * up).astype(BF16)
        part = jnp.dot(act, wd_ref[idx + (sl, slice(None))], preferred_element_type=F32)
        acc = part if acc is None else acc + part
    return acc


def _dense_ffn_kernel(h_ref, g_ref, wg_ref, wu_ref, wd_ref, o_ref, *, ff, chunk):
    h = h_ref[...]
    u = _rms(h, g_ref[...]).astype(BF16)
    o_ref[...] = h + _ffn_chunks(u, wg_ref, wu_ref, wd_ref, (), ff, chunk)


def _dense_ffn(h, g, wg, wu, wd):
    tp = h.shape[0]
    ff = wg.shape[1]
    rd = pl.BlockSpec((ROW_TILE, D_MODEL), lambda i: (i, 0))
    once = lambda shape: pl.BlockSpec(shape, lambda i: (0, 0), pipeline_mode=pl.Buffered(1))
    return pl.pallas_call(
        functools.partial(_dense_ffn_kernel, ff=ff, chunk=256),
        grid=(tp // ROW_TILE,),
        in_specs=[rd, _const_spec((1, D_MODEL)), once(wg.shape), once(wu.shape), once(wd.shape)],
        out_specs=rd,
        out_shape=jax.ShapeDtypeStruct((tp, D_MODEL), F32),
        compiler_params=_cparams(("parallel",)),
        name="dense_ffn",
    )(h, g, wg, wu, wd)


def _router_kernel(h_ref, g_ref, wr_ref, br_ref, u_ref, meta_ref, meta_t_ref, cnt_ref, carry_sc, *, n_valid):
    @pl.when((pl.program_id(0) == 0) & (pl.program_id(1) == 0))
    def _():
        carry_sc[...] = jnp.zeros(carry_sc.shape, F32)

    tm = h_ref.shape[1]
    u = _rms(h_ref[0], g_ref[...])
    u_ref[0] = u
    logits = jnp.dot(u, wr_ref[...], precision=HIGHEST, preferred_element_type=F32) + br_ref[...]
    lane = lax.broadcasted_iota(jnp.int32, (tm, LANE), 1).astype(F32)
    logits = jnp.where(lane < N_EXPERTS, logits, -jnp.inf)
    top0 = jnp.max(logits, -1, keepdims=True)
    e0 = jnp.min(jnp.where(logits == top0, lane, float(LANE)), -1, keepdims=True)
    rest = jnp.where(lane == e0, -jnp.inf, logits)
    top1 = jnp.max(rest, -1, keepdims=True)
    e1 = jnp.min(jnp.where(rest == top1, lane, float(LANE)), -1, keepdims=True)
    ex = jnp.exp(top1 - top0)
    g0 = 1.0 / (1.0 + ex)
    g1 = ex / (1.0 + ex)
    pos = pl.program_id(1) * tm + lax.broadcasted_iota(jnp.int32, (tm, 1), 0)
    valid = pos < n_valid
    oh0 = jnp.where((lane == e0) & valid, 1.0, 0.0)
    oh1 = jnp.where((lane == e1) & valid, 1.0, 0.0)
    ri = lax.broadcasted_iota(jnp.int32, (tm, tm), 0)
    ci = lax.broadcasted_iota(jnp.int32, (tm, tm), 1)
    before = jnp.where(ri > ci, 1.0, 0.0).astype(BF16)
    both = oh0 + oh1
    seen = carry_sc[...] + jnp.dot(before, both.astype(BF16), preferred_element_type=F32)
    rank0 = jnp.sum(jnp.where(lane == e0, seen, 0.0), -1, keepdims=True)
    rank1 = jnp.sum(jnp.where(lane == e1, seen, 0.0), -1, keepdims=True)
    carry_sc[...] = carry_sc[...] + jnp.sum(both, 0, keepdims=True)
    cnt_ref[...] = jnp.broadcast_to(carry_sc[...], cnt_ref.shape)
    vf = jnp.where(valid, 1.0, 0.0)
    meta = jnp.where(lane == 0, e0, 0.0)
    meta = jnp.where(lane == 1, e1, meta)
    meta = jnp.where(lane == 2, rank0, meta)
    meta = jnp.where(lane == 3, rank1, meta)
    meta = jnp.where(lane == 4, g0 * vf, meta)
    meta = jnp.where(lane == 5, g1 * vf, meta)
    meta = jnp.where(lane == 6, vf, meta)
    meta_ref[0] = meta
    meta_t_ref[...] = meta.T[:8]


def _router(h, g, wr, br, n_valid):
    b, lp, _ = h.shape
    row = lambda bi, i: (bi, i, 0)
    return pl.pallas_call(
        functools.partial(_router_kernel, n_valid=n_valid),
        grid=(b, lp // SEQ_TILE),
        in_specs=[pl.BlockSpec((1, SEQ_TILE, D_MODEL), row), _const_spec((1, D_MODEL)),
                  _const_spec(wr.shape), _const_spec(br.shape)],
        out_specs=[pl.BlockSpec((1, SEQ_TILE, D_MODEL), row), pl.BlockSpec((1, SEQ_TILE, LANE), row),
                   pl.BlockSpec((8, SEQ_TILE), lambda bi, i: (0, bi * (lp // SEQ_TILE) + i)),
                   _const_spec((8, LANE))],
        out_shape=[jax.ShapeDtypeStruct((b, lp, D_MODEL), F32), jax.ShapeDtypeStruct((b, lp, LANE), F32),
                   jax.ShapeDtypeStruct((8, b * lp), F32), jax.ShapeDtypeStruct((8, LANE), F32)],
        scratch_shapes=[pltpu.VMEM((1, LANE), F32)],
        compiler_params=_cparams(("arbitrary", "arbitrary")),
        name="moe_router",
    )(h, g, wr, br)


def _dispatch_kernel(s0_ref, s1_ref, u_ref, xb_in_ref, xb_ref, sem):
    del xb_in_ref
    tm = u_ref.shape[0]

    def copies(rw):
        src = u_ref.at[pl.ds(rw, 1), :]
        return [pltpu.make_async_copy(src, xb_ref.at[pl.ds(s_ref[0, 0, rw], 1), :], sem)
                for s_ref in (s0_ref, s1_ref)]

    def start(j, carry):
        for kq in range(DMA_UNROLL):
            for cp in copies(j * DMA_UNROLL + kq):
                cp.start()
        return carry

    def wait(rw, carry):
        for cp in copies(rw):
            cp.wait()
        return carry

    lax.fori_loop(0, tm // DMA_UNROLL, start, 0)
    lax.fori_loop(0, tm, wait, 0)


def _dispatch(slot0, slot1, u, n_rows, tm):
    tp = u.shape[0]
    sspec = pl.BlockSpec((1, 1, tm), lambda i: (i, 0, 0), memory_space=pltpu.SMEM)
    xb0 = jnp.zeros((n_rows, D_MODEL), F32)
    return pl.pallas_call(
        _dispatch_kernel,
        grid=(tp // tm,),
        in_specs=[sspec, sspec, pl.BlockSpec((tm, D_MODEL), lambda i: (i, 0)),
                  pl.BlockSpec(memory_space=pl.ANY)],
        out_specs=pl.BlockSpec(memory_space=pl.ANY),
        out_shape=jax.ShapeDtypeStruct((n_rows, D_MODEL), F32),
        scratch_shapes=[pltpu.SemaphoreType.DMA(())],
        input_output_aliases={3: 0},
        compiler_params=_cparams(("arbitrary",)),
        name="moe_dispatch",
    )(slot0.reshape(tp // tm, 1, tm), slot1.reshape(tp // tm, 1, tm), u, xb0)


def _expert_kernel(be_ref, x_ref, wg_ref, wu_ref, wd_ref, o_ref, *, ff, chunk):
    del be_ref
    u = x_ref[...].astype(BF16)
    o_ref[...] = _ffn_chunks(u, wg_ref, wu_ref, wd_ref, (0,), ff, chunk)


def _experts(block_e, xb, wg, wu, wd):
    n_slots = block_e.shape[0] * MOE_BLOCK
    ff = wg.shape[2]
    rows = pl.BlockSpec((MOE_BLOCK, D_MODEL), lambda i, be: (i, 0))
    return pl.pallas_call(
        functools.partial(_expert_kernel, ff=ff, chunk=512),
        grid_spec=pltpu.PrefetchScalarGridSpec(
            num_scalar_prefetch=1,
            grid=(n_slots // MOE_BLOCK,),
            in_specs=[rows,
                      pl.BlockSpec((1, D_MODEL, ff), lambda i, be: (be[i], 0, 0)),
                      pl.BlockSpec((1, D_MODEL, ff), lambda i, be: (be[i], 0, 0)),
                      pl.BlockSpec((1, ff, D_MODEL), lambda i, be: (be[i], 0, 0))],
            out_specs=rows),
        out_shape=jax.ShapeDtypeStruct((n_slots, D_MODEL), F32),
        compiler_params=_cparams(("arbitrary",), 60 * 1024 * 1024),
        name="moe_experts",
    )(block_e, xb, wg, wu, wd)


def _combine_kernel(s0_ref, s1_ref, n0_ref, n1_ref, h_ref, gates_ref, fn_ref, yb_ref, o_ref, y0_sc, y1_sc, sems):
    t = pl.program_id(0)
    tm = h_ref.shape[0]
    cur = t % 2

    def copies(rw, a_ref, b_ref, buf):
        return [pltpu.make_async_copy(yb_ref.at[pl.ds(s_ref[0, 0, rw], 1), :], y_sc.at[buf, pl.ds(rw, 1), :],
                                      sems.at[buf])
                for s_ref, y_sc in ((a_ref, y0_sc), (b_ref, y1_sc))]

    def issue(a_ref, b_ref, buf):
        def body(j, carry):
            for kq in range(DMA_UNROLL):
                for cp in copies(j * DMA_UNROLL + kq, a_ref, b_ref, buf):
                    cp.start()
            return carry
        lax.fori_loop(0, tm // DMA_UNROLL, body, 0)

    @pl.when(t == 0)
    def _():
        issue(s0_ref, s1_ref, 0)

    @pl.when(t + 1 < pl.num_programs(0))
    def _():
        issue(n0_ref, n1_ref, 1 - cur)

    def wait(rw, carry):
        for cp in copies(rw, s0_ref, s1_ref, cur):
            cp.wait()
        return carry

    lax.fori_loop(0, tm, wait, 0)
    gates = gates_ref[...]
    f = y0_sc[cur] * gates[:, 4:5] + y1_sc[cur] * gates[:, 5:6]
    o_ref[...] = _rms(h_ref[...] + f, fn_ref[...])


def _combine(slot0, slot1, h, meta, fn, yb, tm):
    tp = h.shape[0]
    n_tiles = tp // tm
    sspec = pl.BlockSpec((1, 1, tm), lambda i: (i, 0, 0), memory_space=pltpu.SMEM)
    nspec = pl.BlockSpec((1, 1, tm), lambda i: (jnp.minimum(i + 1, n_tiles - 1), 0, 0), memory_space=pltpu.SMEM)
    rd = pl.BlockSpec((tm, D_MODEL), lambda i: (i, 0))
    s0 = slot0.reshape(n_tiles, 1, tm)
    s1 = slot1.reshape(n_tiles, 1, tm)
    return pl.pallas_call(
        _combine_kernel,
        grid=(n_tiles,),
        in_specs=[sspec, sspec, nspec, nspec, rd, pl.BlockSpec((tm, LANE), lambda i: (i, 0)),
                  _const_spec((1, D_MODEL)), pl.BlockSpec(memory_space=pl.ANY)],
        out_specs=rd,
        out_shape=jax.ShapeDtypeStruct((tp, D_MODEL), F32),
        scratch_shapes=[pltpu.VMEM((2, tm, D_MODEL), F32), pltpu.VMEM((2, tm, D_MODEL), F32),
                        pltpu.SemaphoreType.DMA((2,))],
        compiler_params=_cparams(("arbitrary",)),
        name="moe_combine",
    )(s0, s1, s0, s1, h, meta, fn, yb)


def _final_norm_kernel(h_ref, g_ref, o_ref):
    o_ref[...] = _rms(h_ref[...], g_ref[...])


def _final_norm(h, g):
    tp = h.shape[0]
    rd = pl.BlockSpec((ROW_TILE, D_MODEL), lambda i: (i, 0))
    return pl.pallas_call(
        _final_norm_kernel,
        grid=(tp // ROW_TILE,),
        in_specs=[rd, _const_spec((1, D_MODEL))],
        out_specs=rd,
        out_shape=jax.ShapeDtypeStruct((tp, D_MODEL), F32),
        compiler_params=_cparams(("parallel",)),
        name="final_norm",
    )(h, g)


def _rot_cols(w):
    half = MLA_ROPE // 2
    return jnp.concatenate([-w[..., half:], w[..., :half]], -1)


def _prep_inproj(w_in):
    m_q = w_in[:, :MLA_Q_LORA + MLA_KV_LORA]
    k_rope = w_in[:, MLA_Q_LORA + MLA_KV_LORA:MLA_Q_LORA + MLA_KV_LORA + MLA_ROPE]
    w_m = jnp.concatenate([m_q, k_rope, _rot_cols(k_rope)], -1)
    w_r = w_in[:, MLA_Q_LORA + MLA_KV_LORA + MLA_ROPE:]
    return w_r.astype(BF16), w_m.astype(BF16)


def _prep_wq(w_uq):
    w = w_uq.reshape(MLA_Q_LORA, MLA_HEADS, MLA_NOPE + MLA_ROPE)
    rope = w[..., MLA_NOPE:]
    return jnp.concatenate([w, _rot_cols(rope)], -1).reshape(MLA_Q_LORA, MLA_HEADS * MLA_QK_PAD).astype(BF16)


def _prep_wkv(w_ukv):
    w = w_ukv.reshape(MLA_KV_LORA, MLA_HEADS, MLA_NOPE + MLA_V)
    k_nope = w[..., :MLA_NOPE].reshape(MLA_KV_LORA, MLA_HEADS * MLA_NOPE)
    v = w[..., MLA_NOPE:].reshape(MLA_KV_LORA, MLA_HEADS * MLA_V)
    return jnp.concatenate([k_nope, v], -1).astype(BF16)


def _row(x):
    return x.reshape(1, -1).astype(F32)


def _block_diag_ones():
    idx = np.arange(RWKV_WIDTH) // RWKV_HEAD
    return jnp.asarray(idx[:, None] == idx[None, :], BF16)


def kernel(x, meta_tokens, attn_norm, ffn_norm, final_norm, w_in, w_out, mla_q_norm, mla_kv_norm, mla_w_uq, mla_w_ukv, rwkv_mu, rwkv_w0, rwkv_w2, rwkv_a0, rwkv_a2, rwkv_g2, rwkv_k_k, rwkv_k_a, rwkv_r_k, rwkv_ln_w, rwkv_ln_b, rwkv_v0, rwkv_v1, rwkv_v2, ffn_w_gate, ffn_w_up, ffn_w_down, moe_router, moe_router_bias, moe_w_gate, moe_w_up, moe_w_down):
    b, seq, d = x.shape
    depth = attn_norm.shape[0]
    n_valid = N_META + seq
    lp = -(-n_valid // SEQ_TILE) * SEQ_TILE
    tp = b * lp
    assert d == D_MODEL and tp % ROW_TILE == 0 and lp % SCAN_CHUNK == 0

    meta = jnp.broadcast_to(meta_tokens[None].astype(x.dtype), (b, N_META, d))
    h = jnp.concatenate([meta, x, jnp.zeros((b, lp - n_valid, d), x.dtype)], axis=1).reshape(tp, d)

    tabs = _rope_tables(lp)
    bd = _block_diag_ones()
    zeros_wa = jnp.zeros((W_LORA, RWKV_WIDTH), F32)
    v_first = None
    for i in range(depth):
        w_r, w_m = _prep_inproj(w_in[i])
        pr, pm = _norm_inproj(h, _row(attn_norm[i]), w_r, w_m)

        q, k, v = _mla_up(pm.reshape(b, lp, -1), _row(mla_q_norm[i]), _row(mla_kv_norm[i]),
                          _prep_wq(mla_w_uq[i]), _prep_wkv(mla_w_ukv[i]), tabs)
        y_mla = _flash(q, k, v, n_valid)

        prm = {
            "mu": _row(rwkv_mu[i]), "w0": _row(rwkv_w0[i]), "a0": _row(rwkv_a0[i]),
            "w2": jnp.concatenate([rwkv_w2[i], zeros_wa], 0).astype(BF16),
            "a2": jnp.concatenate([zeros_wa, rwkv_a2[i]], 0).astype(BF16),
            "g2": rwkv_g2[i].astype(BF16),
            "k_k": _row(rwkv_k_k[i]), "k_a": _row(rwkv_k_a[i]), "r_k": _row(rwkv_r_k[i]), "bd": bd,
        }
        if i > 0:
            prm["v0"] = _row(rwkv_v0[i - 1])
            prm["v1"] = jnp.pad(rwkv_v1[i - 1], ((0, 0), (0, LANE - V_LORA))).astype(BF16)
            prm["v2"] = jnp.pad(rwkv_v2[i - 1], ((0, LANE - V_LORA), (0, 0))).astype(BF16)
        r_, k_, v_, kk_, bb_, lw_, bonus, gate = _rwkv_prep(pr.reshape(b, lp, -1), prm, v_first)
        if i == 0:
            v_first = v_
        ys = _rwkv_scan(r_, k_, v_, kk_, bb_, lw_)

        flat = lambda t: t.reshape(tp, -1)
        h = _outproj(flat(ys), flat(bonus), flat(gate), flat(y_mla), _row(rwkv_ln_w[i]), _row(rwkv_ln_b[i]),
                     bd, w_out[i].astype(BF16), h)

        j = i // 2
        last = i == depth - 1
        if i % 2 == 0:
            h = _dense_ffn(h, _row(ffn_norm[i]), ffn_w_gate[j].astype(BF16), ffn_w_up[j].astype(BF16),
                           ffn_w_down[j].astype(BF16))
            if last:
                h = _final_norm(h, _row(final_norm))
        else:
            h = _moe(h, b, lp, n_valid, _row(ffn_norm[i]), moe_router[j], moe_router_bias[j],
                     moe_w_gate[j], moe_w_up[j], moe_w_down[j], _row(final_norm) if last else None)
            if not last:
                raise NotImplementedError("an MoE layer that is not the last layer")
    return h.reshape(b, lp, d)[:, N_META:n_valid]


def _moe(h, b, lp, n_valid, g, router, router_bias, w_gate, w_up, w_down, final_g):
    tp = b * lp
    wr = jnp.pad(router, ((0, 0), (0, LANE - N_EXPERTS))).astype(F32)
    br = jnp.pad(router_bias, (0, LANE - N_EXPERTS)).reshape(1, LANE).astype(F32)
    u, meta, meta_t, cnt = _router(h.reshape(b, lp, D_MODEL), g, wr, br, n_valid)
    meta = meta.reshape(tp, LANE)

    counts = cnt[0, :N_EXPERTS].astype(jnp.int32)
    padded = (counts + MOE_BLOCK - 1) // MOE_BLOCK * MOE_BLOCK
    pend = jnp.cumsum(padded)
    pstart = pend - padded
    n_assign = b * n_valid * TOP_K
    n_blocks = (n_assign + N_EXPERTS * (MOE_BLOCK - 1)) // MOE_BLOCK + 1
    n_slots = n_blocks * MOE_BLOCK
    block_start = jnp.arange(n_blocks, dtype=jnp.int32) * MOE_BLOCK
    block_e = jnp.minimum(jnp.sum((pend[None, :] <= block_start[:, None]).astype(jnp.int32), -1),
                          N_EXPERTS - 1)
    e0, e1, rank0, rank1 = (meta_t[n].astype(jnp.int32) for n in range(4))
    valid = meta_t[6] > 0.5
    slot0 = pstart[e0] + rank0
    slot1 = pstart[e1] + rank1
    spare = n_slots + jnp.arange(tp, dtype=jnp.int32) % LANE
    d0 = jnp.where(valid, slot0, spare)
    d1 = jnp.where(valid, slot1, spare + LANE)
    c0 = jnp.where(valid, slot0, 0)
    c1 = jnp.where(valid, slot1, 0)

    xb = _dispatch(d0, d1, u.reshape(tp, D_MODEL), n_slots + 2 * LANE, LANE)
    yb = _experts(block_e, xb, w_gate.astype(BF16), w_up.astype(BF16), w_down.astype(BF16))
    return _combine(c0, c1, h, meta, final_g, yb, LANE)
```

```python
import functools
import math

import numpy as np
import jax
import jax.numpy as jnp
from jax import lax
from jax.experimental import pallas as pl
from jax.experimental.pallas import tpu as pltpu

F32 = jnp.float32
BF16 = jnp.bfloat16

D_MODEL = 1024
CHUNK = 64
N_META = 16
RMS_EPS = 1e-6

MLA_HEADS = 4
MLA_NOPE = 128
MLA_ROPE = 64
MLA_V = 128
MLA_Q_LORA = 512
MLA_KV_LORA = 256
ROPE_THETA = 10000.0
MLA_WIDTH = MLA_HEADS * MLA_V
MLA_QK_PAD = 256
MLA_COLS_PAD = MLA_Q_LORA + MLA_KV_LORA + 2 * MLA_ROPE

RWKV_WIDTH = 512
RWKV_HEAD = 64
RWKV_HEADS = RWKV_WIDTH // RWKV_HEAD
RWKV_PAIRS = RWKV_HEADS // 2
W_LORA = 64
A_LORA = 64
V_LORA = 32
G_LORA = 128
GN_EPS = 64e-5
RWKV_COLS = 3 * RWKV_WIDTH + W_LORA + A_LORA + G_LORA
SCAN_CHUNK = 64
SCAN_STEP_CHUNKS = 6

N_EXPERTS = 8
TOP_K = 2
MOE_BLOCK = 256

LANE = 128
FRONT_PAD = -N_META % LANE
FIRST = FRONT_PAD
SEQ_TILE = 384
FLASH_KEYS = 256
ROW_TILE = 512
PREV_ROWS = 16
DMA_UNROLL = 8
VMEM_LIMIT = 56 * 1024 * 1024


def _cparams(sem, vmem=VMEM_LIMIT):
    return pltpu.CompilerParams(dimension_semantics=sem, vmem_limit_bytes=vmem)


def _rms(x, g):
    return x * lax.rsqrt(jnp.mean(x * x, -1, keepdims=True) + RMS_EPS) * g


def _sigmoid(x):
    return 1.0 / (1.0 + jnp.exp(-x))


def _split(x):
    hi = x.astype(BF16)
    return hi, (x - hi.astype(F32)).astype(BF16)


def _head_sum(x, bd):
    return jnp.dot(x.astype(BF16), bd, preferred_element_type=F32)


def _const_spec(shape):
    nd = len(shape)
    return pl.BlockSpec(shape, lambda *_: (0,) * nd)


def _norm_inproj_kernel(h_ref, g_ref, wr_ref, wm_ref, pr_ref, pm_ref):
    u = _rms(h_ref[...], g_ref[...]).astype(BF16)
    pr_ref[...] = jnp.dot(u, wr_ref[...], preferred_element_type=F32).astype(pr_ref.dtype)
    pm_ref[...] = jnp.dot(u, wm_ref[...], preferred_element_type=F32).astype(pm_ref.dtype)


def _norm_inproj(h, g, w_r, w_m):
    tp = h.shape[0]
    return pl.pallas_call(
        _norm_inproj_kernel,
        grid=(tp // ROW_TILE,),
        in_specs=[pl.BlockSpec((ROW_TILE, D_MODEL), lambda i: (i, 0)),
                  _const_spec((1, D_MODEL)),
                  _const_spec(w_r.shape), _const_spec(w_m.shape)],
        out_specs=[pl.BlockSpec((ROW_TILE, w_r.shape[1]), lambda i: (i, 0)),
                   pl.BlockSpec((ROW_TILE, w_m.shape[1]), lambda i: (i, 0))],
        out_shape=[jax.ShapeDtypeStruct((tp, w_r.shape[1]), BF16),
                   jax.ShapeDtypeStruct((tp, w_m.shape[1]), BF16)],
        compiler_params=_cparams(("parallel",)),
        name="norm_inproj",
    )(h, g, w_r, w_m)


def _mla_up_kernel(pm_ref, qn_ref, kvn_ref, wq_ref, wkv_ref, cq_ref, sq_ref, ck_ref, sk_ref,
                   q_ref, k_ref, vt_ref):
    pm = pm_ref[0].astype(F32)
    c_q = pm[:, :MLA_Q_LORA]
    c_kv = pm[:, MLA_Q_LORA:MLA_Q_LORA + MLA_KV_LORA]
    k_r = pm[:, MLA_Q_LORA + MLA_KV_LORA:]
    q = jnp.dot(_rms(c_q, qn_ref[...]).astype(BF16), wq_ref[...], preferred_element_type=F32)
    kv = jnp.dot(_rms(c_kv, kvn_ref[...]).astype(BF16), wkv_ref[...], preferred_element_type=F32)
    k_rope = (k_r * ck_ref[...] + pltpu.roll(k_r, MLA_ROPE, 1) * sk_ref[...]).astype(BF16)
    cq = cq_ref[...]
    sq = sq_ref[...]
    for hd in range(MLA_HEADS):
        qh = q[:, hd * MLA_QK_PAD:(hd + 1) * MLA_QK_PAD]
        qh = qh * cq + pltpu.roll(qh, MLA_QK_PAD - MLA_ROPE, 1) * sq
        q_ref[0, :, hd * MLA_QK_PAD:(hd + 1) * MLA_QK_PAD] = qh.astype(BF16)
        k_ref[0, :, hd * MLA_QK_PAD:hd * MLA_QK_PAD + MLA_NOPE] = (
            kv[:, hd * MLA_NOPE:(hd + 1) * MLA_NOPE].astype(BF16))
        k_ref[0, :, hd * MLA_QK_PAD + MLA_NOPE:(hd + 1) * MLA_QK_PAD] = k_rope
    vt_ref[0] = kv[:, MLA_HEADS * MLA_NOPE:].T.astype(BF16)


def _mla_up(pm, q_norm, kv_norm, wq, wkv, tabs):
    b, lp, _ = pm.shape
    cq, sq, ck, sk = tabs
    qk_w = MLA_HEADS * MLA_QK_PAD
    row = lambda bi, i: (bi, i, 0)
    tab = lambda bi, i: (i, 0)
    return pl.pallas_call(
        _mla_up_kernel,
        grid=(b, lp // SEQ_TILE),
        in_specs=[pl.BlockSpec((1, SEQ_TILE, MLA_COLS_PAD), row),
                  _const_spec((1, MLA_Q_LORA)), _const_spec((1, MLA_KV_LORA)),
                  _const_spec(wq.shape), _const_spec(wkv.shape),
                  pl.BlockSpec((SEQ_TILE, MLA_QK_PAD), tab), pl.BlockSpec((SEQ_TILE, MLA_QK_PAD), tab),
                  pl.BlockSpec((SEQ_TILE, LANE), tab), pl.BlockSpec((SEQ_TILE, LANE), tab)],
        out_specs=[pl.BlockSpec((1, SEQ_TILE, qk_w), row),
                   pl.BlockSpec((1, SEQ_TILE, qk_w), row),
                   pl.BlockSpec((1, MLA_WIDTH, SEQ_TILE), lambda bi, i: (bi, 0, i))],
        out_shape=[jax.ShapeDtypeStruct((b, lp, qk_w), BF16),
                   jax.ShapeDtypeStruct((b, lp, qk_w), BF16),
                   jax.ShapeDtypeStruct((b, MLA_WIDTH, lp), BF16)],
        compiler_params=_cparams(("parallel", "parallel")),
        name="mla_up",
    )(pm, q_norm, kv_norm, wq, wkv, cq, sq, ck, sk)


def _rope_tables(lp):
    half = MLA_ROPE // 2
    inv = (np.float32(ROPE_THETA) ** (-np.arange(half, dtype=np.float32) / np.float32(half))).astype(np.float32)
    pos = np.maximum(np.arange(lp) - FIRST, 0).astype(np.float32)
    ang = (pos[:, None] * inv[None, :]).astype(np.float64)
    cos = np.concatenate([np.cos(ang), np.cos(ang)], -1)
    sin = np.concatenate([np.sin(ang), np.sin(ang)], -1)
    scale = (MLA_NOPE + MLA_ROPE) ** -0.5 * np.log2(np.e)
    zeros = np.zeros((lp, MLA_ROPE))
    cq = np.concatenate([np.full((lp, MLA_NOPE), scale), cos * scale, zeros], -1)
    sq = np.concatenate([np.zeros((lp, MLA_NOPE)), sin * scale, zeros], -1)
    ck = np.concatenate([cos, zeros], -1)
    sk = np.concatenate([sin, zeros], -1)
    return tuple(jnp.asarray(t, F32) for t in (cq, sq, ck, sk))


def _chunk_id(row):
    frame = row - (FIRST + N_META)
    return jnp.where(frame < 0, 0, 1 + (frame >> 6))


def _flash_kernel(q_ref, k_ref, vt_ref, o_ref, m_sc, l_sc, acc_sc, *, n_end):
    i = pl.program_id(1)
    tq = q_ref.shape[1]
    lp = k_ref.shape[1]
    heads = range(MLA_HEADS)
    q = [q_ref[0, :, h * MLA_QK_PAD:(h + 1) * MLA_QK_PAD] for h in heads]
    m_sc[...] = jnp.full(m_sc.shape, -jnp.inf, F32)
    l_sc[...] = jnp.zeros(l_sc.shape, F32)
    acc_sc[...] = jnp.zeros(acc_sc.shape, F32)
    n_full = (i * tq) // FLASH_KEYS
    full_end = n_full * FLASH_KEYS
    q_cid = _chunk_id(i * tq + lax.broadcasted_iota(jnp.int32, (1, tq), 1))

    def chunk(start, width, mask):
        keys = pl.ds(start, width)
        s = [lax.dot_general(k_ref[0, keys, h * MLA_QK_PAD:(h + 1) * MLA_QK_PAD], q[h], _NT,
                             preferred_element_type=F32) for h in heads]
        if mask is not None:
            k_pos = start + lax.broadcasted_iota(jnp.int32, (width, 1), 0)
            visible = k_pos >= FIRST
            if mask == "causal":
                visible = visible & (k_pos >= full_end) & (k_pos < n_end) & (_chunk_id(k_pos) <= q_cid)
            s = [jnp.where(visible, s[h], -jnp.inf) for h in heads]
        m_prev = [m_sc[h] for h in heads]
        m_new = [jnp.maximum(m_prev[h], jnp.max(s[h], 0, keepdims=True)) for h in heads]
        p = [jnp.exp2(s[h] - m_new[h]) for h in heads]
        for h in heads:
            alpha = jnp.exp2(m_prev[h] - m_new[h])
            l_sc[h] = alpha * l_sc[h] + jnp.sum(p[h], 0, keepdims=True)
            acc_sc[h] = alpha * acc_sc[h] + jnp.dot(vt_ref[0, h * MLA_V:(h + 1) * MLA_V, keys],
                                                    p[h].astype(BF16), preferred_element_type=F32)
            m_sc[h] = m_new[h]

    def chunk_pair(j, carry):
        chunk(pl.multiple_of(j * (2 * FLASH_KEYS), 2 * FLASH_KEYS), 2 * FLASH_KEYS, None)
        return carry

    @pl.when(n_full >= 2)
    def _():
        chunk(0, 2 * FLASH_KEYS, "front")

    lax.fori_loop(1, n_full // 2, chunk_pair, 0)

    @pl.when(n_full % 2 == 1)
    def _():
        chunk(pl.multiple_of(full_end - FLASH_KEYS, FLASH_KEYS), FLASH_KEYS, "front")

    for offset in range(0, FLASH_KEYS, math.gcd(FLASH_KEYS, tq)):
        @pl.when(i * tq - full_end == offset)
        def _(width=offset + tq):
            chunk(pl.multiple_of(full_end, LANE), width, "causal")

    for h in heads:
        o_ref[0, :, h * MLA_V:(h + 1) * MLA_V] = (acc_sc[h] / l_sc[h]).T.astype(o_ref.dtype)


def _flash(q, k, vt, n_end):
    b, lp, _ = q.shape
    assert SEQ_TILE % LANE == 0 and FLASH_KEYS % LANE == 0 and SEQ_TILE % CHUNK == 0 and (FIRST + N_META) % CHUNK == 0
    assert FIRST + N_META <= min(SEQ_TILE, FLASH_KEYS)
    return pl.pallas_call(
        functools.partial(_flash_kernel, n_end=n_end),
        grid=(b, lp // SEQ_TILE),
        in_specs=[pl.BlockSpec((1, SEQ_TILE, MLA_HEADS * MLA_QK_PAD), lambda bi, i: (bi, i, 0)),
                  pl.BlockSpec((1, lp, MLA_HEADS * MLA_QK_PAD), lambda bi, i: (bi, 0, 0)),
                  pl.BlockSpec((1, MLA_WIDTH, lp), lambda bi, i: (bi, 0, 0))],
        out_specs=pl.BlockSpec((1, SEQ_TILE, MLA_WIDTH), lambda bi, i: (bi, i, 0)),
        out_shape=jax.ShapeDtypeStruct((b, lp, MLA_WIDTH), BF16),
        scratch_shapes=[pltpu.VMEM((MLA_HEADS, 1, SEQ_TILE), F32), pltpu.VMEM((MLA_HEADS, 1, SEQ_TILE), F32),
                        pltpu.VMEM((MLA_HEADS, MLA_V, SEQ_TILE), F32)],
        compiler_params=_cparams(("parallel", "arbitrary")),
        name="mla_flash",
    )(q, k, vt)


def _rwkv_prep_kernel(*refs, has_vres):
    (p_ref, prev_ref, mu_ref, w0_ref, w2_ref, a0_ref, a2_ref, g2_ref, kk_ref, ka_ref, rk_ref,
     bd_ref) = refs[:12]
    n_in = 16 if has_vres else 12
    r_out, k_out, v_out, kk_out, bb_out, lw_out, bonus_out, g_out = refs[n_in:]
    c = RWKV_WIDTH
    p = p_ref[0].astype(F32)
    tm = p.shape[0]
    row = lax.broadcasted_iota(jnp.int32, (tm, 1), 0)
    pos = pl.program_id(1) * tm + row
    prev = jnp.where(row == 0, prev_ref[0, PREV_ROWS - 1:PREV_ROWS, :].astype(F32), pltpu.roll(p, 1, 0))
    prev = jnp.where(pos <= FIRST, 0.0, prev)
    used = pos >= FIRST
    ps = p + (prev - p) * mu_ref[...]
    r = ps[:, :c]
    k = ps[:, c:2 * c]
    v = ps[:, 2 * c:3 * c]
    xwa = ps[:, 3 * c:3 * c + W_LORA + A_LORA]
    xg = ps[:, 3 * c + W_LORA + A_LORA:]
    zw = w0_ref[...] + jnp.dot(jnp.tanh(xwa).astype(BF16), w2_ref[...], preferred_element_type=F32)
    nz = -zw
    softplus = jnp.maximum(nz, 0.0) + jnp.log(1.0 + jnp.exp(-jnp.abs(nz)))
    log_decay = -jnp.exp(-softplus - 0.5)
    if has_vres:
        vf_ref, v0_ref, v1_ref, v2_ref = refs[12:16]
        lo = jnp.dot(v.astype(BF16), v1_ref[...], preferred_element_type=F32)
        gate = _sigmoid(v0_ref[...] + jnp.dot(lo.astype(BF16), v2_ref[...], preferred_element_type=F32))
        v = v + (vf_ref[0].astype(F32) - v) * gate
    a = _sigmoid(a0_ref[...] + jnp.dot(xwa.astype(BF16), a2_ref[...], preferred_element_type=F32))
    g = jnp.dot(_sigmoid(xg).astype(BF16), g2_ref[...], preferred_element_type=F32)
    bd = bd_ref[...]
    kk = k * kk_ref[...]
    kk = kk * lax.rsqrt(jnp.maximum(_head_sum(kk * kk, bd), 1e-24))
    k = k * (1.0 + (a - 1.0) * ka_ref[...])
    bonus = _head_sum(r * k * rk_ref[...], bd) * v
    r_out[0] = r.astype(r_out.dtype)
    k_out[0] = jnp.where(used, k, 0.0).astype(k_out.dtype)
    v_out[0] = jnp.where(used, v, 0.0).astype(v_out.dtype)
    kk_out[0] = jnp.where(used, kk, 0.0).astype(kk_out.dtype)
    bb_out[0] = jnp.where(used, kk * a, 0.0).astype(bb_out.dtype)
    lw_out[0] = log_decay
    bonus_out[0] = bonus.astype(bonus_out.dtype)
    g_out[0] = g.astype(g_out.dtype)


def _rwkv_prep(pr, prm, v_first):
    b, lp, cols = pr.shape
    has_vres = v_first is not None
    c = RWKV_WIDTH
    row = lambda bi, i: (bi, i, 0)
    prev = lambda bi, i: (bi, jnp.maximum(i * (SEQ_TILE // PREV_ROWS) - 1, 0), 0)
    names = ["mu", "w0", "w2", "a0", "a2", "g2", "k_k", "k_a", "r_k", "bd"]
    args = [pr, pr] + [prm[n] for n in names]
    in_specs = [pl.BlockSpec((1, SEQ_TILE, cols), row), pl.BlockSpec((1, PREV_ROWS, cols), prev)]
    in_specs += [_const_spec(prm[n].shape) for n in names]
    if has_vres:
        args += [v_first, prm["v0"], prm["v1"], prm["v2"]]
        in_specs += [pl.BlockSpec((1, SEQ_TILE, c), row)]
        in_specs += [_const_spec(prm[n].shape) for n in ("v0", "v1", "v2")]
    out_spec = pl.BlockSpec((1, SEQ_TILE, c), row)
    return pl.pallas_call(
        functools.partial(_rwkv_prep_kernel, has_vres=has_vres),
        grid=(b, lp // SEQ_TILE),
        in_specs=in_specs,
        out_specs=[out_spec] * 8,
        out_shape=[jax.ShapeDtypeStruct((b, lp, c), F32 if n == 5 else BF16) for n in range(8)],
        compiler_params=_cparams(("parallel", "parallel")),
        name="rwkv_prep",
    )(*args)


_NN = (((1,), (0,)), ((), ()))
_NT = (((1,), (1,)), ((), ()))


def _mm(a, b, dims):
    return lax.dot_general(a.astype(BF16), b.astype(BF16), dims, preferred_element_type=F32)


def _rwkv_scan_kernel(r_ref, k_ref, v_ref, kk_ref, bb_ref, lw_ref, y_ref, s_sc):
    @pl.when(pl.program_id(1) == 0)
    def _():
        s_sc[...] = jnp.zeros(s_sc.shape, F32)

    c = SCAN_CHUNK
    n = 2 * c
    n_chunks = r_ref.shape[1] // c
    ri = lax.broadcasted_iota(jnp.int32, (c, c), 0)
    ci = lax.broadcasted_iota(jnp.int32, (c, c), 1)
    tri = jnp.where(ri >= ci, 1.0, 0.0).astype(BF16)
    row = lax.broadcasted_iota(jnp.int32, (2 * n, 2 * n), 0)
    col = lax.broadcasted_iota(jnp.int32, (2 * n, 2 * n), 1)
    t_idx = row & (c - 1)
    s_idx = col & (c - 1)
    causal = (t_idx > s_idx) | ((t_idx == s_idx) & (row >= n))
    eye = jnp.where(lax.broadcasted_iota(jnp.int32, (n, n), 0) == lax.broadcasted_iota(jnp.int32, (n, n), 1),
                    1.0, 0.0).astype(F32)
    head0 = lax.broadcasted_iota(jnp.int32, (1, LANE), 1) < RWKV_HEAD

    def stack(x):
        return jnp.concatenate([jnp.where(head0, x, 0.0), jnp.where(head0, 0.0, x)], axis=0)

    ar, bk, v_t, w_end = [], [], [], []
    for ch in range(n_chunks):
        rows = slice(ch * c, (ch + 1) * c)
        lw = lw_ref[0, rows, :]
        lw_hi = lw.astype(BF16)
        lw_lo = (lw - lw_hi.astype(F32)).astype(BF16)
        cum = (jnp.dot(tri, lw_hi, preferred_element_type=F32)
               + jnp.dot(tri, lw_lo, preferred_element_type=F32))
        w_incl = jnp.exp(cum)
        w_inv = jnp.exp(-cum)
        a_hat = -kk_ref[0, rows, :].astype(F32) * jnp.exp(cum - lw)
        r_hat = r_ref[0, rows, :].astype(F32) * w_incl
        b_hat = bb_ref[0, rows, :].astype(F32) * w_inv
        k_hat = k_ref[0, rows, :].astype(F32) * w_inv
        v = v_ref[0, rows, :].astype(F32)
        for pr in range(RWKV_PAIRS):
            sl = slice(pr * LANE, (pr + 1) * LANE)
            ar.append(jnp.concatenate([stack(a_hat[:, sl]), stack(r_hat[:, sl])], 0).astype(BF16))
            bk.append(jnp.concatenate([stack(b_hat[:, sl]), stack(k_hat[:, sl])], 0).astype(BF16))
            v_t.append(stack(v[:, sl]).T.astype(BF16))
            w_end.append(w_incl[c - 1:c, sl])
    every = range(len(ar))
    gram = [jnp.where(causal, _mm(ar[g], bk[g], _NT), 0.0) for g in every]
    pw = [gram[g][:n, :n] for g in every]
    t_inv = [eye + pw[g] for g in every]
    pw = [_mm(pw[g], pw[g], _NN) for g in every]
    for _ in range(4):
        both = [_mm(pw[g], jnp.concatenate([pw[g], t_inv[g]], 1), _NN) for g in every]
        pw = [both[g][:, :n] for g in every]
        t_inv = [t_inv[g] + both[g][:, n:] for g in every]
    t_inv = [(t_inv[g] + _mm(pw[g], t_inv[g], _NN)).astype(BF16) for g in every]
    va = [_mm(v_t[g], gram[g][:n, n:], _NT) for g in every]
    q_bk = [gram[g][n:, :].astype(BF16) for g in every]

    state = [s_sc[pr] for pr in range(RWKV_PAIRS)]
    for ch in range(n_chunks):
        gs = [ch * RWKV_PAIRS + pr for pr in range(RWKV_PAIRS)]
        s_ar = [_mm(state[pr], ar[g], _NT) for pr, g in enumerate(gs)]
        u_t = [_mm(s_ar[pr][:, :n] + va[g], t_inv[g], _NT) for pr, g in enumerate(gs)]
        uv = [jnp.concatenate([u_t[pr].astype(BF16), v_t[g]], 1) for pr, g in enumerate(gs)]
        state = [(state[pr] + _mm(uv[pr], bk[g], _NN)) * w_end[g] for pr, g in enumerate(gs)]
        for pr, g in enumerate(gs):
            y = (s_ar[pr][:, n:] + _mm(uv[pr], q_bk[g], _NT)).T
            y_ref[0, ch * c:(ch + 1) * c, pr * LANE:(pr + 1) * LANE] = y[:c] + y[c:]
    for pr in range(RWKV_PAIRS):
        s_sc[pr] = state[pr]


def _rwkv_scan(r, k, v, kk, bb, lw):
    b, lp, c = r.shape
    rows = SCAN_STEP_CHUNKS * SCAN_CHUNK
    spec = pl.BlockSpec((1, rows, c), lambda bi, i: (bi, i, 0))
    return pl.pallas_call(
        _rwkv_scan_kernel,
        grid=(b, lp // rows),
        in_specs=[spec] * 6,
        out_specs=spec,
        out_shape=jax.ShapeDtypeStruct((b, lp, c), F32),
        scratch_shapes=[pltpu.VMEM((RWKV_PAIRS, LANE, LANE), F32)],
        compiler_params=_cparams(("parallel", "arbitrary")),
        name="rwkv_scan",
    )(r, k, v, kk, bb, lw)


def _outproj_kernel(ys_ref, bonus_ref, g_ref, ymla_ref, lnw_ref, lnb_ref, bd_ref, wo_ref, h_ref, o_ref):
    y = ys_ref[...]
    bd = bd_ref[...]
    inv_n = 1.0 / RWKV_HEAD
    d = y - _head_sum(y, bd) * inv_n
    var = _head_sum(d * d, bd) * inv_n
    yn = d * lax.rsqrt(var + GN_EPS) * lnw_ref[...] + lnb_ref[...]
    yr = ((yn + bonus_ref[...].astype(F32)) * g_ref[...].astype(F32)).astype(BF16)
    o_ref[...] = (h_ref[...]
                  + jnp.dot(ymla_ref[...], wo_ref[:MLA_WIDTH, :], preferred_element_type=F32)
                  + jnp.dot(yr, wo_ref[MLA_WIDTH:, :], preferred_element_type=F32))


def _outproj(ys, bonus, g, ymla, ln_w, ln_b, bd, wo, h):
    tp = h.shape[0]
    c = RWKV_WIDTH
    rc = pl.BlockSpec((ROW_TILE, c), lambda i: (i, 0))
    rd = pl.BlockSpec((ROW_TILE, D_MODEL), lambda i: (i, 0))
    return pl.pallas_call(
        _outproj_kernel,
        grid=(tp // ROW_TILE,),
        in_specs=[rc, rc, rc, rc, _const_spec((1, c)), _const_spec((1, c)), _const_spec(bd.shape),
                  _const_spec(wo.shape), rd],
        out_specs=rd,
        out_shape=jax.ShapeDtypeStruct((tp, D_MODEL), F32),
        compiler_params=_cparams(("parallel",)),
        name="outproj",
    )(ys, bonus, g, ymla, ln_w, ln_b, bd, wo, h)


def _ffn_chunks(u, wg_ref, wu_ref, wd_ref, idx, ff, chunk):
    acc = None
    for c0 in range(0, ff, chunk):
        sl = slice(c0, c0 + chunk)
        gate = jnp.dot(u, wg_ref[idx + (slice(None), sl)], preferred_element_type=F32)
        up = jnp.dot(u, wu_ref[idx + (slice(None), sl)], preferred_element_type=F32)
        act = (gate * _sigmoid(gate) * up).astype(BF16)
        part = jnp.dot(act, wd_ref[idx + (sl, slice(None))], preferred_element_type=F32)
        acc = part if acc is None else acc + part
    return acc


def _dense_ffn_kernel(h_ref, g_ref, wg_ref, wu_ref, wd_ref, o_ref, *, ff, chunk):
    h = h_ref[...]
    u = _rms(h, g_ref[...]).astype(BF16)
    o_ref[...] = h + _ffn_chunks(u, wg_ref, wu_ref, wd_ref, (), ff, chunk)


def _dense_ffn(h, g, wg, wu, wd):
    tp = h.shape[0]
    ff = wg.shape[1]
    rd = pl.BlockSpec((ROW_TILE, D_MODEL), lambda i: (i, 0))
    once = lambda shape: pl.BlockSpec(shape, lambda i: (0, 0), pipeline_mode=pl.Buffered(1))
    return pl.pallas_call(
        functools.partial(_dense_ffn_kernel, ff=ff, chunk=256),
        grid=(tp // ROW_TILE,),
        in_specs=[rd, _const_spec((1, D_MODEL)), once(wg.shape), once(wu.shape), once(wd.shape)],
        out_specs=rd,
        out_shape=jax.ShapeDtypeStruct((tp, D_MODEL), F32),
        compiler_params=_cparams(("parallel",)),
        name="dense_ffn",
    )(h, g, wg, wu, wd)


def _router_kernel(h_ref, g_ref, wr_ref, br_ref, u_ref, meta_ref, meta_t_ref, cnt_ref, carry_sc, *, n_end):
    @pl.when((pl.program_id(0) == 0) & (pl.program_id(1) == 0))
    def _():
        carry_sc[...] = jnp.zeros(carry_sc.shape, F32)

    tm = h_ref.shape[1]
    u = _rms(h_ref[0], g_ref[...])
    u_ref[0] = u
    u_hi, u_lo = _split(u)
    w_hi, w_lo = _split(wr_ref[...])
    logits = (jnp.dot(u_hi, w_hi, preferred_element_type=F32) + jnp.dot(u_hi, w_lo, preferred_element_type=F32)
              + jnp.dot(u_lo, w_hi, preferred_element_type=F32) + br_ref[...])
    lane = lax.broadcasted_iota(jnp.int32, (tm, LANE), 1).astype(F32)
    logits = jnp.where(lane < N_EXPERTS, logits, -jnp.inf)
    top0 = jnp.max(logits, -1, keepdims=True)
    e0 = jnp.min(jnp.where(logits == top0, lane, float(LANE)), -1, keepdims=True)
    rest = jnp.where(lane == e0, -jnp.inf, logits)
    top1 = jnp.max(rest, -1, keepdims=True)
    e1 = jnp.min(jnp.where(rest == top1, lane, float(LANE)), -1, keepdims=True)
    ex = jnp.exp(top1 - top0)
    g0 = 1.0 / (1.0 + ex)
    g1 = ex / (1.0 + ex)
    pos = pl.program_id(1) * tm + lax.broadcasted_iota(jnp.int32, (tm, 1), 0)
    valid = (pos >= FIRST) & (pos < n_end)
    oh0 = jnp.where((lane == e0) & valid, 1.0, 0.0)
    oh1 = jnp.where((lane == e1) & valid, 1.0, 0.0)
    ri = lax.broadcasted_iota(jnp.int32, (tm, tm), 0)
    ci = lax.broadcasted_iota(jnp.int32, (tm, tm), 1)
    before = jnp.where(ri > ci, 1.0, 0.0).astype(BF16)
    both = oh0 + oh1
    seen = carry_sc[...] + jnp.dot(before, both.astype(BF16), preferred_element_type=F32)
    rank0 = jnp.sum(jnp.where(lane == e0, seen, 0.0), -1, keepdims=True)
    rank1 = jnp.sum(jnp.where(lane == e1, seen, 0.0), -1, keepdims=True)
    carry_sc[...] = carry_sc[...] + jnp.sum(both, 0, keepdims=True)
    cnt_ref[...] = jnp.broadcast_to(carry_sc[...], cnt_ref.shape)
    vf = jnp.where(valid, 1.0, 0.0)
    meta = jnp.where(lane == 0, e0, 0.0)
    meta = jnp.where(lane == 1, e1, meta)
    meta = jnp.where(lane == 2, rank0, meta)
    meta = jnp.where(lane == 3, rank1, meta)
    meta = jnp.where(lane == 4, g0 * vf, meta)
    meta = jnp.where(lane == 5, g1 * vf, meta)
    meta = jnp.where(lane == 6, vf, meta)
    meta_ref[0] = meta
    meta_t_ref[...] = meta.T[:8]


def _router(h, g, wr, br, n_end):
    b, lp, _ = h.shape
    row = lambda bi, i: (bi, i, 0)
    return pl.pallas_call(
        functools.partial(_router_kernel, n_end=n_end),
        grid=(b, lp // SEQ_TILE),
        in_specs=[pl.BlockSpec((1, SEQ_TILE, D_MODEL), row), _const_spec((1, D_MODEL)),
                  _const_spec(wr.shape), _const_spec(br.shape)],
        out_specs=[pl.BlockSpec((1, SEQ_TILE, D_MODEL), row), pl.BlockSpec((1, SEQ_TILE, LANE), row),
                   pl.BlockSpec((8, SEQ_TILE), lambda bi, i: (0, bi * (lp // SEQ_TILE) + i)),
                   _const_spec((8, LANE))],
        out_shape=[jax.ShapeDtypeStruct((b, lp, D_MODEL), F32), jax.ShapeDtypeStruct((b, lp, LANE), F32),
                   jax.ShapeDtypeStruct((8, b * lp), F32), jax.ShapeDtypeStruct((8, LANE), F32)],
        scratch_shapes=[pltpu.VMEM((1, LANE), F32)],
        compiler_params=_cparams(("arbitrary", "arbitrary")),
        name="moe_router",
    )(h, g, wr, br)


def _dispatch_kernel(s0_ref, s1_ref, u_ref, xb_in_ref, xb_ref, sem):
    del xb_in_ref
    tm = u_ref.shape[0]

    def copies(rw):
        src = u_ref.at[pl.ds(rw, 1), :]
        return [pltpu.make_async_copy(src, xb_ref.at[pl.ds(s_ref[0, 0, rw], 1), :], sem)
                for s_ref in (s0_ref, s1_ref)]

    def start(j, carry):
        for kq in range(DMA_UNROLL):
            for thread, cp in enumerate(copies(j * DMA_UNROLL + kq)):
                cp.start(priority=thread)
        return carry

    def wait(rw, carry):
        for cp in copies(rw):
            cp.wait()
        return carry

    lax.fori_loop(0, tm // DMA_UNROLL, start, 0)
    lax.fori_loop(0, tm, wait, 0)


def _dispatch(slot0, slot1, u, n_rows, tm):
    tp = u.shape[0]
    sspec = pl.BlockSpec((1, 1, tm), lambda i: (i, 0, 0), memory_space=pltpu.SMEM)
    xb0 = jnp.zeros((n_rows, D_MODEL), F32)
    return pl.pallas_call(
        _dispatch_kernel,
        grid=(tp // tm,),
        in_specs=[sspec, sspec, pl.BlockSpec((tm, D_MODEL), lambda i: (i, 0)),
                  pl.BlockSpec(memory_space=pl.ANY)],
        out_specs=pl.BlockSpec(memory_space=pl.ANY),
        out_shape=jax.ShapeDtypeStruct((n_rows, D_MODEL), F32),
        scratch_shapes=[pltpu.SemaphoreType.DMA(())],
        input_output_aliases={3: 0},
        compiler_params=_cparams(("arbitrary",)),
        name="moe_dispatch",
    )(slot0.reshape(tp // tm, 1, tm), slot1.reshape(tp // tm, 1, tm), u, xb0)


def _expert_kernel(be_ref, used_ref, x_ref, wg_ref, wu_ref, wd_ref, o_ref, *, ff, chunk):
    del be_ref
    in_use = pl.program_id(0) < used_ref[0]

    @pl.when(in_use)
    def _():
        o_ref[...] = _ffn_chunks(x_ref[...].astype(BF16), wg_ref, wu_ref, wd_ref, (0,), ff, chunk)

    @pl.when(jnp.logical_not(in_use))
    def _():
        o_ref[...] = jnp.zeros(o_ref.shape, F32)


def _experts(block_e, n_used, xb, wg, wu, wd):
    n_slots = block_e.shape[0] * MOE_BLOCK
    ff = wg.shape[2]
    rows = pl.BlockSpec((MOE_BLOCK, D_MODEL), lambda i, be, used: (i, 0))
    return pl.pallas_call(
        functools.partial(_expert_kernel, ff=ff, chunk=512),
        grid_spec=pltpu.PrefetchScalarGridSpec(
            num_scalar_prefetch=2,
            grid=(n_slots // MOE_BLOCK,),
            in_specs=[rows,
                      pl.BlockSpec((1, D_MODEL, ff), lambda i, be, used: (be[i], 0, 0)),
                      pl.BlockSpec((1, D_MODEL, ff), lambda i, be, used: (be[i], 0, 0)),
                      pl.BlockSpec((1, ff, D_MODEL), lambda i, be, used: (be[i], 0, 0))],
            out_specs=rows),
        out_shape=jax.ShapeDtypeStruct((n_slots, D_MODEL), F32),
        compiler_params=_cparams(("arbitrary",), 60 * 1024 * 1024),
        name="moe_experts",
    )(block_e, n_used, xb, wg, wu, wd)


def _combine_kernel(s0_ref, s1_ref, n0_ref, n1_ref, h_ref, gates_ref, fn_ref, yb_ref, o_ref, y0_sc, y1_sc, sems, *,
                    per_row, n_frame_tiles):
    t = pl.program_id(0)
    tm = h_ref.shape[0]
    cur = t % 2

    def copies(rw, a_ref, b_ref, buf):
        return [pltpu.make_async_copy(yb_ref.at[pl.ds(s_ref[0, 0, rw], 1), :], y_sc.at[buf, pl.ds(rw, 1), :],
                                      sems.at[buf])
                for s_ref, y_sc in ((a_ref, y0_sc), (b_ref, y1_sc))]

    def issue(a_ref, b_ref, buf):
        def body(j, carry):
            for kq in range(DMA_UNROLL):
                for thread, cp in enumerate(copies(j * DMA_UNROLL + kq, a_ref, b_ref, buf)):
                    cp.start(priority=thread)
            return carry
        lax.fori_loop(0, tm // DMA_UNROLL, body, 0)

    @pl.when(t == 0)
    def _():
        issue(s0_ref, s1_ref, 0)

    @pl.when(t + 1 < pl.num_programs(0))
    def _():
        issue(n0_ref, n1_ref, 1 - cur)

    def wait(rw, carry):
        for cp in copies(rw, s0_ref, s1_ref, cur):
            cp.wait()
        return carry

    lax.fori_loop(0, tm, wait, 0)
    tile = t % per_row

    @pl.when((tile >= 1) & (tile <= n_frame_tiles))
    def _():
        gates = gates_ref[...]
        f = y0_sc[cur] * gates[:, 4:5] + y1_sc[cur] * gates[:, 5:6]
        o_ref[0] = _rms(h_ref[...] + f, fn_ref[...])


def _combine(slot0, slot1, h, meta, fn, yb, tm, b, seq):
    tp = h.shape[0]
    n_tiles = tp // tm
    per_row = n_tiles // b
    assert (FIRST + N_META) == tm and seq % tm == 0
    sspec = pl.BlockSpec((1, 1, tm), lambda i: (i, 0, 0), memory_space=pltpu.SMEM)
    nspec = pl.BlockSpec((1, 1, tm), lambda i: (jnp.minimum(i + 1, n_tiles - 1), 0, 0), memory_space=pltpu.SMEM)
    rd = pl.BlockSpec((tm, D_MODEL), lambda i: (i, 0))
    s0 = slot0.reshape(n_tiles, 1, tm)
    s1 = slot1.reshape(n_tiles, 1, tm)
    return pl.pallas_call(
        functools.partial(_combine_kernel, per_row=per_row, n_frame_tiles=seq // tm),
        grid=(n_tiles,),
        in_specs=[sspec, sspec, nspec, nspec, rd, pl.BlockSpec((tm, LANE), lambda i: (i, 0)),
                  _const_spec((1, D_MODEL)), pl.BlockSpec(memory_space=pl.ANY)],
        out_specs=pl.BlockSpec((1, tm, D_MODEL), lambda i: (i // per_row, jnp.clip(i % per_row - 1, 0, seq // tm - 1), 0)),
        out_shape=jax.ShapeDtypeStruct((b, seq, D_MODEL), F32),
        scratch_shapes=[pltpu.VMEM((2, tm, D_MODEL), F32), pltpu.VMEM((2, tm, D_MODEL), F32),
                        pltpu.SemaphoreType.DMA((2,))],
        compiler_params=_cparams(("arbitrary",)),
        name="moe_combine",
    )(s0, s1, s0, s1, h, meta, fn, yb)


def _final_norm_kernel(h_ref, g_ref, o_ref):
    o_ref[...] = _rms(h_ref[...], g_ref[...])


def _final_norm(h, g):
    tp = h.shape[0]
    rd = pl.BlockSpec((ROW_TILE, D_MODEL), lambda i: (i, 0))
    return pl.pallas_call(
        _final_norm_kernel,
        grid=(tp // ROW_TILE,),
        in_specs=[rd, _const_spec((1, D_MODEL))],
        out_specs=rd,
        out_shape=jax.ShapeDtypeStruct((tp, D_MODEL), F32),
        compiler_params=_cparams(("parallel",)),
        name="final_norm",
    )(h, g)


def _rot_cols(w):
    half = MLA_ROPE // 2
    return jnp.concatenate([-w[..., half:], w[..., :half]], -1)


def _prep_inproj(w_in):
    m_q = w_in[:, :MLA_Q_LORA + MLA_KV_LORA]
    k_rope = w_in[:, MLA_Q_LORA + MLA_KV_LORA:MLA_Q_LORA + MLA_KV_LORA + MLA_ROPE]
    w_m = jnp.concatenate([m_q, k_rope, _rot_cols(k_rope)], -1)
    w_r = w_in[:, MLA_Q_LORA + MLA_KV_LORA + MLA_ROPE:]
    return w_r.astype(BF16), w_m.astype(BF16)


def _prep_wq(w_uq):
    w = w_uq.reshape(MLA_Q_LORA, MLA_HEADS, MLA_NOPE + MLA_ROPE)
    rope = w[..., MLA_NOPE:]
    return jnp.concatenate([w, _rot_cols(rope)], -1).reshape(MLA_Q_LORA, MLA_HEADS * MLA_QK_PAD).astype(BF16)


def _prep_wkv(w_ukv):
    w = w_ukv.reshape(MLA_KV_LORA, MLA_HEADS, MLA_NOPE + MLA_V)
    k_nope = w[..., :MLA_NOPE].reshape(MLA_KV_LORA, MLA_HEADS * MLA_NOPE)
    v = w[..., MLA_NOPE:].reshape(MLA_KV_LORA, MLA_HEADS * MLA_V)
    return jnp.concatenate([k_nope, v], -1).astype(BF16)


def _row(x):
    return x.reshape(1, -1).astype(F32)


def _block_diag_ones():
    idx = np.arange(RWKV_WIDTH) // RWKV_HEAD
    return jnp.asarray(idx[:, None] == idx[None, :], BF16)


def kernel(x, meta_tokens, attn_norm, ffn_norm, final_norm, w_in, w_out, mla_q_norm, mla_kv_norm, mla_w_uq, mla_w_ukv, rwkv_mu, rwkv_w0, rwkv_w2, rwkv_a0, rwkv_a2, rwkv_g2, rwkv_k_k, rwkv_k_a, rwkv_r_k, rwkv_ln_w, rwkv_ln_b, rwkv_v0, rwkv_v1, rwkv_v2, ffn_w_gate, ffn_w_up, ffn_w_down, moe_router, moe_router_bias, moe_w_gate, moe_w_up, moe_w_down):
    b, seq, d = x.shape
    depth = attn_norm.shape[0]
    n_end = FIRST + N_META + seq
    lp = -(-n_end // SEQ_TILE) * SEQ_TILE
    tp = b * lp
    assert d == D_MODEL and tp % ROW_TILE == 0 and lp % SCAN_CHUNK == 0

    meta = jnp.broadcast_to(meta_tokens[None].astype(x.dtype), (b, N_META, d))
    h = jnp.concatenate([jnp.zeros((b, FIRST, d), x.dtype), meta, x, jnp.zeros((b, lp - n_end, d), x.dtype)],
                        axis=1).reshape(tp, d)

    tabs = _rope_tables(lp)
    bd = _block_diag_ones()
    zeros_wa = jnp.zeros((W_LORA, RWKV_WIDTH), F32)
    v_first = None
    for i in range(depth):
        w_r, w_m = _prep_inproj(w_in[i])
        pr, pm = _norm_inproj(h, _row(attn_norm[i]), w_r, w_m)

        q, k, v = _mla_up(pm.reshape(b, lp, -1), _row(mla_q_norm[i]), _row(mla_kv_norm[i]),
                          _prep_wq(mla_w_uq[i]), _prep_wkv(mla_w_ukv[i]), tabs)
        y_mla = _flash(q, k, v, n_end)

        prm = {
            "mu": _row(rwkv_mu[i]), "w0": _row(rwkv_w0[i]), "a0": _row(rwkv_a0[i]),
            "w2": jnp.concatenate([rwkv_w2[i], zeros_wa], 0).astype(BF16),
            "a2": jnp.concatenate([zeros_wa, rwkv_a2[i]], 0).astype(BF16),
            "g2": rwkv_g2[i].astype(BF16),
            "k_k": _row(rwkv_k_k[i]), "k_a": _row(rwkv_k_a[i]), "r_k": _row(rwkv_r_k[i]), "bd": bd,
        }
        if i > 0:
            prm["v0"] = _row(rwkv_v0[i - 1])
            prm["v1"] = jnp.pad(rwkv_v1[i - 1], ((0, 0), (0, LANE - V_LORA))).astype(BF16)
            prm["v2"] = jnp.pad(rwkv_v2[i - 1], ((0, LANE - V_LORA), (0, 0))).astype(BF16)
        r_, k_, v_, kk_, bb_, lw_, bonus, gate = _rwkv_prep(pr.reshape(b, lp, -1), prm, v_first)
        if i == 0:
            v_first = v_
        ys = _rwkv_scan(r_, k_, v_, kk_, bb_, lw_)

        flat = lambda t: t.reshape(tp, -1)
        h = _outproj(flat(ys), flat(bonus), flat(gate), flat(y_mla), _row(rwkv_ln_w[i]), _row(rwkv_ln_b[i]),
                     bd, w_out[i].astype(BF16), h)

        j = i // 2
        last = i == depth - 1
        if i % 2 == 0:
            h = _dense_ffn(h, _row(ffn_norm[i]), ffn_w_gate[j].astype(BF16), ffn_w_up[j].astype(BF16),
                           ffn_w_down[j].astype(BF16))
            if last:
                h = _final_norm(h, _row(final_norm)).reshape(b, lp, d)[:, FIRST + N_META:n_end]
        else:
            if not last:
                raise NotImplementedError("an MoE layer that is not the last layer")
            h = _moe(h, b, lp, n_end, _row(ffn_norm[i]), moe_router[j], moe_router_bias[j],
                     moe_w_gate[j], moe_w_up[j], moe_w_down[j], _row(final_norm))
    return h


def _moe(h, b, lp, n_end, g, router, router_bias, w_gate, w_up, w_down, final_g):
    tp = b * lp
    wr = jnp.pad(router, ((0, 0), (0, LANE - N_EXPERTS))).astype(F32)
    br = jnp.pad(router_bias, (0, LANE - N_EXPERTS)).reshape(1, LANE).astype(F32)
    u, meta, meta_t, cnt = _router(h.reshape(b, lp, D_MODEL), g, wr, br, n_end)
    meta = meta.reshape(tp, LANE)

    counts = cnt[0, :N_EXPERTS].astype(jnp.int32)
    padded = (counts + MOE_BLOCK - 1) // MOE_BLOCK * MOE_BLOCK
    pend = jnp.cumsum(padded)
    pstart = pend - padded
    n_assign = b * (n_end - FIRST) * TOP_K
    n_blocks = (n_assign + N_EXPERTS * (MOE_BLOCK - 1)) // MOE_BLOCK + 1
    n_slots = n_blocks * MOE_BLOCK
    block_start = jnp.arange(n_blocks, dtype=jnp.int32) * MOE_BLOCK
    block_e = jnp.minimum(jnp.sum((pend[None, :] <= block_start[:, None]).astype(jnp.int32), -1),
                          N_EXPERTS - 1)
    e0, e1, rank0, rank1 = (meta_t[n].astype(jnp.int32) for n in range(4))
    valid = meta_t[6] > 0.5
    slot0 = pstart[e0] + rank0
    slot1 = pstart[e1] + rank1
    spare = n_slots + jnp.arange(tp, dtype=jnp.int32) % LANE
    d0 = jnp.where(valid, slot0, spare)
    d1 = jnp.where(valid, slot1, spare + LANE)
    c0 = jnp.where(valid, slot0, 0)
    c1 = jnp.where(valid, slot1, 0)

    xb = _dispatch(d0, d1, u.reshape(tp, D_MODEL), n_slots + 2 * LANE, LANE)
    n_used = (pend[N_EXPERTS - 1:] // MOE_BLOCK).astype(jnp.int32)
    yb = _experts(block_e, n_used, xb, w_gate.astype(BF16), w_up.astype(BF16), w_down.astype(BF16))
    return _combine(c0, c1, h, meta, final_g, yb, LANE, b, n_end - FIRST - N_META)
```

```python
import functools
import math

import numpy as np
import jax
import jax.numpy as jnp
from jax import lax
from jax.experimental import pallas as pl
from jax.experimental.pallas import tpu as pltpu

F32 = jnp.float32
BF16 = jnp.bfloat16

D_MODEL = 1024
CHUNK = 64
N_META = 16
RMS_EPS = 1e-6

MLA_HEADS = 4
MLA_NOPE = 128
MLA_ROPE = 64
MLA_V = 128
MLA_Q_LORA = 512
MLA_KV_LORA = 256
ROPE_THETA = 10000.0
MLA_WIDTH = MLA_HEADS * MLA_V
MLA_QK_PAD = 256
MLA_COLS_PAD = MLA_Q_LORA + MLA_KV_LORA + 2 * MLA_ROPE

RWKV_WIDTH = 512
RWKV_HEAD = 64
RWKV_HEADS = RWKV_WIDTH // RWKV_HEAD
RWKV_PAIRS = RWKV_HEADS // 2
W_LORA = 64
A_LORA = 64
V_LORA = 32
G_LORA = 128
GN_EPS = 64e-5
RWKV_COLS = 3 * RWKV_WIDTH + W_LORA + A_LORA + G_LORA
SCAN_CHUNK = 64
SCAN_STEP_CHUNKS = 6

N_EXPERTS = 8
TOP_K = 2
MOE_BLOCK = 256

LANE = 128
FRONT_PAD = -N_META % LANE
FIRST = FRONT_PAD
SEQ_TILE = 384
FLASH_KEYS = 256
ROW_TILE = 512
PREV_ROWS = 16
ROW_TILES = D_MODEL // LANE
DMA_UNROLL = 8
VMEM_LIMIT = 56 * 1024 * 1024


def _cparams(sem, vmem=VMEM_LIMIT):
    return pltpu.CompilerParams(dimension_semantics=sem, vmem_limit_bytes=vmem)


def _rms(x, g):
    return x * lax.rsqrt(jnp.mean(x * x, -1, keepdims=True) + RMS_EPS) * g


def _sigmoid(x):
    return 1.0 / (1.0 + jnp.exp(-x))


def _split(x):
    hi = x.astype(BF16)
    return hi, (x - hi.astype(F32)).astype(BF16)


def _head_sum(x, bd):
    return jnp.dot(x.astype(BF16), bd, preferred_element_type=F32)


def _to_row_tiles(ref, x):
    for s in range(ROW_TILES):
        ref[pl.ds(s, x.shape[0], stride=ROW_TILES), :] = x[:, s * LANE:(s + 1) * LANE]


def _from_row_tiles(ref, rows):
    return jnp.concatenate([ref[pl.ds(s, rows, stride=ROW_TILES), :] for s in range(ROW_TILES)], axis=1)


def _row_tile(ref, r):
    return ref.at[pl.ds(pl.multiple_of(r * ROW_TILES, ROW_TILES), ROW_TILES), :]


def _const_spec(shape):
    nd = len(shape)
    return pl.BlockSpec(shape, lambda *_: (0,) * nd)


def _norm_inproj_kernel(h_ref, g_ref, wr_ref, wm_ref, pr_ref, pm_ref):
    u = _rms(h_ref[...], g_ref[...]).astype(BF16)
    pr_ref[...] = jnp.dot(u, wr_ref[...], preferred_element_type=F32).astype(pr_ref.dtype)
    pm_ref[...] = jnp.dot(u, wm_ref[...], preferred_element_type=F32).astype(pm_ref.dtype)


def _norm_inproj(h, g, w_r, w_m):
    tp = h.shape[0]
    return pl.pallas_call(
        _norm_inproj_kernel,
        grid=(tp // ROW_TILE,),
        in_specs=[pl.BlockSpec((ROW_TILE, D_MODEL), lambda i: (i, 0)),
                  _const_spec((1, D_MODEL)),
                  _const_spec(w_r.shape), _const_spec(w_m.shape)],
        out_specs=[pl.BlockSpec((ROW_TILE, w_r.shape[1]), lambda i: (i, 0)),
                   pl.BlockSpec((ROW_TILE, w_m.shape[1]), lambda i: (i, 0))],
        out_shape=[jax.ShapeDtypeStruct((tp, w_r.shape[1]), BF16),
                   jax.ShapeDtypeStruct((tp, w_m.shape[1]), BF16)],
        compiler_params=_cparams(("parallel",)),
        name="norm_inproj",
    )(h, g, w_r, w_m)


def _mla_up_kernel(pm_ref, qn_ref, kvn_ref, wq_ref, wkv_ref, cq_ref, sq_ref, ck_ref, sk_ref,
                   q_ref, k_ref, vt_ref):
    pm = pm_ref[0].astype(F32)
    c_q = pm[:, :MLA_Q_LORA]
    c_kv = pm[:, MLA_Q_LORA:MLA_Q_LORA + MLA_KV_LORA]
    k_r = pm[:, MLA_Q_LORA + MLA_KV_LORA:]
    q = jnp.dot(_rms(c_q, qn_ref[...]).astype(BF16), wq_ref[...], preferred_element_type=F32)
    kv = jnp.dot(_rms(c_kv, kvn_ref[...]).astype(BF16), wkv_ref[...], preferred_element_type=F32)
    k_rope = (k_r * ck_ref[...] + pltpu.roll(k_r, MLA_ROPE, 1) * sk_ref[...]).astype(BF16)
    cq = cq_ref[...]
    sq = sq_ref[...]
    for hd in range(MLA_HEADS):
        qh = q[:, hd * MLA_QK_PAD:(hd + 1) * MLA_QK_PAD]
        qh = qh * cq + pltpu.roll(qh, MLA_QK_PAD - MLA_ROPE, 1) * sq
        q_ref[0, :, hd * MLA_QK_PAD:(hd + 1) * MLA_QK_PAD] = qh.astype(BF16)
        k_ref[0, :, hd * MLA_QK_PAD:hd * MLA_QK_PAD + MLA_NOPE] = (
            kv[:, hd * MLA_NOPE:(hd + 1) * MLA_NOPE].astype(BF16))
        k_ref[0, :, hd * MLA_QK_PAD + MLA_NOPE:(hd + 1) * MLA_QK_PAD] = k_rope
    vt_ref[0] = kv[:, MLA_HEADS * MLA_NOPE:].T.astype(BF16)


def _mla_up(pm, q_norm, kv_norm, wq, wkv, tabs):
    b, lp, _ = pm.shape
    cq, sq, ck, sk = tabs
    qk_w = MLA_HEADS * MLA_QK_PAD
    row = lambda bi, i: (bi, i, 0)
    tab = lambda bi, i: (i, 0)
    return pl.pallas_call(
        _mla_up_kernel,
        grid=(b, lp // SEQ_TILE),
        in_specs=[pl.BlockSpec((1, SEQ_TILE, MLA_COLS_PAD), row),
                  _const_spec((1, MLA_Q_LORA)), _const_spec((1, MLA_KV_LORA)),
                  _const_spec(wq.shape), _const_spec(wkv.shape),
                  pl.BlockSpec((SEQ_TILE, MLA_QK_PAD), tab), pl.BlockSpec((SEQ_TILE, MLA_QK_PAD), tab),
                  pl.BlockSpec((SEQ_TILE, LANE), tab), pl.BlockSpec((SEQ_TILE, LANE), tab)],
        out_specs=[pl.BlockSpec((1, SEQ_TILE, qk_w), row),
                   pl.BlockSpec((1, SEQ_TILE, qk_w), row),
                   pl.BlockSpec((1, MLA_WIDTH, SEQ_TILE), lambda bi, i: (bi, 0, i))],
        out_shape=[jax.ShapeDtypeStruct((b, lp, qk_w), BF16),
                   jax.ShapeDtypeStruct((b, lp, qk_w), BF16),
                   jax.ShapeDtypeStruct((b, MLA_WIDTH, lp), BF16)],
        compiler_params=_cparams(("parallel", "parallel")),
        name="mla_up",
    )(pm, q_norm, kv_norm, wq, wkv, cq, sq, ck, sk)


def _rope_tables(lp):
    half = MLA_ROPE // 2
    inv = (np.float32(ROPE_THETA) ** (-np.arange(half, dtype=np.float32) / np.float32(half))).astype(np.float32)
    pos = np.maximum(np.arange(lp) - FIRST, 0).astype(np.float32)
    ang = (pos[:, None] * inv[None, :]).astype(np.float64)
    cos = np.concatenate([np.cos(ang), np.cos(ang)], -1)
    sin = np.concatenate([np.sin(ang), np.sin(ang)], -1)
    scale = (MLA_NOPE + MLA_ROPE) ** -0.5 * np.log2(np.e)
    zeros = np.zeros((lp, MLA_ROPE))
    cq = np.concatenate([np.full((lp, MLA_NOPE), scale), cos * scale, zeros], -1)
    sq = np.concatenate([np.zeros((lp, MLA_NOPE)), sin * scale, zeros], -1)
    ck = np.concatenate([cos, zeros], -1)
    sk = np.concatenate([sin, zeros], -1)
    return tuple(jnp.asarray(t, F32) for t in (cq, sq, ck, sk))


def _chunk_id(row):
    frame = row - (FIRST + N_META)
    return jnp.where(frame < 0, 0, 1 + (frame >> 6))


def _flash_kernel(q_ref, k_ref, vt_ref, o_ref, m_sc, l_sc, acc_sc, *, n_end):
    i = pl.program_id(1)
    tq = q_ref.shape[1]
    lp = k_ref.shape[1]
    heads = range(MLA_HEADS)
    q = [q_ref[0, :, h * MLA_QK_PAD:(h + 1) * MLA_QK_PAD] for h in heads]
    m_sc[...] = jnp.full(m_sc.shape, -jnp.inf, F32)
    l_sc[...] = jnp.zeros(l_sc.shape, F32)
    acc_sc[...] = jnp.zeros(acc_sc.shape, F32)
    n_full = (i * tq) // FLASH_KEYS
    full_end = n_full * FLASH_KEYS
    q_cid = _chunk_id(i * tq + lax.broadcasted_iota(jnp.int32, (1, tq), 1))

    def chunk(start, width, mask):
        keys = pl.ds(start, width)
        s = [lax.dot_general(k_ref[0, keys, h * MLA_QK_PAD:(h + 1) * MLA_QK_PAD], q[h], _NT,
                             preferred_element_type=F32) for h in heads]
        if mask is not None:
            k_pos = start + lax.broadcasted_iota(jnp.int32, (width, 1), 0)
            visible = k_pos >= FIRST
            if mask == "causal":
                visible = visible & (k_pos >= full_end) & (k_pos < n_end) & (_chunk_id(k_pos) <= q_cid)
            s = [jnp.where(visible, s[h], -jnp.inf) for h in heads]
        m_prev = [m_sc[h] for h in heads]
        m_new = [jnp.maximum(m_prev[h], jnp.max(s[h], 0, keepdims=True)) for h in heads]
        p = [jnp.exp2(s[h] - m_new[h]) for h in heads]
        for h in heads:
            alpha = jnp.exp2(m_prev[h] - m_new[h])
            l_sc[h] = alpha * l_sc[h] + jnp.sum(p[h], 0, keepdims=True)
            acc_sc[h] = alpha * acc_sc[h] + jnp.dot(vt_ref[0, h * MLA_V:(h + 1) * MLA_V, keys],
                                                    p[h].astype(BF16), preferred_element_type=F32)
            m_sc[h] = m_new[h]

    def chunk_pair(j, carry):
        chunk(pl.multiple_of(j * (2 * FLASH_KEYS), 2 * FLASH_KEYS), 2 * FLASH_KEYS, None)
        return carry

    @pl.when(n_full >= 2)
    def _():
        chunk(0, 2 * FLASH_KEYS, "front")

    lax.fori_loop(1, n_full // 2, chunk_pair, 0)

    @pl.when(n_full % 2 == 1)
    def _():
        chunk(pl.multiple_of(full_end - FLASH_KEYS, FLASH_KEYS), FLASH_KEYS, "front")

    for offset in range(0, FLASH_KEYS, math.gcd(FLASH_KEYS, tq)):
        @pl.when(i * tq - full_end == offset)
        def _(width=offset + tq):
            chunk(pl.multiple_of(full_end, LANE), width, "causal")

    for h in heads:
        o_ref[0, :, h * MLA_V:(h + 1) * MLA_V] = (acc_sc[h] / l_sc[h]).T.astype(o_ref.dtype)


def _flash(q, k, vt, n_end):
    b, lp, _ = q.shape
    assert SEQ_TILE % LANE == 0 and FLASH_KEYS % LANE == 0 and SEQ_TILE % CHUNK == 0 and (FIRST + N_META) % CHUNK == 0
    assert FIRST + N_META <= min(SEQ_TILE, FLASH_KEYS)
    return pl.pallas_call(
        functools.partial(_flash_kernel, n_end=n_end),
        grid=(b, lp // SEQ_TILE),
        in_specs=[pl.BlockSpec((1, SEQ_TILE, MLA_HEADS * MLA_QK_PAD), lambda bi, i: (bi, i, 0)),
                  pl.BlockSpec((1, lp, MLA_HEADS * MLA_QK_PAD), lambda bi, i: (bi, 0, 0)),
                  pl.BlockSpec((1, MLA_WIDTH, lp), lambda bi, i: (bi, 0, 0))],
        out_specs=pl.BlockSpec((1, SEQ_TILE, MLA_WIDTH), lambda bi, i: (bi, i, 0)),
        out_shape=jax.ShapeDtypeStruct((b, lp, MLA_WIDTH), BF16),
        scratch_shapes=[pltpu.VMEM((MLA_HEADS, 1, SEQ_TILE), F32), pltpu.VMEM((MLA_HEADS, 1, SEQ_TILE), F32),
                        pltpu.VMEM((MLA_HEADS, MLA_V, SEQ_TILE), F32)],
        compiler_params=_cparams(("parallel", "arbitrary")),
        name="mla_flash",
    )(q, k, vt)


def _rwkv_prep_kernel(*refs, has_vres):
    (p_ref, prev_ref, mu_ref, w0_ref, w2_ref, a0_ref, a2_ref, g2_ref, kk_ref, ka_ref, rk_ref,
     bd_ref) = refs[:12]
    n_in = 16 if has_vres else 12
    r_out, k_out, v_out, kk_out, bb_out, lw_out, bonus_out, g_out = refs[n_in:]
    c = RWKV_WIDTH
    p = p_ref[0].astype(F32)
    tm = p.shape[0]
    row = lax.broadcasted_iota(jnp.int32, (tm, 1), 0)
    pos = pl.program_id(1) * tm + row
    prev = jnp.where(row == 0, prev_ref[0, PREV_ROWS - 1:PREV_ROWS, :].astype(F32), pltpu.roll(p, 1, 0))
    prev = jnp.where(pos <= FIRST, 0.0, prev)
    used = pos >= FIRST
    ps = p + (prev - p) * mu_ref[...]
    r = ps[:, :c]
    k = ps[:, c:2 * c]
    v = ps[:, 2 * c:3 * c]
    xwa = ps[:, 3 * c:3 * c + W_LORA + A_LORA]
    xg = ps[:, 3 * c + W_LORA + A_LORA:]
    zw = w0_ref[...] + jnp.dot(jnp.tanh(xwa).astype(BF16), w2_ref[...], preferred_element_type=F32)
    nz = -zw
    softplus = jnp.maximum(nz, 0.0) + jnp.log(1.0 + jnp.exp(-jnp.abs(nz)))
    log_decay = -jnp.exp(-softplus - 0.5)
    if has_vres:
        vf_ref, v0_ref, v1_ref, v2_ref = refs[12:16]
        lo = jnp.dot(v.astype(BF16), v1_ref[...], preferred_element_type=F32)
        gate = _sigmoid(v0_ref[...] + jnp.dot(lo.astype(BF16), v2_ref[...], preferred_element_type=F32))
        v = v + (vf_ref[0].astype(F32) - v) * gate
    a = _sigmoid(a0_ref[...] + jnp.dot(xwa.astype(BF16), a2_ref[...], preferred_element_type=F32))
    g = jnp.dot(_sigmoid(xg).astype(BF16), g2_ref[...], preferred_element_type=F32)
    bd = bd_ref[...]
    kk = k * kk_ref[...]
    kk = kk * lax.rsqrt(jnp.maximum(_head_sum(kk * kk, bd), 1e-24))
    k = k * (1.0 + (a - 1.0) * ka_ref[...])
    bonus = _head_sum(r * k * rk_ref[...], bd) * v
    r_out[0] = r.astype(r_out.dtype)
    k_out[0] = jnp.where(used, k, 0.0).astype(k_out.dtype)
    v_out[0] = jnp.where(used, v, 0.0).astype(v_out.dtype)
    kk_out[0] = jnp.where(used, kk, 0.0).astype(kk_out.dtype)
    bb_out[0] = jnp.where(used, kk * a, 0.0).astype(bb_out.dtype)
    lw_out[0] = log_decay
    bonus_out[0] = bonus.astype(bonus_out.dtype)
    g_out[0] = g.astype(g_out.dtype)


def _rwkv_prep(pr, prm, v_first):
    b, lp, cols = pr.shape
    has_vres = v_first is not None
    c = RWKV_WIDTH
    row = lambda bi, i: (bi, i, 0)
    prev = lambda bi, i: (bi, jnp.maximum(i * (SEQ_TILE // PREV_ROWS) - 1, 0), 0)
    names = ["mu", "w0", "w2", "a0", "a2", "g2", "k_k", "k_a", "r_k", "bd"]
    args = [pr, pr] + [prm[n] for n in names]
    in_specs = [pl.BlockSpec((1, SEQ_TILE, cols), row), pl.BlockSpec((1, PREV_ROWS, cols), prev)]
    in_specs += [_const_spec(prm[n].shape) for n in names]
    if has_vres:
        args += [v_first, prm["v0"], prm["v1"], prm["v2"]]
        in_specs += [pl.BlockSpec((1, SEQ_TILE, c), row)]
        in_specs += [_const_spec(prm[n].shape) for n in ("v0", "v1", "v2")]
    out_spec = pl.BlockSpec((1, SEQ_TILE, c), row)
    return pl.pallas_call(
        functools.partial(_rwkv_prep_kernel, has_vres=has_vres),
        grid=(b, lp // SEQ_TILE),
        in_specs=in_specs,
        out_specs=[out_spec] * 8,
        out_shape=[jax.ShapeDtypeStruct((b, lp, c), F32 if n == 5 else BF16) for n in range(8)],
        compiler_params=_cparams(("parallel", "parallel")),
        name="rwkv_prep",
    )(*args)


_NN = (((1,), (0,)), ((), ()))
_NT = (((1,), (1,)), ((), ()))


def _mm(a, b, dims):
    return lax.dot_general(a.astype(BF16), b.astype(BF16), dims, preferred_element_type=F32)


def _rwkv_scan_kernel(r_ref, k_ref, v_ref, kk_ref, bb_ref, lw_ref, y_ref, s_sc):
    @pl.when(pl.program_id(1) == 0)
    def _():
        s_sc[...] = jnp.zeros(s_sc.shape, F32)

    c = SCAN_CHUNK
    n = 2 * c
    n_chunks = r_ref.shape[1] // c
    ri = lax.broadcasted_iota(jnp.int32, (c, c), 0)
    ci = lax.broadcasted_iota(jnp.int32, (c, c), 1)
    tri = jnp.where(ri >= ci, 1.0, 0.0).astype(BF16)
    row = lax.broadcasted_iota(jnp.int32, (2 * n, 2 * n), 0)
    col = lax.broadcasted_iota(jnp.int32, (2 * n, 2 * n), 1)
    t_idx = row & (c - 1)
    s_idx = col & (c - 1)
    causal = (t_idx > s_idx) | ((t_idx == s_idx) & (row >= n))
    eye = jnp.where(lax.broadcasted_iota(jnp.int32, (n, n), 0) == lax.broadcasted_iota(jnp.int32, (n, n), 1),
                    1.0, 0.0).astype(F32)
    head0 = lax.broadcasted_iota(jnp.int32, (1, LANE), 1) < RWKV_HEAD

    def stack(x):
        return jnp.concatenate([jnp.where(head0, x, 0.0), jnp.where(head0, 0.0, x)], axis=0)

    ar, bk, v_t, w_end = [], [], [], []
    for ch in range(n_chunks):
        rows = slice(ch * c, (ch + 1) * c)
        lw = lw_ref[0, rows, :]
        lw_hi = lw.astype(BF16)
        lw_lo = (lw - lw_hi.astype(F32)).astype(BF16)
        cum = (jnp.dot(tri, lw_hi, preferred_element_type=F32)
               + jnp.dot(tri, lw_lo, preferred_element_type=F32))
        w_incl = jnp.exp(cum)
        w_inv = jnp.exp(-cum)
        a_hat = -kk_ref[0, rows, :].astype(F32) * jnp.exp(cum - lw)
        r_hat = r_ref[0, rows, :].astype(F32) * w_incl
        b_hat = bb_ref[0, rows, :].astype(F32) * w_inv
        k_hat = k_ref[0, rows, :].astype(F32) * w_inv
        v = v_ref[0, rows, :].astype(F32)
        for pr in range(RWKV_PAIRS):
            sl = slice(pr * LANE, (pr + 1) * LANE)
            ar.append(jnp.concatenate([stack(a_hat[:, sl]), stack(r_hat[:, sl])], 0).astype(BF16))
            bk.append(jnp.concatenate([stack(b_hat[:, sl]), stack(k_hat[:, sl])], 0).astype(BF16))
            v_t.append(stack(v[:, sl]).T.astype(BF16))
            w_end.append(w_incl[c - 1:c, sl])
    every = range(len(ar))
    gram = [jnp.where(causal, _mm(ar[g], bk[g], _NT), 0.0) for g in every]
    pw = [gram[g][:n, :n] for g in every]
    t_inv = [eye + pw[g] for g in every]
    pw = [_mm(pw[g], pw[g], _NN) for g in every]
    for _ in range(4):
        both = [_mm(pw[g], jnp.concatenate([pw[g], t_inv[g]], 1), _NN) for g in every]
        pw = [both[g][:, :n] for g in every]
        t_inv = [t_inv[g] + both[g][:, n:] for g in every]
    t_inv = [(t_inv[g] + _mm(pw[g], t_inv[g], _NN)).astype(BF16) for g in every]
    va = [_mm(v_t[g], gram[g][:n, n:], _NT) for g in every]
    q_bk = [gram[g][n:, :].astype(BF16) for g in every]

    state = [s_sc[pr] for pr in range(RWKV_PAIRS)]
    for ch in range(n_chunks):
        gs = [ch * RWKV_PAIRS + pr for pr in range(RWKV_PAIRS)]
        s_ar = [_mm(state[pr], ar[g], _NT) for pr, g in enumerate(gs)]
        u_t = [_mm(s_ar[pr][:, :n] + va[g], t_inv[g], _NT) for pr, g in enumerate(gs)]
        uv = [jnp.concatenate([u_t[pr].astype(BF16), v_t[g]], 1) for pr, g in enumerate(gs)]
        state = [(state[pr] + _mm(uv[pr], bk[g], _NN)) * w_end[g] for pr, g in enumerate(gs)]
        for pr, g in enumerate(gs):
            y = (s_ar[pr][:, n:] + _mm(uv[pr], q_bk[g], _NT)).T
            y_ref[0, ch * c:(ch + 1) * c, pr * LANE:(pr + 1) * LANE] = y[:c] + y[c:]
    for pr in range(RWKV_PAIRS):
        s_sc[pr] = state[pr]


def _rwkv_scan(r, k, v, kk, bb, lw):
    b, lp, c = r.shape
    rows = SCAN_STEP_CHUNKS * SCAN_CHUNK
    spec = pl.BlockSpec((1, rows, c), lambda bi, i: (bi, i, 0))
    return pl.pallas_call(
        _rwkv_scan_kernel,
        grid=(b, lp // rows),
        in_specs=[spec] * 6,
        out_specs=spec,
        out_shape=jax.ShapeDtypeStruct((b, lp, c), F32),
        scratch_shapes=[pltpu.VMEM((RWKV_PAIRS, LANE, LANE), F32)],
        compiler_params=_cparams(("parallel", "arbitrary")),
        name="rwkv_scan",
    )(r, k, v, kk, bb, lw)


def _outproj_kernel(ys_ref, bonus_ref, g_ref, ymla_ref, lnw_ref, lnb_ref, bd_ref, wo_ref, h_ref, o_ref):
    y = ys_ref[...]
    bd = bd_ref[...]
    inv_n = 1.0 / RWKV_HEAD
    d = y - _head_sum(y, bd) * inv_n
    var = _head_sum(d * d, bd) * inv_n
    yn = d * lax.rsqrt(var + GN_EPS) * lnw_ref[...] + lnb_ref[...]
    yr = ((yn + bonus_ref[...].astype(F32)) * g_ref[...].astype(F32)).astype(BF16)
    o_ref[...] = (h_ref[...]
                  + jnp.dot(ymla_ref[...], wo_ref[:MLA_WIDTH, :], preferred_element_type=F32)
                  + jnp.dot(yr, wo_ref[MLA_WIDTH:, :], preferred_element_type=F32))


def _outproj(ys, bonus, g, ymla, ln_w, ln_b, bd, wo, h):
    tp = h.shape[0]
    c = RWKV_WIDTH
    rc = pl.BlockSpec((ROW_TILE, c), lambda i: (i, 0))
    rd = pl.BlockSpec((ROW_TILE, D_MODEL), lambda i: (i, 0))
    return pl.pallas_call(
        _outproj_kernel,
        grid=(tp // ROW_TILE,),
        in_specs=[rc, rc, rc, rc, _const_spec((1, c)), _const_spec((1, c)), _const_spec(bd.shape),
                  _const_spec(wo.shape), rd],
        out_specs=rd,
        out_shape=jax.ShapeDtypeStruct((tp, D_MODEL), F32),
        compiler_params=_cparams(("parallel",)),
        name="outproj",
    )(ys, bonus, g, ymla, ln_w, ln_b, bd, wo, h)


def _ffn_chunks(u, wg_ref, wu_ref, wd_ref, idx, ff, chunk):
    acc = None
    for c0 in range(0, ff, chunk):
        sl = slice(c0, c0 + chunk)
        gate = jnp.dot(u, wg_ref[idx + (slice(None), sl)], preferred_element_type=F32)
        up = jnp.dot(u, wu_ref[idx + (slice(None), sl)], preferred_element_type=F32)
        act = (gate * _sigmoid(gate) * up).astype(BF16)
        part = jnp.dot(act, wd_ref[idx + (sl, slice(None))], preferred_element_type=F32)
        acc = part if acc is None else acc + part
    return acc


def _dense_ffn_kernel(h_ref, g_ref, wg_ref, wu_ref, wd_ref, o_ref, *, ff, chunk):
    h = h_ref[...]
    u = _rms(h, g_ref[...]).astype(BF16)
    o_ref[...] = h + _ffn_chunks(u, wg_ref, wu_ref, wd_ref, (), ff, chunk)


def _dense_ffn(h, g, wg, wu, wd):
    tp = h.shape[0]
    ff = wg.shape[1]
    rd = pl.BlockSpec((ROW_TILE, D_MODEL), lambda i: (i, 0))
    once = lambda shape: pl.BlockSpec(shape, lambda i: (0, 0), pipeline_mode=pl.Buffered(1))
    return pl.pallas_call(
        functools.partial(_dense_ffn_kernel, ff=ff, chunk=256),
        grid=(tp // ROW_TILE,),
        in_specs=[rd, _const_spec((1, D_MODEL)), once(wg.shape), once(wu.shape), once(wd.shape)],
        out_specs=rd,
        out_shape=jax.ShapeDtypeStruct((tp, D_MODEL), F32),
        compiler_params=_cparams(("parallel",)),
        name="dense_ffn",
    )(h, g, wg, wu, wd)


def _router_kernel(h_ref, g_ref, wr_ref, br_ref, u_ref, meta_ref, meta_t_ref, cnt_ref, carry_sc, *, n_end):
    @pl.when((pl.program_id(0) == 0) & (pl.program_id(1) == 0))
    def _():
        carry_sc[...] = jnp.zeros(carry_sc.shape, F32)

    tm = h_ref.shape[1]
    u = _rms(h_ref[0], g_ref[...])
    _to_row_tiles(u_ref, u)
    u_hi, u_lo = _split(u)
    w_hi, w_lo = _split(wr_ref[...])
    logits = (jnp.dot(u_hi, w_hi, preferred_element_type=F32) + jnp.dot(u_hi, w_lo, preferred_element_type=F32)
              + jnp.dot(u_lo, w_hi, preferred_element_type=F32) + br_ref[...])
    lane = lax.broadcasted_iota(jnp.int32, (tm, LANE), 1).astype(F32)
    logits = jnp.where(lane < N_EXPERTS, logits, -jnp.inf)
    top0 = jnp.max(logits, -1, keepdims=True)
    e0 = jnp.min(jnp.where(logits == top0, lane, float(LANE)), -1, keepdims=True)
    rest = jnp.where(lane == e0, -jnp.inf, logits)
    top1 = jnp.max(rest, -1, keepdims=True)
    e1 = jnp.min(jnp.where(rest == top1, lane, float(LANE)), -1, keepdims=True)
    ex = jnp.exp(top1 - top0)
    g0 = 1.0 / (1.0 + ex)
    g1 = ex / (1.0 + ex)
    pos = pl.program_id(1) * tm + lax.broadcasted_iota(jnp.int32, (tm, 1), 0)
    valid = (pos >= FIRST) & (pos < n_end)
    oh0 = jnp.where((lane == e0) & valid, 1.0, 0.0)
    oh1 = jnp.where((lane == e1) & valid, 1.0, 0.0)
    ri = lax.broadcasted_iota(jnp.int32, (tm, tm), 0)
    ci = lax.broadcasted_iota(jnp.int32, (tm, tm), 1)
    before = jnp.where(ri > ci, 1.0, 0.0).astype(BF16)
    both = oh0 + oh1
    seen = carry_sc[...] + jnp.dot(before, both.astype(BF16), preferred_element_type=F32)
    rank0 = jnp.sum(jnp.where(lane == e0, seen, 0.0), -1, keepdims=True)
    rank1 = jnp.sum(jnp.where(lane == e1, seen, 0.0), -1, keepdims=True)
    carry_sc[...] = carry_sc[...] + jnp.sum(both, 0, keepdims=True)
    cnt_ref[...] = jnp.broadcast_to(carry_sc[...], cnt_ref.shape)
    vf = jnp.where(valid, 1.0, 0.0)
    meta = jnp.where(lane == 0, e0, 0.0)
    meta = jnp.where(lane == 1, e1, meta)
    meta = jnp.where(lane == 2, rank0, meta)
    meta = jnp.where(lane == 3, rank1, meta)
    meta = jnp.where(lane == 4, g0 * vf, meta)
    meta = jnp.where(lane == 5, g1 * vf, meta)
    meta = jnp.where(lane == 6, vf, meta)
    meta_ref[0] = meta
    meta_t_ref[...] = meta.T[:8]


def _router(h, g, wr, br, n_end):
    b, lp, _ = h.shape
    row = lambda bi, i: (bi, i, 0)
    return pl.pallas_call(
        functools.partial(_router_kernel, n_end=n_end),
        grid=(b, lp // SEQ_TILE),
        in_specs=[pl.BlockSpec((1, SEQ_TILE, D_MODEL), row), _const_spec((1, D_MODEL)),
                  _const_spec(wr.shape), _const_spec(br.shape)],
        out_specs=[pl.BlockSpec((SEQ_TILE * ROW_TILES, LANE), lambda bi, i: (bi * (lp // SEQ_TILE) + i, 0)),
                   pl.BlockSpec((1, SEQ_TILE, LANE), row),
                   pl.BlockSpec((8, SEQ_TILE), lambda bi, i: (0, bi * (lp // SEQ_TILE) + i)),
                   _const_spec((8, LANE))],
        out_shape=[jax.ShapeDtypeStruct((b * lp * ROW_TILES, LANE), F32), jax.ShapeDtypeStruct((b, lp, LANE), F32),
                   jax.ShapeDtypeStruct((8, b * lp), F32), jax.ShapeDtypeStruct((8, LANE), F32)],
        scratch_shapes=[pltpu.VMEM((1, LANE), F32)],
        compiler_params=_cparams(("arbitrary", "arbitrary")),
        name="moe_router",
    )(h, g, wr, br)


def _dispatch_kernel(s0_ref, s1_ref, u_ref, xb_in_ref, xb_ref, sem):
    del xb_in_ref
    tm = u_ref.shape[0] // ROW_TILES

    def copies(rw):
        return [pltpu.make_async_copy(_row_tile(u_ref, rw), _row_tile(xb_ref, s_ref[0, 0, rw]), sem)
                for s_ref in (s0_ref, s1_ref)]

    def start(j, carry):
        for kq in range(DMA_UNROLL):
            for thread, cp in enumerate(copies(j * DMA_UNROLL + kq)):
                cp.start(priority=thread)
        return carry

    lax.fori_loop(0, tm // DMA_UNROLL, start, 0)
    for _ in (s0_ref, s1_ref):
        pltpu.make_async_copy(u_ref, xb_ref.at[pl.ds(0, tm * ROW_TILES), :], sem).wait()


def _dispatch(slot0, slot1, u, n_rows, tm):
    tp = u.shape[0] // ROW_TILES
    sspec = pl.BlockSpec((1, 1, tm), lambda i: (i, 0, 0), memory_space=pltpu.SMEM)
    xb0 = jnp.zeros((n_rows * ROW_TILES, LANE), F32)
    return pl.pallas_call(
        _dispatch_kernel,
        grid=(tp // tm,),
        in_specs=[sspec, sspec, pl.BlockSpec((tm * ROW_TILES, LANE), lambda i: (i, 0)),
                  pl.BlockSpec(memory_space=pl.ANY)],
        out_specs=pl.BlockSpec(memory_space=pl.ANY),
        out_shape=jax.ShapeDtypeStruct((n_rows * ROW_TILES, LANE), F32),
        scratch_shapes=[pltpu.SemaphoreType.DMA(())],
        input_output_aliases={3: 0},
        compiler_params=_cparams(("arbitrary",)),
        name="moe_dispatch",
    )(slot0.reshape(tp // tm, 1, tm), slot1.reshape(tp // tm, 1, tm), u, xb0)


def _expert_kernel(be_ref, fill_ref, x_ref, wg_ref, wu_ref, wd_ref, o_ref, *, ff, chunk):
    del be_ref
    fill = fill_ref[pl.program_id(0)]

    @pl.when(fill > 0)
    def _():
        x = _from_row_tiles(x_ref, MOE_BLOCK).astype(BF16)
        _to_row_tiles(o_ref, _ffn_chunks(x, wg_ref, wu_ref, wd_ref, (0,), ff, chunk))

    @pl.when(fill <= 0)
    def _():
        o_ref[...] = jnp.zeros(o_ref.shape, F32)


def _experts(block_e, block_fill, xb, wg, wu, wd):
    n_slots = block_e.shape[0] * MOE_BLOCK
    ff = wg.shape[2]
    rows = pl.BlockSpec((MOE_BLOCK * ROW_TILES, LANE), lambda i, be, fill: (i, 0))
    return pl.pallas_call(
        functools.partial(_expert_kernel, ff=ff, chunk=ff // 2),
        grid_spec=pltpu.PrefetchScalarGridSpec(
            num_scalar_prefetch=2,
            grid=(n_slots // MOE_BLOCK,),
            in_specs=[rows,
                      pl.BlockSpec((1, D_MODEL, ff), lambda i, be, fill: (be[i], 0, 0)),
                      pl.BlockSpec((1, D_MODEL, ff), lambda i, be, fill: (be[i], 0, 0)),
                      pl.BlockSpec((1, ff, D_MODEL), lambda i, be, fill: (be[i], 0, 0))],
            out_specs=rows),
        out_shape=jax.ShapeDtypeStruct((n_slots * ROW_TILES, LANE), F32),
        compiler_params=_cparams(("arbitrary",), 60 * 1024 * 1024),
        name="moe_experts",
    )(block_e, block_fill, xb, wg, wu, wd)


def _combine_kernel(s0_ref, s1_ref, n0_ref, n1_ref, h_ref, gates_ref, fn_ref, yb_ref, o_ref, y0_sc, y1_sc, sems, *,
                    per_row, n_frame_tiles):
    t = pl.program_id(0)
    tm = h_ref.shape[0]
    cur = t % 2

    def copies(rw, a_ref, b_ref, buf):
        return [pltpu.make_async_copy(_row_tile(yb_ref, s_ref[0, 0, rw]), _row_tile(y_sc.at[buf], rw), sems.at[buf])
                for s_ref, y_sc in ((a_ref, y0_sc), (b_ref, y1_sc))]

    def issue(a_ref, b_ref, buf):
        def body(j, carry):
            for kq in range(DMA_UNROLL):
                for thread, cp in enumerate(copies(j * DMA_UNROLL + kq, a_ref, b_ref, buf)):
                    cp.start(priority=thread)
            return carry
        lax.fori_loop(0, tm // DMA_UNROLL, body, 0)

    @pl.when(t == 0)
    def _():
        issue(s0_ref, s1_ref, 0)

    @pl.when(t + 1 < pl.num_programs(0))
    def _():
        issue(n0_ref, n1_ref, 1 - cur)

    for y_sc in (y0_sc, y1_sc):
        pltpu.make_async_copy(yb_ref.at[pl.ds(0, tm * ROW_TILES), :], y_sc.at[cur], sems.at[cur]).wait()
    tile = t % per_row

    @pl.when((tile >= 1) & (tile <= n_frame_tiles))
    def _():
        gates = gates_ref[...]
        f = (_from_row_tiles(y0_sc.at[cur], tm) * gates[:, 4:5] + _from_row_tiles(y1_sc.at[cur], tm) * gates[:, 5:6])
        o_ref[0] = _rms(h_ref[...] + f, fn_ref[...])


def _combine(slot0, slot1, h, meta, fn, yb, tm, b, seq):
    tp = h.shape[0]
    n_tiles = tp // tm
    per_row = n_tiles // b
    assert (FIRST + N_META) == tm and seq % tm == 0
    sspec = pl.BlockSpec((1, 1, tm), lambda i: (i, 0, 0), memory_space=pltpu.SMEM)
    nspec = pl.BlockSpec((1, 1, tm), lambda i: (jnp.minimum(i + 1, n_tiles - 1), 0, 0), memory_space=pltpu.SMEM)
    rd = pl.BlockSpec((tm, D_MODEL), lambda i: (i, 0))
    s0 = slot0.reshape(n_tiles, 1, tm)
    s1 = slot1.reshape(n_tiles, 1, tm)
    return pl.pallas_call(
        functools.partial(_combine_kernel, per_row=per_row, n_frame_tiles=seq // tm),
        grid=(n_tiles,),
        in_specs=[sspec, sspec, nspec, nspec, rd, pl.BlockSpec((tm, LANE), lambda i: (i, 0)),
                  _const_spec((1, D_MODEL)), pl.BlockSpec(memory_space=pl.ANY)],
        out_specs=pl.BlockSpec((1, tm, D_MODEL), lambda i: (i // per_row, jnp.clip(i % per_row - 1, 0, seq // tm - 1), 0)),
        out_shape=jax.ShapeDtypeStruct((b, seq, D_MODEL), F32),
        scratch_shapes=[pltpu.VMEM((2, tm * ROW_TILES, LANE), F32), pltpu.VMEM((2, tm * ROW_TILES, LANE), F32),
                        pltpu.SemaphoreType.DMA((2,))],
        compiler_params=_cparams(("arbitrary",)),
        name="moe_combine",
    )(s0, s1, s0, s1, h, meta, fn, yb)


def _final_norm_kernel(h_ref, g_ref, o_ref):
    o_ref[...] = _rms(h_ref[...], g_ref[...])


def _final_norm(h, g):
    tp = h.shape[0]
    rd = pl.BlockSpec((ROW_TILE, D_MODEL), lambda i: (i, 0))
    return pl.pallas_call(
        _final_norm_kernel,
        grid=(tp // ROW_TILE,),
        in_specs=[rd, _const_spec((1, D_MODEL))],
        out_specs=rd,
        out_shape=jax.ShapeDtypeStruct((tp, D_MODEL), F32),
        compiler_params=_cparams(("parallel",)),
        name="final_norm",
    )(h, g)


def _rot_cols(w):
    half = MLA_ROPE // 2
    return jnp.concatenate([-w[..., half:], w[..., :half]], -1)


def _prep_inproj(w_in):
    m_q = w_in[:, :MLA_Q_LORA + MLA_KV_LORA]
    k_rope = w_in[:, MLA_Q_LORA + MLA_KV_LORA:MLA_Q_LORA + MLA_KV_LORA + MLA_ROPE]
    w_m = jnp.concatenate([m_q, k_rope, _rot_cols(k_rope)], -1)
    w_r = w_in[:, MLA_Q_LORA + MLA_KV_LORA + MLA_ROPE:]
    return w_r.astype(BF16), w_m.astype(BF16)


def _prep_wq(w_uq):
    w = w_uq.reshape(MLA_Q_LORA, MLA_HEADS, MLA_NOPE + MLA_ROPE)
    rope = w[..., MLA_NOPE:]
    return jnp.concatenate([w, _rot_cols(rope)], -1).reshape(MLA_Q_LORA, MLA_HEADS * MLA_QK_PAD).astype(BF16)


def _prep_wkv(w_ukv):
    w = w_ukv.reshape(MLA_KV_LORA, MLA_HEADS, MLA_NOPE + MLA_V)
    k_nope = w[..., :MLA_NOPE].reshape(MLA_KV_LORA, MLA_HEADS * MLA_NOPE)
    v = w[..., MLA_NOPE:].reshape(MLA_KV_LORA, MLA_HEADS * MLA_V)
    return jnp.concatenate([k_nope, v], -1).astype(BF16)


def _row(x):
    return x.reshape(1, -1).astype(F32)


def _block_diag_ones():
    idx = np.arange(RWKV_WIDTH) // RWKV_HEAD
    return jnp.asarray(idx[:, None] == idx[None, :], BF16)


def kernel(x, meta_tokens, attn_norm, ffn_norm, final_norm, w_in, w_out, mla_q_norm, mla_kv_norm, mla_w_uq, mla_w_ukv, rwkv_mu, rwkv_w0, rwkv_w2, rwkv_a0, rwkv_a2, rwkv_g2, rwkv_k_k, rwkv_k_a, rwkv_r_k, rwkv_ln_w, rwkv_ln_b, rwkv_v0, rwkv_v1, rwkv_v2, ffn_w_gate, ffn_w_up, ffn_w_down, moe_router, moe_router_bias, moe_w_gate, moe_w_up, moe_w_down):
    b, seq, d = x.shape
    depth = attn_norm.shape[0]
    n_end = FIRST + N_META + seq
    lp = -(-n_end // SEQ_TILE) * SEQ_TILE
    tp = b * lp
    assert d == D_MODEL and tp % ROW_TILE == 0 and lp % SCAN_CHUNK == 0

    meta = jnp.broadcast_to(meta_tokens[None].astype(x.dtype), (b, N_META, d))
    h = jnp.concatenate([jnp.zeros((b, FIRST, d), x.dtype), meta, x, jnp.zeros((b, lp - n_end, d), x.dtype)],
                        axis=1).reshape(tp, d)

    tabs = _rope_tables(lp)
    bd = _block_diag_ones()
    zeros_wa = jnp.zeros((W_LORA, RWKV_WIDTH), F32)
    v_first = None
    for i in range(depth):
        w_r, w_m = _prep_inproj(w_in[i])
        pr, pm = _norm_inproj(h, _row(attn_norm[i]), w_r, w_m)

        q, k, v = _mla_up(pm.reshape(b, lp, -1), _row(mla_q_norm[i]), _row(mla_kv_norm[i]),
                          _prep_wq(mla_w_uq[i]), _prep_wkv(mla_w_ukv[i]), tabs)
        y_mla = _flash(q, k, v, n_end)

        prm = {
            "mu": _row(rwkv_mu[i]), "w0": _row(rwkv_w0[i]), "a0": _row(rwkv_a0[i]),
            "w2": jnp.concatenate([rwkv_w2[i], zeros_wa], 0).astype(BF16),
            "a2": jnp.concatenate([zeros_wa, rwkv_a2[i]], 0).astype(BF16),
            "g2": rwkv_g2[i].astype(BF16),
            "k_k": _row(rwkv_k_k[i]), "k_a": _row(rwkv_k_a[i]), "r_k": _row(rwkv_r_k[i]), "bd": bd,
        }
        if i > 0:
            prm["v0"] = _row(rwkv_v0[i - 1])
            prm["v1"] = jnp.pad(rwkv_v1[i - 1], ((0, 0), (0, LANE - V_LORA))).astype(BF16)
            prm["v2"] = jnp.pad(rwkv_v2[i - 1], ((0, LANE - V_LORA), (0, 0))).astype(BF16)
        r_, k_, v_, kk_, bb_, lw_, bonus, gate = _rwkv_prep(pr.reshape(b, lp, -1), prm, v_first)
        if i == 0:
            v_first = v_
        ys = _rwkv_scan(r_, k_, v_, kk_, bb_, lw_)

        flat = lambda t: t.reshape(tp, -1)
        h = _outproj(flat(ys), flat(bonus), flat(gate), flat(y_mla), _row(rwkv_ln_w[i]), _row(rwkv_ln_b[i]),
                     bd, w_out[i].astype(BF16), h)

        j = i // 2
        last = i == depth - 1
        if i % 2 == 0:
            h = _dense_ffn(h, _row(ffn_norm[i]), ffn_w_gate[j].astype(BF16), ffn_w_up[j].astype(BF16),
                           ffn_w_down[j].astype(BF16))
            if last:
                h = _final_norm(h, _row(final_norm)).reshape(b, lp, d)[:, FIRST + N_META:n_end]
        else:
            if not last:
                raise NotImplementedError("an MoE layer that is not the last layer")
            h = _moe(h, b, lp, n_end, _row(ffn_norm[i]), moe_router[j], moe_router_bias[j],
                     moe_w_gate[j], moe_w_up[j], moe_w_down[j], _row(final_norm))
    return h


def _moe(h, b, lp, n_end, g, router, router_bias, w_gate, w_up, w_down, final_g):
    tp = b * lp
    wr = jnp.pad(router, ((0, 0), (0, LANE - N_EXPERTS))).astype(F32)
    br = jnp.pad(router_bias, (0, LANE - N_EXPERTS)).reshape(1, LANE).astype(F32)
    u, meta, meta_t, cnt = _router(h.reshape(b, lp, D_MODEL), g, wr, br, n_end)
    meta = meta.reshape(tp, LANE)

    counts = cnt[0, :N_EXPERTS].astype(jnp.int32)
    padded = (counts + MOE_BLOCK - 1) // MOE_BLOCK * MOE_BLOCK
    pend = jnp.cumsum(padded)
    pstart = pend - padded
    n_assign = b * (n_end - FIRST) * TOP_K
    n_blocks = (n_assign + N_EXPERTS * (MOE_BLOCK - 1)) // MOE_BLOCK + 1
    n_slots = n_blocks * MOE_BLOCK
    block_start = jnp.arange(n_blocks, dtype=jnp.int32) * MOE_BLOCK
    block_e = jnp.minimum(jnp.sum((pend[None, :] <= block_start[:, None]).astype(jnp.int32), -1),
                          N_EXPERTS - 1)
    e0, e1, rank0, rank1 = (meta_t[n].astype(jnp.int32) for n in range(4))
    valid = meta_t[6] > 0.5
    slot0 = pstart[e0] + rank0
    slot1 = pstart[e1] + rank1
    spare = n_slots + jnp.arange(tp, dtype=jnp.int32) % LANE
    d0 = jnp.where(valid, slot0, spare)
    d1 = jnp.where(valid, slot1, spare + LANE)
    c0 = jnp.where(valid, slot0, 0)
    c1 = jnp.where(valid, slot1, 0)

    xb = _dispatch(d0, d1, u, n_slots + 2 * LANE, LANE)
    block_fill = jnp.clip(pstart[block_e] + counts[block_e] - block_start, 0, MOE_BLOCK).astype(jnp.int32)
    yb = _experts(block_e, block_fill, xb, w_gate.astype(BF16), w_up.astype(BF16), w_down.astype(BF16))
    return _combine(c0, c1, h, meta, final_g, yb, LANE, b, n_end - FIRST - N_META)
```

```python
import functools
import math

import numpy as np
import jax
import jax.numpy as jnp
from jax import lax
from jax.experimental import pallas as pl
from jax.experimental.pallas import tpu as pltpu

F32 = jnp.float32
BF16 = jnp.bfloat16

D_MODEL = 1024
CHUNK = 64
N_META = 16
RMS_EPS = 1e-6

MLA_HEADS = 4
MLA_NOPE = 128
MLA_ROPE = 64
MLA_V = 128
MLA_Q_LORA = 512
MLA_KV_LORA = 256
ROPE_THETA = 10000.0
MLA_WIDTH = MLA_HEADS * MLA_V
MLA_QK_PAD = 256
Q_SCALE = float((MLA_NOPE + MLA_ROPE) ** -0.5 * np.log2(np.e))
MLA_COLS_PAD = MLA_Q_LORA + MLA_KV_LORA + 2 * MLA_ROPE

RWKV_WIDTH = 512
RWKV_HEAD = 64
RWKV_HEADS = RWKV_WIDTH // RWKV_HEAD
RWKV_PAIRS = RWKV_HEADS // 2
W_LORA = 64
A_LORA = 64
V_LORA = 32
G_LORA = 128
GN_EPS = 64e-5
RWKV_COLS = 3 * RWKV_WIDTH + W_LORA + A_LORA + G_LORA
SCAN_CHUNK = 64
SCAN_STEP_CHUNKS = 6

N_EXPERTS = 8
TOP_K = 2
MOE_BLOCK = 256

LANE = 128
FRONT_PAD = -N_META % LANE
FIRST = FRONT_PAD
SEQ_TILE = 384
FLASH_KEYS = 256
ROW_TILE = 512
PREV_ROWS = 16
ROW_TILES = D_MODEL // LANE
DMA_UNROLL = 8
VMEM_LIMIT = 56 * 1024 * 1024


def _cparams(sem, vmem=VMEM_LIMIT):
    return pltpu.CompilerParams(dimension_semantics=sem, vmem_limit_bytes=vmem)


def _rms(x, g):
    return x * lax.rsqrt(jnp.mean(x * x, -1, keepdims=True) + RMS_EPS) * g


def _sigmoid(x):
    return 1.0 / (1.0 + jnp.exp(-x))


def _split(x):
    hi = x.astype(BF16)
    return hi, (x - hi.astype(F32)).astype(BF16)


def _head_sum(x, bd):
    return jnp.dot(x.astype(BF16), bd, preferred_element_type=F32)


def _to_row_tiles(ref, x):
    for s in range(ROW_TILES):
        ref[pl.ds(s, x.shape[0], stride=ROW_TILES), :] = x[:, s * LANE:(s + 1) * LANE]


def _from_row_tiles(ref, rows):
    return jnp.concatenate([ref[pl.ds(s, rows, stride=ROW_TILES), :] for s in range(ROW_TILES)], axis=1)


def _row_tile(ref, r):
    return ref.at[pl.ds(pl.multiple_of(r * ROW_TILES, ROW_TILES), ROW_TILES), :]


def _const_spec(shape):
    nd = len(shape)
    return pl.BlockSpec(shape, lambda *_: (0,) * nd)


def _norm_inproj_kernel(h_ref, g_ref, wr_ref, wm_ref, pr_ref, pm_ref):
    u = _rms(h_ref[...], g_ref[...]).astype(BF16)
    pr_ref[...] = jnp.dot(u, wr_ref[...], preferred_element_type=F32).astype(pr_ref.dtype)
    pm_ref[...] = jnp.dot(u, wm_ref[...], preferred_element_type=F32).astype(pm_ref.dtype)


def _norm_inproj(h, g, w_r, w_m):
    tp = h.shape[0]
    return pl.pallas_call(
        _norm_inproj_kernel,
        grid=(tp // ROW_TILE,),
        in_specs=[pl.BlockSpec((ROW_TILE, D_MODEL), lambda i: (i, 0)),
                  _const_spec((1, D_MODEL)),
                  _const_spec(w_r.shape), _const_spec(w_m.shape)],
        out_specs=[pl.BlockSpec((ROW_TILE, w_r.shape[1]), lambda i: (i, 0)),
                   pl.BlockSpec((ROW_TILE, w_m.shape[1]), lambda i: (i, 0))],
        out_shape=[jax.ShapeDtypeStruct((tp, w_r.shape[1]), BF16),
                   jax.ShapeDtypeStruct((tp, w_m.shape[1]), BF16)],
        compiler_params=_cparams(("parallel",)),
        name="norm_inproj",
    )(h, g, w_r, w_m)


def _embed_inproj_kernel(*refs, n_end):
    n_sub = SEQ_TILE // LANE
    x_refs = refs[:n_sub]
    meta_ref, g_ref, wr_ref, wm_ref, h_ref, pr_ref, pm_ref = refs[n_sub:]
    i = pl.program_id(1)
    front = jnp.concatenate([jnp.zeros((FRONT_PAD, D_MODEL), F32), meta_ref[...]], 0)
    parts = [jnp.where(i == 0, front, x_refs[0][0])] + [r[0] for r in x_refs[1:]]
    h = jnp.concatenate(parts, 0)
    pos = i * SEQ_TILE + lax.broadcasted_iota(jnp.int32, (SEQ_TILE, 1), 0)
    h = jnp.where(pos < n_end, h, 0.0)
    h_ref[0] = h
    u = _rms(h, g_ref[...]).astype(BF16)
    pr_ref[0] = jnp.dot(u, wr_ref[...], preferred_element_type=F32).astype(pr_ref.dtype)
    pm_ref[0] = jnp.dot(u, wm_ref[...], preferred_element_type=F32).astype(pm_ref.dtype)


def _embed_inproj(x, meta_tokens, lp, g, w_r, w_m):
    b, seq, _ = x.shape
    n_sub = SEQ_TILE // LANE
    assert FRONT_PAD + N_META == LANE and seq % LANE == 0
    last = seq // LANE - 1
    x_specs = [pl.BlockSpec((1, LANE, D_MODEL), functools.partial(
        lambda bi, i, k: (bi, jnp.clip(i * n_sub + k - 1, 0, last), 0), k=k)) for k in range(n_sub)]
    row = lambda bi, i: (bi, i, 0)
    return pl.pallas_call(
        functools.partial(_embed_inproj_kernel, n_end=FIRST + N_META + seq),
        grid=(b, lp // SEQ_TILE),
        in_specs=x_specs + [_const_spec((N_META, D_MODEL)), _const_spec((1, D_MODEL)),
                            _const_spec(w_r.shape), _const_spec(w_m.shape)],
        out_specs=[pl.BlockSpec((1, SEQ_TILE, D_MODEL), row), pl.BlockSpec((1, SEQ_TILE, w_r.shape[1]), row),
                   pl.BlockSpec((1, SEQ_TILE, w_m.shape[1]), row)],
        out_shape=[jax.ShapeDtypeStruct((b, lp, D_MODEL), F32), jax.ShapeDtypeStruct((b, lp, w_r.shape[1]), BF16),
                   jax.ShapeDtypeStruct((b, lp, w_m.shape[1]), BF16)],
        compiler_params=_cparams(("parallel", "parallel")),
        name="embed_inproj",
    )(*([x] * n_sub), meta_tokens.astype(F32), g, w_r, w_m)


def _mla_up_kernel(pm_ref, qn_ref, kvn_ref, wq_ref, wkv_ref, tq_ref, tk_ref, q_ref, k_ref, vt_ref):
    pm = pm_ref[0].astype(F32)
    c_q = pm[:, :MLA_Q_LORA]
    c_kv = pm[:, MLA_Q_LORA:MLA_Q_LORA + MLA_KV_LORA]
    k_r = pm[:, MLA_Q_LORA + MLA_KV_LORA:]
    q = jnp.dot(_rms(c_q, qn_ref[...]).astype(BF16), wq_ref[...], preferred_element_type=F32)
    kv = jnp.dot(_rms(c_kv, kvn_ref[...]).astype(BF16), wkv_ref[...], preferred_element_type=F32)
    rope_lanes = lax.broadcasted_iota(jnp.int32, (1, LANE), 1) < MLA_ROPE
    t_k = tk_ref[...]
    ck = jnp.where(rope_lanes, t_k, 0.0)
    sk = jnp.where(rope_lanes, pltpu.roll(t_k, MLA_ROPE, 1), 0.0)
    k_rope = (k_r * ck + pltpu.roll(k_r, MLA_ROPE, 1) * sk).astype(BF16)
    t_q = tq_ref[...]
    cq = jnp.concatenate([jnp.full(t_q.shape, Q_SCALE, F32), jnp.where(rope_lanes, t_q, 0.0)], 1)
    sq = jnp.concatenate([jnp.zeros(t_q.shape, F32), jnp.where(rope_lanes, pltpu.roll(t_q, MLA_ROPE, 1), 0.0)], 1)
    for hd in range(MLA_HEADS):
        qh = q[:, hd * MLA_QK_PAD:(hd + 1) * MLA_QK_PAD]
        qh = qh * cq + pltpu.roll(qh, MLA_QK_PAD - MLA_ROPE, 1) * sq
        q_ref[0, :, hd * MLA_QK_PAD:(hd + 1) * MLA_QK_PAD] = qh.astype(BF16)
        k_ref[0, :, hd * MLA_QK_PAD:hd * MLA_QK_PAD + MLA_NOPE] = (
            kv[:, hd * MLA_NOPE:(hd + 1) * MLA_NOPE].astype(BF16))
        k_ref[0, :, hd * MLA_QK_PAD + MLA_NOPE:(hd + 1) * MLA_QK_PAD] = k_rope
    vt_ref[0] = kv[:, MLA_HEADS * MLA_NOPE:].T.astype(BF16)


def _mla_up(pm, q_norm, kv_norm, wq, wkv, tabs):
    b, lp, _ = pm.shape
    t_q, t_k = tabs
    qk_w = MLA_HEADS * MLA_QK_PAD
    row = lambda bi, i: (bi, i, 0)
    tab = lambda bi, i: (i, 0)
    return pl.pallas_call(
        _mla_up_kernel,
        grid=(b, lp // SEQ_TILE),
        in_specs=[pl.BlockSpec((1, SEQ_TILE, MLA_COLS_PAD), row),
                  _const_spec((1, MLA_Q_LORA)), _const_spec((1, MLA_KV_LORA)),
                  _const_spec(wq.shape), _const_spec(wkv.shape),
                  pl.BlockSpec((SEQ_TILE, LANE), tab), pl.BlockSpec((SEQ_TILE, LANE), tab)],
        out_specs=[pl.BlockSpec((1, SEQ_TILE, qk_w), row),
                   pl.BlockSpec((1, SEQ_TILE, qk_w), row),
                   pl.BlockSpec((1, MLA_WIDTH, SEQ_TILE), lambda bi, i: (bi, 0, i))],
        out_shape=[jax.ShapeDtypeStruct((b, lp, qk_w), BF16),
                   jax.ShapeDtypeStruct((b, lp, qk_w), BF16),
                   jax.ShapeDtypeStruct((b, MLA_WIDTH, lp), BF16)],
        compiler_params=_cparams(("parallel", "parallel")),
        name="mla_up",
    )(pm, q_norm, kv_norm, wq, wkv, t_q, t_k)


def _rope_tables(lp):
    half = MLA_ROPE // 2
    inv = (np.float32(ROPE_THETA) ** (-np.arange(half, dtype=np.float32) / np.float32(half))).astype(np.float32)
    pos = np.maximum(np.arange(lp) - FIRST, 0).astype(np.float32)
    ang = (pos[:, None] * inv[None, :]).astype(np.float64)
    t_k = np.concatenate([np.cos(ang), np.cos(ang), np.sin(ang), np.sin(ang)], -1)
    return jnp.asarray(t_k * Q_SCALE, F32), jnp.asarray(t_k, F32)


def _chunk_id(row):
    frame = row - (FIRST + N_META)
    return jnp.where(frame < 0, 0, 1 + (frame >> 6))


def _flash_kernel(q_ref, k_ref, vt_ref, o_ref, m_sc, l_sc, acc_sc, *, n_end):
    i = pl.program_id(1)
    tq = q_ref.shape[1]
    lp = k_ref.shape[1]
    heads = range(MLA_HEADS)
    q = [q_ref[0, :, h * MLA_QK_PAD:(h + 1) * MLA_QK_PAD] for h in heads]
    m_sc[...] = jnp.full(m_sc.shape, -jnp.inf, F32)
    l_sc[...] = jnp.zeros(l_sc.shape, F32)
    acc_sc[...] = jnp.zeros(acc_sc.shape, F32)
    n_full = (i * tq) // FLASH_KEYS
    full_end = n_full * FLASH_KEYS
    q_cid = _chunk_id(i * tq + lax.broadcasted_iota(jnp.int32, (1, tq), 1))

    def chunk(start, width, mask):
        keys = pl.ds(start, width)
        s = [lax.dot_general(k_ref[0, keys, h * MLA_QK_PAD:(h + 1) * MLA_QK_PAD], q[h], _NT,
                             preferred_element_type=F32) for h in heads]
        if mask is not None:
            k_pos = start + lax.broadcasted_iota(jnp.int32, (width, 1), 0)
            visible = k_pos >= FIRST
            if mask == "causal":
                visible = visible & (k_pos >= full_end) & (k_pos < n_end) & (_chunk_id(k_pos) <= q_cid)
            s = [jnp.where(visible, s[h], -jnp.inf) for h in heads]
        m_prev = [m_sc[h] for h in heads]
        m_new = [jnp.maximum(m_prev[h], jnp.max(s[h], 0, keepdims=True)) for h in heads]
        p = [jnp.exp2(s[h] - m_new[h]) for h in heads]
        for h in heads:
            alpha = jnp.exp2(m_prev[h] - m_new[h])
            l_sc[h] = alpha * l_sc[h] + jnp.sum(p[h], 0, keepdims=True)
            acc_sc[h] = alpha * acc_sc[h] + jnp.dot(vt_ref[0, h * MLA_V:(h + 1) * MLA_V, keys],
                                                    p[h].astype(BF16), preferred_element_type=F32)
            m_sc[h] = m_new[h]

    def chunk_pair(j, carry):
        chunk(pl.multiple_of(j * (2 * FLASH_KEYS), 2 * FLASH_KEYS), 2 * FLASH_KEYS, None)
        return carry

    @pl.when(n_full >= 2)
    def _():
        chunk(0, 2 * FLASH_KEYS, "front")

    lax.fori_loop(1, n_full // 2, chunk_pair, 0)

    @pl.when(n_full % 2 == 1)
    def _():
        chunk(pl.multiple_of(full_end - FLASH_KEYS, FLASH_KEYS), FLASH_KEYS, "front")

    for offset in range(0, FLASH_KEYS, math.gcd(FLASH_KEYS, tq)):
        @pl.when(i * tq - full_end == offset)
        def _(width=offset + tq):
            chunk(pl.multiple_of(full_end, LANE), width, "causal")

    for h in heads:
        o_ref[0, :, h * MLA_V:(h + 1) * MLA_V] = (acc_sc[h] / l_sc[h]).T.astype(o_ref.dtype)


def _flash(q, k, vt, n_end):
    b, lp, _ = q.shape
    assert SEQ_TILE % LANE == 0 and FLASH_KEYS % LANE == 0 and SEQ_TILE % CHUNK == 0 and (FIRST + N_META) % CHUNK == 0
    assert FIRST + N_META <= min(SEQ_TILE, FLASH_KEYS)
    return pl.pallas_call(
        functools.partial(_flash_kernel, n_end=n_end),
        grid=(b, lp // SEQ_TILE),
        in_specs=[pl.BlockSpec((1, SEQ_TILE, MLA_HEADS * MLA_QK_PAD), lambda bi, i: (bi, i, 0)),
                  pl.BlockSpec((1, lp, MLA_HEADS * MLA_QK_PAD), lambda bi, i: (bi, 0, 0)),
                  pl.BlockSpec((1, MLA_WIDTH, lp), lambda bi, i: (bi, 0, 0))],
        out_specs=pl.BlockSpec((1, SEQ_TILE, MLA_WIDTH), lambda bi, i: (bi, i, 0)),
        out_shape=jax.ShapeDtypeStruct((b, lp, MLA_WIDTH), BF16),
        scratch_shapes=[pltpu.VMEM((MLA_HEADS, 1, SEQ_TILE), F32), pltpu.VMEM((MLA_HEADS, 1, SEQ_TILE), F32),
                        pltpu.VMEM((MLA_HEADS, MLA_V, SEQ_TILE), F32)],
        compiler_params=_cparams(("parallel", "arbitrary")),
        name="mla_flash",
    )(q, k, vt)


def _rwkv_prep_kernel(*refs, has_vres):
    (p_ref, prev_ref, mu_ref, w0_ref, w2_ref, a0_ref, a2_ref, g2_ref, kk_ref, ka_ref, rk_ref,
     bd_ref) = refs[:12]
    n_in = 16 if has_vres else 12
    r_out, k_out, v_out, kk_out, bb_out, lw_out, bonus_out, g_out = refs[n_in:]
    c = RWKV_WIDTH
    p = p_ref[0].astype(F32)
    tm = p.shape[0]
    row = lax.broadcasted_iota(jnp.int32, (tm, 1), 0)
    pos = pl.program_id(1) * tm + row
    prev = jnp.where(row == 0, prev_ref[0, PREV_ROWS - 1:PREV_ROWS, :].astype(F32), pltpu.roll(p, 1, 0))
    prev = jnp.where(pos <= FIRST, 0.0, prev)
    used = pos >= FIRST
    ps = p + (prev - p) * mu_ref[...]
    r = ps[:, :c]
    k = ps[:, c:2 * c]
    v = ps[:, 2 * c:3 * c]
    xwa = ps[:, 3 * c:3 * c + W_LORA + A_LORA]
    xg = ps[:, 3 * c + W_LORA + A_LORA:]
    zw = w0_ref[...] + jnp.dot(jnp.tanh(xwa).astype(BF16), w2_ref[...], preferred_element_type=F32)
    nz = -zw
    softplus = jnp.maximum(nz, 0.0) + jnp.log(1.0 + jnp.exp(-jnp.abs(nz)))
    log_decay = -jnp.exp(-softplus - 0.5)
    if has_vres:
        vf_ref, v0_ref, v1_ref, v2_ref = refs[12:16]
        lo = jnp.dot(v.astype(BF16), v1_ref[...], preferred_element_type=F32)
        gate = _sigmoid(v0_ref[...] + jnp.dot(lo.astype(BF16), v2_ref[...], preferred_element_type=F32))
        v = v + (vf_ref[0].astype(F32) - v) * gate
    a = _sigmoid(a0_ref[...] + jnp.dot(xwa.astype(BF16), a2_ref[...], preferred_element_type=F32))
    g = jnp.dot(_sigmoid(xg).astype(BF16), g2_ref[...], preferred_element_type=F32)
    bd = bd_ref[...]
    kk = k * kk_ref[...]
    kk = kk * lax.rsqrt(jnp.maximum(_head_sum(kk * kk, bd), 1e-24))
    k = k * (1.0 + (a - 1.0) * ka_ref[...])
    bonus = _head_sum(r * k * rk_ref[...], bd) * v
    r_out[0] = r.astype(r_out.dtype)
    k_out[0] = jnp.where(used, k, 0.0).astype(k_out.dtype)
    v_out[0] = jnp.where(used, v, 0.0).astype(v_out.dtype)
    kk_out[0] = jnp.where(used, kk, 0.0).astype(kk_out.dtype)
    bb_out[0] = jnp.where(used, kk * a, 0.0).astype(bb_out.dtype)
    lw_out[0] = log_decay
    bonus_out[0] = bonus.astype(bonus_out.dtype)
    g_out[0] = g.astype(g_out.dtype)


def _rwkv_prep(pr, prm, v_first):
    b, lp, cols = pr.shape
    has_vres = v_first is not None
    c = RWKV_WIDTH
    row = lambda bi, i: (bi, i, 0)
    prev = lambda bi, i: (bi, jnp.maximum(i * (SEQ_TILE // PREV_ROWS) - 1, 0), 0)
    names = ["mu", "w0", "w2", "a0", "a2", "g2", "k_k", "k_a", "r_k", "bd"]
    args = [pr, pr] + [prm[n] for n in names]
    in_specs = [pl.BlockSpec((1, SEQ_TILE, cols), row), pl.BlockSpec((1, PREV_ROWS, cols), prev)]
    in_specs += [_const_spec(prm[n].shape) for n in names]
    if has_vres:
        args += [v_first, prm["v0"], prm["v1"], prm["v2"]]
        in_specs += [pl.BlockSpec((1, SEQ_TILE, c), row)]
        in_specs += [_const_spec(prm[n].shape) for n in ("v0", "v1", "v2")]
    out_spec = pl.BlockSpec((1, SEQ_TILE, c), row)
    return pl.pallas_call(
        functools.partial(_rwkv_prep_kernel, has_vres=has_vres),
        grid=(b, lp // SEQ_TILE),
        in_specs=in_specs,
        out_specs=[out_spec] * 8,
        out_shape=[jax.ShapeDtypeStruct((b, lp, c), F32 if n == 5 else BF16) for n in range(8)],
        compiler_params=_cparams(("parallel", "parallel")),
        name="rwkv_prep",
    )(*args)


_NN = (((1,), (0,)), ((), ()))
_NT = (((1,), (1,)), ((), ()))


def _mm(a, b, dims):
    return lax.dot_general(a.astype(BF16), b.astype(BF16), dims, preferred_element_type=F32)


def _rwkv_scan_kernel(r_ref, k_ref, v_ref, kk_ref, bb_ref, lw_ref, y_ref, s_sc):
    @pl.when(pl.program_id(1) == 0)
    def _():
        s_sc[...] = jnp.zeros(s_sc.shape, F32)

    c = SCAN_CHUNK
    n = 2 * c
    n_chunks = r_ref.shape[1] // c
    ri = lax.broadcasted_iota(jnp.int32, (c, c), 0)
    ci = lax.broadcasted_iota(jnp.int32, (c, c), 1)
    tri = jnp.where(ri >= ci, 1.0, 0.0).astype(BF16)
    row = lax.broadcasted_iota(jnp.int32, (2 * n, 2 * n), 0)
    col = lax.broadcasted_iota(jnp.int32, (2 * n, 2 * n), 1)
    t_idx = row & (c - 1)
    s_idx = col & (c - 1)
    causal = (t_idx > s_idx) | ((t_idx == s_idx) & (row >= n))
    eye = jnp.where(lax.broadcasted_iota(jnp.int32, (n, n), 0) == lax.broadcasted_iota(jnp.int32, (n, n), 1),
                    1.0, 0.0).astype(F32)
    head0 = lax.broadcasted_iota(jnp.int32, (1, LANE), 1) < RWKV_HEAD

    def stack(x):
        return jnp.concatenate([jnp.where(head0, x, 0.0), jnp.where(head0, 0.0, x)], axis=0)

    ar, bk, v_t, w_end = [], [], [], []
    for ch in range(n_chunks):
        rows = slice(ch * c, (ch + 1) * c)
        lw = lw_ref[0, rows, :]
        lw_hi = lw.astype(BF16)
        lw_lo = (lw - lw_hi.astype(F32)).astype(BF16)
        cum = (jnp.dot(tri, lw_hi, preferred_element_type=F32)
               + jnp.dot(tri, lw_lo, preferred_element_type=F32))
        w_incl = jnp.exp(cum)
        w_inv = jnp.exp(-cum)
        a_hat = -kk_ref[0, rows, :].astype(F32) * jnp.exp(cum - lw)
        r_hat = r_ref[0, rows, :].astype(F32) * w_incl
        b_hat = bb_ref[0, rows, :].astype(F32) * w_inv
        k_hat = k_ref[0, rows, :].astype(F32) * w_inv
        v = v_ref[0, rows, :].astype(F32)
        for pr in range(RWKV_PAIRS):
            sl = slice(pr * LANE, (pr + 1) * LANE)
            ar.append(jnp.concatenate([stack(a_hat[:, sl]), stack(r_hat[:, sl])], 0).astype(BF16))
            bk.append(jnp.concatenate([stack(b_hat[:, sl]), stack(k_hat[:, sl])], 0).astype(BF16))
            v_t.append(stack(v[:, sl]).T.astype(BF16))
            w_end.append(w_incl[c - 1:c, sl])
    every = range(len(ar))
    gram = [jnp.where(causal, _mm(ar[g], bk[g], _NT), 0.0) for g in every]
    pw = [gram[g][:n, :n] for g in every]
    t_inv = [eye + pw[g] for g in every]
    pw = [_mm(pw[g], pw[g], _NN) for g in every]
    for _ in range(4):
        both = [_mm(pw[g], jnp.concatenate([pw[g], t_inv[g]], 1), _NN) for g in every]
        pw = [both[g][:, :n] for g in every]
        t_inv = [t_inv[g] + both[g][:, n:] for g in every]
    t_inv = [(t_inv[g] + _mm(pw[g], t_inv[g], _NN)).astype(BF16) for g in every]
    va = [_mm(v_t[g], gram[g][:n, n:], _NT) for g in every]
    q_bk = [gram[g][n:, :].astype(BF16) for g in every]

    state = [s_sc[pr] for pr in range(RWKV_PAIRS)]
    for ch in range(n_chunks):
        gs = [ch * RWKV_PAIRS + pr for pr in range(RWKV_PAIRS)]
        s_ar = [_mm(state[pr], ar[g], _NT) for pr, g in enumerate(gs)]
        u_t = [_mm(s_ar[pr][:, :n] + va[g], t_inv[g], _NT) for pr, g in enumerate(gs)]
        uv = [jnp.concatenate([u_t[pr].astype(BF16), v_t[g]], 1) for pr, g in enumerate(gs)]
        state = [(state[pr] + _mm(uv[pr], bk[g], _NN)) * w_end[g] for pr, g in enumerate(gs)]
        for pr, g in enumerate(gs):
            y = (s_ar[pr][:, n:] + _mm(uv[pr], q_bk[g], _NT)).T
            y_ref[0, ch * c:(ch + 1) * c, pr * LANE:(pr + 1) * LANE] = y[:c] + y[c:]
    for pr in range(RWKV_PAIRS):
        s_sc[pr] = state[pr]


def _rwkv_scan(r, k, v, kk, bb, lw):
    b, lp, c = r.shape
    rows = SCAN_STEP_CHUNKS * SCAN_CHUNK
    spec = pl.BlockSpec((1, rows, c), lambda bi, i: (bi, i, 0))
    return pl.pallas_call(
        _rwkv_scan_kernel,
        grid=(b, lp // rows),
        in_specs=[spec] * 6,
        out_specs=spec,
        out_shape=jax.ShapeDtypeStruct((b, lp, c), F32),
        scratch_shapes=[pltpu.VMEM((RWKV_PAIRS, LANE, LANE), F32)],
        compiler_params=_cparams(("parallel", "arbitrary")),
        name="rwkv_scan",
    )(r, k, v, kk, bb, lw)


def _outproj_kernel(ys_ref, bonus_ref, g_ref, ymla_ref, lnw_ref, lnb_ref, bd_ref, wo_ref, h_ref, o_ref):
    y = ys_ref[...]
    bd = bd_ref[...]
    inv_n = 1.0 / RWKV_HEAD
    d = y - _head_sum(y, bd) * inv_n
    var = _head_sum(d * d, bd) * inv_n
    yn = d * lax.rsqrt(var + GN_EPS) * lnw_ref[...] + lnb_ref[...]
    yr = ((yn + bonus_ref[...].astype(F32)) * g_ref[...].astype(F32)).astype(BF16)
    o_ref[...] = (h_ref[...]
                  + jnp.dot(ymla_ref[...], wo_ref[:MLA_WIDTH, :], preferred_element_type=F32)
                  + jnp.dot(yr, wo_ref[MLA_WIDTH:, :], preferred_element_type=F32))


def _outproj(ys, bonus, g, ymla, ln_w, ln_b, bd, wo, h):
    tp = h.shape[0]
    c = RWKV_WIDTH
    rc = pl.BlockSpec((ROW_TILE, c), lambda i: (i, 0))
    rd = pl.BlockSpec((ROW_TILE, D_MODEL), lambda i: (i, 0))
    return pl.pallas_call(
        _outproj_kernel,
        grid=(tp // ROW_TILE,),
        in_specs=[rc, rc, rc, rc, _const_spec((1, c)), _const_spec((1, c)), _const_spec(bd.shape),
                  _const_spec(wo.shape), rd],
        out_specs=rd,
        out_shape=jax.ShapeDtypeStruct((tp, D_MODEL), F32),
        compiler_params=_cparams(("parallel",)),
        name="outproj",
    )(ys, bonus, g, ymla, ln_w, ln_b, bd, wo, h)


def _ffn_chunks(u, wg_ref, wu_ref, wd_ref, idx, ff, chunk):
    acc = None
    for c0 in range(0, ff, chunk):
        sl = slice(c0, c0 + chunk)
        gate = jnp.dot(u, wg_ref[idx + (slice(None), sl)], preferred_element_type=F32)
        up = jnp.dot(u, wu_ref[idx + (slice(None), sl)], preferred_element_type=F32)
        act = (gate * _sigmoid(gate) * up).astype(BF16)
        part = jnp.dot(act, wd_ref[idx + (sl, slice(None))], preferred_element_type=F32)
        acc = part if acc is None else acc + part
    return acc


def _dense_ffn_kernel(h_ref, g_ref, wg_ref, wu_ref, wd_ref, o_ref, *, ff, chunk):
    h = h_ref[...]
    u = _rms(h, g_ref[...]).astype(BF16)
    o_ref[...] = h + _ffn_chunks(u, wg_ref, wu_ref, wd_ref, (), ff, chunk)


def _dense_ffn(h, g, wg, wu, wd):
    tp = h.shape[0]
    ff = wg.shape[1]
    rd = pl.BlockSpec((ROW_TILE, D_MODEL), lambda i: (i, 0))
    once = lambda shape: pl.BlockSpec(shape, lambda i: (0, 0), pipeline_mode=pl.Buffered(1))
    return pl.pallas_call(
        functools.partial(_dense_ffn_kernel, ff=ff, chunk=256),
        grid=(tp // ROW_TILE,),
        in_specs=[rd, _const_spec((1, D_MODEL)), once(wg.shape), once(wu.shape), once(wd.shape)],
        out_specs=rd,
        out_shape=jax.ShapeDtypeStruct((tp, D_MODEL), F32),
        compiler_params=_cparams(("parallel",)),
        name="dense_ffn",
    )(h, g, wg, wu, wd)


def _router_kernel(h_ref, g_ref, wr_ref, br_ref, u_ref, meta_ref, meta_t_ref, cnt_ref, carry_sc, *, n_end):
    @pl.when((pl.program_id(0) == 0) & (pl.program_id(1) == 0))
    def _():
        carry_sc[...] = jnp.zeros(carry_sc.shape, F32)

    tm = h_ref.shape[1]
    u = _rms(h_ref[0], g_ref[...])
    _to_row_tiles(u_ref, u)
    u_hi, u_lo = _split(u)
    w_hi, w_lo = _split(wr_ref[...])
    logits = (jnp.dot(u_hi, w_hi, preferred_element_type=F32) + jnp.dot(u_hi, w_lo, preferred_element_type=F32)
              + jnp.dot(u_lo, w_hi, preferred_element_type=F32) + br_ref[...])
    lane = lax.broadcasted_iota(jnp.int32, (tm, LANE), 1).astype(F32)
    logits = jnp.where(lane < N_EXPERTS, logits, -jnp.inf)
    top0 = jnp.max(logits, -1, keepdims=True)
    e0 = jnp.min(jnp.where(logits == top0, lane, float(LANE)), -1, keepdims=True)
    rest = jnp.where(lane == e0, -jnp.inf, logits)
    top1 = jnp.max(rest, -1, keepdims=True)
    e1 = jnp.min(jnp.where(rest == top1, lane, float(LANE)), -1, keepdims=True)
    ex = jnp.exp(top1 - top0)
    g0 = 1.0 / (1.0 + ex)
    g1 = ex / (1.0 + ex)
    pos = pl.program_id(1) * tm + lax.broadcasted_iota(jnp.int32, (tm, 1), 0)
    valid = (pos >= FIRST) & (pos < n_end)
    oh0 = jnp.where((lane == e0) & valid, 1.0, 0.0)
    oh1 = jnp.where((lane == e1) & valid, 1.0, 0.0)
    ri = lax.broadcasted_iota(jnp.int32, (tm, tm), 0)
    ci = lax.broadcasted_iota(jnp.int32, (tm, tm), 1)
    before = jnp.where(ri > ci, 1.0, 0.0).astype(BF16)
    both = oh0 + oh1
    seen = carry_sc[...] + jnp.dot(before, both.astype(BF16), preferred_element_type=F32)
    rank0 = jnp.sum(jnp.where(lane == e0, seen, 0.0), -1, keepdims=True)
    rank1 = jnp.sum(jnp.where(lane == e1, seen, 0.0), -1, keepdims=True)
    carry_sc[...] = carry_sc[...] + jnp.sum(both, 0, keepdims=True)
    cnt_ref[...] = jnp.broadcast_to(carry_sc[...], cnt_ref.shape)
    vf = jnp.where(valid, 1.0, 0.0)
    meta = jnp.where(lane == 0, e0, 0.0)
    meta = jnp.where(lane == 1, e1, meta)
    meta = jnp.where(lane == 2, rank0, meta)
    meta = jnp.where(lane == 3, rank1, meta)
    meta = jnp.where(lane == 4, g0 * vf, meta)
    meta = jnp.where(lane == 5, g1 * vf, meta)
    meta = jnp.where(lane == 6, vf, meta)
    meta_ref[0] = meta
    meta_t_ref[...] = meta.T[:8]


def _router(h, g, wr, br, n_end):
    b, lp, _ = h.shape
    row = lambda bi, i: (bi, i, 0)
    return pl.pallas_call(
        functools.partial(_router_kernel, n_end=n_end),
        grid=(b, lp // SEQ_TILE),
        in_specs=[pl.BlockSpec((1, SEQ_TILE, D_MODEL), row), _const_spec((1, D_MODEL)),
                  _const_spec(wr.shape), _const_spec(br.shape)],
        out_specs=[pl.BlockSpec((SEQ_TILE * ROW_TILES, LANE), lambda bi, i: (bi * (lp // SEQ_TILE) + i, 0)),
                   pl.BlockSpec((1, SEQ_TILE, LANE), row),
                   pl.BlockSpec((8, SEQ_TILE), lambda bi, i: (0, bi * (lp // SEQ_TILE) + i)),
                   _const_spec((8, LANE))],
        out_shape=[jax.ShapeDtypeStruct((b * lp * ROW_TILES, LANE), F32), jax.ShapeDtypeStruct((b, lp, LANE), F32),
                   jax.ShapeDtypeStruct((8, b * lp), F32), jax.ShapeDtypeStruct((8, LANE), F32)],
        scratch_shapes=[pltpu.VMEM((1, LANE), F32)],
        compiler_params=_cparams(("arbitrary", "arbitrary")),
        name="moe_router",
    )(h, g, wr, br)


def _dispatch_kernel(zb_ref, s0_ref, s1_ref, u_ref, xb_ref, zero_sc, sem):
    tm = u_ref.shape[0] // ROW_TILES
    block_rows = MOE_BLOCK * ROW_TILES

    @pl.when(pl.program_id(0) == 0)
    def _():
        zero_sc[...] = jnp.zeros(zero_sc.shape, F32)

        def fill(j):
            start = pl.multiple_of(zb_ref[j] * block_rows, block_rows)
            return pltpu.make_async_copy(zero_sc, xb_ref.at[pl.ds(start, block_rows), :], sem)

        for j in range(zb_ref.shape[0]):
            @pl.when(zb_ref[j] >= 0)
            def _(j=j):
                fill(j).start()
        for j in range(zb_ref.shape[0]):
            @pl.when(zb_ref[j] >= 0)
            def _(j=j):
                fill(j).wait()

    def copies(rw):
        return [pltpu.make_async_copy(_row_tile(u_ref, rw), _row_tile(xb_ref, s_ref[0, 0, rw]), sem)
                for s_ref in (s0_ref, s1_ref)]

    def start(j, carry):
        for kq in range(DMA_UNROLL):
            for thread, cp in enumerate(copies(j * DMA_UNROLL + kq)):
                cp.start(priority=thread)
        return carry

    lax.fori_loop(0, tm // DMA_UNROLL, start, 0)
    for _ in (s0_ref, s1_ref):
        pltpu.make_async_copy(u_ref, xb_ref.at[pl.ds(0, tm * ROW_TILES), :], sem).wait()


def _dispatch(zero_blocks, slot0, slot1, u, n_rows, tm):
    tp = u.shape[0] // ROW_TILES
    sspec = pl.BlockSpec((1, 1, tm), lambda i, zb: (i, 0, 0), memory_space=pltpu.SMEM)
    return pl.pallas_call(
        _dispatch_kernel,
        grid_spec=pltpu.PrefetchScalarGridSpec(
            num_scalar_prefetch=1,
            grid=(tp // tm,),
            in_specs=[sspec, sspec, pl.BlockSpec((tm * ROW_TILES, LANE), lambda i, zb: (i, 0))],
            out_specs=pl.BlockSpec(memory_space=pl.ANY),
            scratch_shapes=[pltpu.VMEM((MOE_BLOCK * ROW_TILES, LANE), F32), pltpu.SemaphoreType.DMA(())]),
        out_shape=jax.ShapeDtypeStruct((n_rows * ROW_TILES, LANE), F32),
        compiler_params=_cparams(("arbitrary",)),
        name="moe_dispatch",
    )(zero_blocks, slot0.reshape(tp // tm, 1, tm), slot1.reshape(tp // tm, 1, tm), u)


def _expert_kernel(be_ref, fill_ref, x_ref, wg_ref, wu_ref, wd_ref, o_ref, *, ff, chunk):
    del be_ref
    fill = fill_ref[pl.program_id(0)]

    @pl.when(fill > 0)
    def _():
        x = _from_row_tiles(x_ref, MOE_BLOCK).astype(BF16)
        _to_row_tiles(o_ref, _ffn_chunks(x, wg_ref, wu_ref, wd_ref, (0,), ff, chunk))

    @pl.when(fill <= 0)
    def _():
        o_ref[...] = jnp.zeros(o_ref.shape, F32)


def _experts(block_e, block_fill, xb, wg, wu, wd):
    n_slots = block_e.shape[0] * MOE_BLOCK
    ff = wg.shape[2]
    rows = pl.BlockSpec((MOE_BLOCK * ROW_TILES, LANE), lambda i, be, fill: (i, 0))
    return pl.pallas_call(
        functools.partial(_expert_kernel, ff=ff, chunk=ff // 2),
        grid_spec=pltpu.PrefetchScalarGridSpec(
            num_scalar_prefetch=2,
            grid=(n_slots // MOE_BLOCK,),
            in_specs=[rows,
                      pl.BlockSpec((1, D_MODEL, ff), lambda i, be, fill: (be[i], 0, 0)),
                      pl.BlockSpec((1, D_MODEL, ff), lambda i, be, fill: (be[i], 0, 0)),
                      pl.BlockSpec((1, ff, D_MODEL), lambda i, be, fill: (be[i], 0, 0))],
            out_specs=rows),
        out_shape=jax.ShapeDtypeStruct((n_slots * ROW_TILES, LANE), F32),
        compiler_params=_cparams(("arbitrary",), 60 * 1024 * 1024),
        name="moe_experts",
    )(block_e, block_fill, xb, wg, wu, wd)


def _combine_kernel(s0_ref, s1_ref, n0_ref, n1_ref, h_ref, gates_ref, fn_ref, yb_ref, o_ref, y0_sc, y1_sc, sems, *,
                    per_row, n_frame_tiles):
    t = pl.program_id(0)
    tm = h_ref.shape[0]
    cur = t % 2

    def copies(rw, a_ref, b_ref, buf):
        return [pltpu.make_async_copy(_row_tile(yb_ref, s_ref[0, 0, rw]), _row_tile(y_sc.at[buf], rw), sems.at[buf])
                for s_ref, y_sc in ((a_ref, y0_sc), (b_ref, y1_sc))]

    def issue(a_ref, b_ref, buf):
        def body(j, carry):
            for kq in range(DMA_UNROLL):
                for thread, cp in enumerate(copies(j * DMA_UNROLL + kq, a_ref, b_ref, buf)):
                    cp.start(priority=thread)
            return carry
        lax.fori_loop(0, tm // DMA_UNROLL, body, 0)

    @pl.when(t == 0)
    def _():
        issue(s0_ref, s1_ref, 0)

    @pl.when(t + 1 < pl.num_programs(0))
    def _():
        issue(n0_ref, n1_ref, 1 - cur)

    for y_sc in (y0_sc, y1_sc):
        pltpu.make_async_copy(yb_ref.at[pl.ds(0, tm * ROW_TILES), :], y_sc.at[cur], sems.at[cur]).wait()
    tile = t % per_row

    @pl.when((tile >= 1) & (tile <= n_frame_tiles))
    def _():
        gates = gates_ref[...]
        f = (_from_row_tiles(y0_sc.at[cur], tm) * gates[:, 4:5] + _from_row_tiles(y1_sc.at[cur], tm) * gates[:, 5:6])
        o_ref[0] = _rms(h_ref[...] + f, fn_ref[...])


def _combine(slot0, slot1, h, meta, fn, yb, tm, b, seq):
    tp = h.shape[0]
    n_tiles = tp // tm
    per_row = n_tiles // b
    assert (FIRST + N_META) == tm and seq % tm == 0
    sspec = pl.BlockSpec((1, 1, tm), lambda i: (i, 0, 0), memory_space=pltpu.SMEM)
    nspec = pl.BlockSpec((1, 1, tm), lambda i: (jnp.minimum(i + 1, n_tiles - 1), 0, 0), memory_space=pltpu.SMEM)
    rd = pl.BlockSpec((tm, D_MODEL), lambda i: (i, 0))
    s0 = slot0.reshape(n_tiles, 1, tm)
    s1 = slot1.reshape(n_tiles, 1, tm)
    return pl.pallas_call(
        functools.partial(_combine_kernel, per_row=per_row, n_frame_tiles=seq // tm),
        grid=(n_tiles,),
        in_specs=[sspec, sspec, nspec, nspec, rd, pl.BlockSpec((tm, LANE), lambda i: (i, 0)),
                  _const_spec((1, D_MODEL)), pl.BlockSpec(memory_space=pl.ANY)],
        out_specs=pl.BlockSpec((1, tm, D_MODEL), lambda i: (i // per_row, jnp.clip(i % per_row - 1, 0, seq // tm - 1), 0)),
        out_shape=jax.ShapeDtypeStruct((b, seq, D_MODEL), F32),
        scratch_shapes=[pltpu.VMEM((2, tm * ROW_TILES, LANE), F32), pltpu.VMEM((2, tm * ROW_TILES, LANE), F32),
                        pltpu.SemaphoreType.DMA((2,))],
        compiler_params=_cparams(("arbitrary",)),
        name="moe_combine",
    )(s0, s1, s0, s1, h, meta, fn, yb)


def _final_norm_kernel(h_ref, g_ref, o_ref):
    o_ref[...] = _rms(h_ref[...], g_ref[...])


def _final_norm(h, g):
    tp = h.shape[0]
    rd = pl.BlockSpec((ROW_TILE, D_MODEL), lambda i: (i, 0))
    return pl.pallas_call(
        _final_norm_kernel,
        grid=(tp // ROW_TILE,),
        in_specs=[rd, _const_spec((1, D_MODEL))],
        out_specs=rd,
        out_shape=jax.ShapeDtypeStruct((tp, D_MODEL), F32),
        compiler_params=_cparams(("parallel",)),
        name="final_norm",
    )(h, g)


def _rot_cols(w):
    half = MLA_ROPE // 2
    return jnp.concatenate([-w[..., half:], w[..., :half]], -1)


def _prep_inproj(w_in):
    m_q = w_in[:, :MLA_Q_LORA + MLA_KV_LORA]
    k_rope = w_in[:, MLA_Q_LORA + MLA_KV_LORA:MLA_Q_LORA + MLA_KV_LORA + MLA_ROPE]
    w_m = jnp.concatenate([m_q, k_rope, _rot_cols(k_rope)], -1)
    w_r = w_in[:, MLA_Q_LORA + MLA_KV_LORA + MLA_ROPE:]
    return w_r.astype(BF16), w_m.astype(BF16)


def _prep_wq(w_uq):
    w = w_uq.reshape(MLA_Q_LORA, MLA_HEADS, MLA_NOPE + MLA_ROPE)
    rope = w[..., MLA_NOPE:]
    return jnp.concatenate([w, _rot_cols(rope)], -1).reshape(MLA_Q_LORA, MLA_HEADS * MLA_QK_PAD).astype(BF16)


def _prep_wkv(w_ukv):
    w = w_ukv.reshape(MLA_KV_LORA, MLA_HEADS, MLA_NOPE + MLA_V)
    k_nope = w[..., :MLA_NOPE].reshape(MLA_KV_LORA, MLA_HEADS * MLA_NOPE)
    v = w[..., MLA_NOPE:].reshape(MLA_KV_LORA, MLA_HEADS * MLA_V)
    return jnp.concatenate([k_nope, v], -1).astype(BF16)


def _row(x):
    return x.reshape(1, -1).astype(F32)


def _block_diag_ones():
    idx = np.arange(RWKV_WIDTH) // RWKV_HEAD
    return jnp.asarray(idx[:, None] == idx[None, :], BF16)


def kernel(x, meta_tokens, attn_norm, ffn_norm, final_norm, w_in, w_out, mla_q_norm, mla_kv_norm, mla_w_uq, mla_w_ukv, rwkv_mu, rwkv_w0, rwkv_w2, rwkv_a0, rwkv_a2, rwkv_g2, rwkv_k_k, rwkv_k_a, rwkv_r_k, rwkv_ln_w, rwkv_ln_b, rwkv_v0, rwkv_v1, rwkv_v2, ffn_w_gate, ffn_w_up, ffn_w_down, moe_router, moe_router_bias, moe_w_gate, moe_w_up, moe_w_down):
    b, seq, d = x.shape
    depth = attn_norm.shape[0]
    n_end = FIRST + N_META + seq
    lp = -(-n_end // SEQ_TILE) * SEQ_TILE
    tp = b * lp
    assert d == D_MODEL and tp % ROW_TILE == 0 and lp % SCAN_CHUNK == 0

    tabs = _rope_tables(lp)
    bd = _block_diag_ones()
    zeros_wa = jnp.zeros((W_LORA, RWKV_WIDTH), F32)
    v_first = None
    for i in range(depth):
        w_r, w_m = _prep_inproj(w_in[i])
        if i == 0:
            h, pr, pm = _embed_inproj(x, meta_tokens, lp, _row(attn_norm[i]), w_r, w_m)
            h = h.reshape(tp, d)
        else:
            pr, pm = _norm_inproj(h, _row(attn_norm[i]), w_r, w_m)

        q, k, v = _mla_up(pm.reshape(b, lp, -1), _row(mla_q_norm[i]), _row(mla_kv_norm[i]),
                          _prep_wq(mla_w_uq[i]), _prep_wkv(mla_w_ukv[i]), tabs)
        y_mla = _flash(q, k, v, n_end)

        prm = {
            "mu": _row(rwkv_mu[i]), "w0": _row(rwkv_w0[i]), "a0": _row(rwkv_a0[i]),
            "w2": jnp.concatenate([rwkv_w2[i], zeros_wa], 0).astype(BF16),
            "a2": jnp.concatenate([zeros_wa, rwkv_a2[i]], 0).astype(BF16),
            "g2": rwkv_g2[i].astype(BF16),
            "k_k": _row(rwkv_k_k[i]), "k_a": _row(rwkv_k_a[i]), "r_k": _row(rwkv_r_k[i]), "bd": bd,
        }
        if i > 0:
            prm["v0"] = _row(rwkv_v0[i - 1])
            prm["v1"] = jnp.pad(rwkv_v1[i - 1], ((0, 0), (0, LANE - V_LORA))).astype(BF16)
            prm["v2"] = jnp.pad(rwkv_v2[i - 1], ((0, LANE - V_LORA), (0, 0))).astype(BF16)
        r_, k_, v_, kk_, bb_, lw_, bonus, gate = _rwkv_prep(pr.reshape(b, lp, -1), prm, v_first)
        if i == 0:
            v_first = v_
        ys = _rwkv_scan(r_, k_, v_, kk_, bb_, lw_)

        flat = lambda t: t.reshape(tp, -1)
        h = _outproj(flat(ys), flat(bonus), flat(gate), flat(y_mla), _row(rwkv_ln_w[i]), _row(rwkv_ln_b[i]),
                     bd, w_out[i].astype(BF16), h)

        j = i // 2
        last = i == depth - 1
        if i % 2 == 0:
            h = _dense_ffn(h, _row(ffn_norm[i]), ffn_w_gate[j].astype(BF16), ffn_w_up[j].astype(BF16),
                           ffn_w_down[j].astype(BF16))
            if last:
                h = _final_norm(h, _row(final_norm)).reshape(b, lp, d)[:, FIRST + N_META:n_end]
        else:
            if not last:
                raise NotImplementedError("an MoE layer that is not the last layer")
            h = _moe(h, b, lp, n_end, _row(ffn_norm[i]), moe_router[j], moe_router_bias[j],
                     moe_w_gate[j], moe_w_up[j], moe_w_down[j], _row(final_norm))
    return h


def _moe(h, b, lp, n_end, g, router, router_bias, w_gate, w_up, w_down, final_g):
    tp = b * lp
    wr = jnp.pad(router, ((0, 0), (0, LANE - N_EXPERTS))).astype(F32)
    br = jnp.pad(router_bias, (0, LANE - N_EXPERTS)).reshape(1, LANE).astype(F32)
    u, meta, meta_t, cnt = _router(h.reshape(b, lp, D_MODEL), g, wr, br, n_end)
    meta = meta.reshape(tp, LANE)

    counts = cnt[0, :N_EXPERTS].astype(jnp.int32)
    padded = (counts + MOE_BLOCK - 1) // MOE_BLOCK * MOE_BLOCK
    pend = jnp.cumsum(padded)
    pstart = pend - padded
    n_assign = b * (n_end - FIRST) * TOP_K
    n_blocks = (n_assign + N_EXPERTS * (MOE_BLOCK - 1)) // MOE_BLOCK + 1
    n_slots = n_blocks * MOE_BLOCK
    block_start = jnp.arange(n_blocks, dtype=jnp.int32) * MOE_BLOCK
    block_e = jnp.minimum(jnp.sum((pend[None, :] <= block_start[:, None]).astype(jnp.int32), -1),
                          N_EXPERTS - 1)
    e0, e1, rank0, rank1 = (meta_t[n].astype(jnp.int32) for n in range(4))
    valid = meta_t[6] > 0.5
    slot0 = pstart[e0] + rank0
    slot1 = pstart[e1] + rank1
    spare = n_slots + jnp.arange(tp, dtype=jnp.int32) % LANE
    d0 = jnp.where(valid, slot0, spare)
    d1 = jnp.where(valid, slot1, spare + LANE)
    c0 = jnp.where(valid, slot0, 0)
    c1 = jnp.where(valid, slot1, 0)

    last_block = jnp.where(padded > 0, pend // MOE_BLOCK - 1, -1)
    min_used = -(-n_assign // MOE_BLOCK)
    after = jnp.arange(min_used, n_blocks + 1, dtype=jnp.int32)
    after = jnp.where(after >= pend[N_EXPERTS - 1] // MOE_BLOCK, after, -1)
    zero_blocks = jnp.concatenate([last_block, after]).astype(jnp.int32)
    assert 2 * LANE == MOE_BLOCK
    xb = _dispatch(zero_blocks, d0, d1, u, n_slots + 2 * LANE, LANE)
    block_fill = jnp.clip(pstart[block_e] + counts[block_e] - block_start, 0, MOE_BLOCK).astype(jnp.int32)
    yb = _experts(block_e, block_fill, xb, w_gate.astype(BF16), w_up.astype(BF16), w_down.astype(BF16))
    return _combine(c0, c1, h, meta, final_g, yb, LANE, b, n_end - FIRST - N_META)
```

```python
import functools
import math

import numpy as np
import jax
import jax.numpy as jnp
from jax import lax
from jax.experimental import pallas as pl
from jax.experimental.pallas import tpu as pltpu

F32 = jnp.float32
BF16 = jnp.bfloat16

D_MODEL = 1024
CHUNK = 64
N_META = 16
RMS_EPS = 1e-6

MLA_HEADS = 4
MLA_NOPE = 128
MLA_ROPE = 64
MLA_V = 128
MLA_Q_LORA = 512
MLA_KV_LORA = 256
ROPE_THETA = 10000.0
MLA_WIDTH = MLA_HEADS * MLA_V
MLA_QK_PAD = 256
Q_SCALE = float((MLA_NOPE + MLA_ROPE) ** -0.5 * np.log2(np.e))
MLA_COLS_PAD = MLA_Q_LORA + MLA_KV_LORA + 2 * MLA_ROPE

RWKV_WIDTH = 512
RWKV_HEAD = 64
RWKV_HEADS = RWKV_WIDTH // RWKV_HEAD
RWKV_PAIRS = RWKV_HEADS // 2
W_LORA = 64
A_LORA = 64
V_LORA = 32
G_LORA = 128
GN_EPS = 64e-5
RWKV_COLS = 3 * RWKV_WIDTH + W_LORA + A_LORA + G_LORA
SCAN_CHUNK = 64
SCAN_STEP_CHUNKS = 6

N_EXPERTS = 8
TOP_K = 2
MOE_BLOCK = 256

LANE = 128
FRONT_PAD = -N_META % LANE
FIRST = FRONT_PAD
SEQ_TILE = 384
FLASH_KEYS = 256
MLA_UP_TILE_MAX = 1536
ROW_TILE = 1024
PREV_ROWS = 16
ROW_TILES = D_MODEL // LANE
DMA_UNROLL = 8
VMEM_LIMIT = 56 * 1024 * 1024
EXPERT_VMEM_LIMIT = 60 * 1024 * 1024
F_E0, F_E1, F_RANK0, F_RANK1, F_GATE0, F_GATE1, F_VALID, N_FIELDS = range(8)


def _cparams(sem, vmem=VMEM_LIMIT):
    return pltpu.CompilerParams(dimension_semantics=sem, vmem_limit_bytes=vmem)


def _rms(x, g):
    return x * lax.rsqrt(jnp.mean(x * x, -1, keepdims=True) + RMS_EPS) * g


def _sigmoid(x):
    return 1.0 / (1.0 + jnp.exp(-x))


def _split(x):
    hi = x.astype(BF16)
    return hi, (x - hi.astype(F32)).astype(BF16)


def _head_sum(x, bd):
    return jnp.dot(x.astype(BF16), bd, preferred_element_type=F32)


def _to_row_tiles(ref, x):
    for s in range(ROW_TILES):
        ref[pl.ds(s, x.shape[0], stride=ROW_TILES), :] = x[:, s * LANE:(s + 1) * LANE]


def _from_row_tiles(ref, rows):
    return jnp.concatenate([ref[pl.ds(s, rows, stride=ROW_TILES), :] for s in range(ROW_TILES)], axis=1)


def _row_tile(ref, r):
    return ref.at[pl.ds(pl.multiple_of(r * ROW_TILES, ROW_TILES), ROW_TILES), :]


def _const_spec(shape):
    nd = len(shape)
    return pl.BlockSpec(shape, lambda *_: (0,) * nd)


def _norm_inproj_kernel(h_ref, g_ref, wr_ref, wm_ref, pr_ref, pm_ref):
    u = _rms(h_ref[...], g_ref[...]).astype(BF16)
    pr_ref[...] = jnp.dot(u, wr_ref[...], preferred_element_type=F32).astype(pr_ref.dtype)
    pm_ref[...] = jnp.dot(u, wm_ref[...], preferred_element_type=F32).astype(pm_ref.dtype)


def _norm_inproj(h, g, w_r, w_m):
    tp = h.shape[0]
    return pl.pallas_call(
        _norm_inproj_kernel,
        grid=(tp // ROW_TILE,),
        in_specs=[pl.BlockSpec((ROW_TILE, D_MODEL), lambda i: (i, 0)),
                  _const_spec((1, D_MODEL)),
                  _const_spec(w_r.shape), _const_spec(w_m.shape)],
        out_specs=[pl.BlockSpec((ROW_TILE, w_r.shape[1]), lambda i: (i, 0)),
                   pl.BlockSpec((ROW_TILE, w_m.shape[1]), lambda i: (i, 0))],
        out_shape=[jax.ShapeDtypeStruct((tp, w_r.shape[1]), BF16),
                   jax.ShapeDtypeStruct((tp, w_m.shape[1]), BF16)],
        compiler_params=_cparams(("parallel",)),
        name="norm_inproj",
    )(h, g, w_r, w_m)


def _embed_inproj_kernel(*refs, n_end):
    n_sub = SEQ_TILE // LANE
    x_refs = refs[:n_sub]
    meta_ref, g_ref, wr_ref, wm_ref, h_ref, pr_ref, pm_ref = refs[n_sub:]
    i = pl.program_id(1)
    front = jnp.concatenate([jnp.zeros((FRONT_PAD, D_MODEL), F32), meta_ref[...]], 0)
    parts = [jnp.where(i == 0, front, x_refs[0][0])] + [r[0] for r in x_refs[1:]]
    h = jnp.concatenate(parts, 0)
    pos = i * SEQ_TILE + lax.broadcasted_iota(jnp.int32, (SEQ_TILE, 1), 0)
    h = jnp.where(pos < n_end, h, 0.0)
    h_ref[0] = h
    u = _rms(h, g_ref[...]).astype(BF16)
    pr_ref[0] = jnp.dot(u, wr_ref[...], preferred_element_type=F32).astype(pr_ref.dtype)
    pm_ref[0] = jnp.dot(u, wm_ref[...], preferred_element_type=F32).astype(pm_ref.dtype)


def _embed_inproj(x, meta_tokens, lp, g, w_r, w_m):
    b, seq, _ = x.shape
    n_sub = SEQ_TILE // LANE
    assert FRONT_PAD + N_META == LANE and seq % LANE == 0
    last = seq // LANE - 1
    x_specs = [pl.BlockSpec((1, LANE, D_MODEL), functools.partial(
        lambda bi, i, k: (bi, jnp.clip(i * n_sub + k - 1, 0, last), 0), k=k)) for k in range(n_sub)]
    row = lambda bi, i: (bi, i, 0)
    return pl.pallas_call(
        functools.partial(_embed_inproj_kernel, n_end=FIRST + N_META + seq),
        grid=(b, lp // SEQ_TILE),
        in_specs=x_specs + [_const_spec((N_META, D_MODEL)), _const_spec((1, D_MODEL)),
                            _const_spec(w_r.shape), _const_spec(w_m.shape)],
        out_specs=[pl.BlockSpec((1, SEQ_TILE, D_MODEL), row), pl.BlockSpec((1, SEQ_TILE, w_r.shape[1]), row),
                   pl.BlockSpec((1, SEQ_TILE, w_m.shape[1]), row)],
        out_shape=[jax.ShapeDtypeStruct((b, lp, D_MODEL), F32), jax.ShapeDtypeStruct((b, lp, w_r.shape[1]), BF16),
                   jax.ShapeDtypeStruct((b, lp, w_m.shape[1]), BF16)],
        compiler_params=_cparams(("parallel", "parallel")),
        name="embed_inproj",
    )(*([x] * n_sub), meta_tokens.astype(F32), g, w_r, w_m)


def _mla_up_kernel(pm_ref, qn_ref, kvn_ref, wq_ref, wkv_ref, tq_ref, tk_ref, q_ref, k_ref, vt_ref):
    pm = pm_ref[0].astype(F32)
    c_q = pm[:, :MLA_Q_LORA]
    c_kv = pm[:, MLA_Q_LORA:MLA_Q_LORA + MLA_KV_LORA]
    k_r = pm[:, MLA_Q_LORA + MLA_KV_LORA:]
    q = jnp.dot(_rms(c_q, qn_ref[...]).astype(BF16), wq_ref[...], preferred_element_type=F32)
    kv = jnp.dot(_rms(c_kv, kvn_ref[...]).astype(BF16), wkv_ref[...], preferred_element_type=F32)
    rope_lanes = lax.broadcasted_iota(jnp.int32, (1, LANE), 1) < MLA_ROPE
    t_k = tk_ref[...]
    ck = jnp.where(rope_lanes, t_k, 0.0)
    sk = jnp.where(rope_lanes, pltpu.roll(t_k, MLA_ROPE, 1), 0.0)
    k_rope = (k_r * ck + pltpu.roll(k_r, MLA_ROPE, 1) * sk).astype(BF16)
    t_q = tq_ref[...]
    cq = jnp.concatenate([jnp.full(t_q.shape, Q_SCALE, F32), jnp.where(rope_lanes, t_q, 0.0)], 1)
    sq = jnp.concatenate([jnp.zeros(t_q.shape, F32), jnp.where(rope_lanes, pltpu.roll(t_q, MLA_ROPE, 1), 0.0)], 1)
    for hd in range(MLA_HEADS):
        qh = q[:, hd * MLA_QK_PAD:(hd + 1) * MLA_QK_PAD]
        qh = qh * cq + pltpu.roll(qh, MLA_QK_PAD - MLA_ROPE, 1) * sq
        q_ref[0, :, hd * MLA_QK_PAD:(hd + 1) * MLA_QK_PAD] = qh.astype(BF16)
        k_ref[0, :, hd * MLA_QK_PAD:hd * MLA_QK_PAD + MLA_NOPE] = (
            kv[:, hd * MLA_NOPE:(hd + 1) * MLA_NOPE].astype(BF16))
        k_ref[0, :, hd * MLA_QK_PAD + MLA_NOPE:(hd + 1) * MLA_QK_PAD] = k_rope
    vt_ref[0] = kv[:, MLA_HEADS * MLA_NOPE:].T.astype(BF16)


def _mla_up(pm, q_norm, kv_norm, wq, wkv, tabs):
    b, lp, _ = pm.shape
    t_q, t_k = tabs
    qk_w = MLA_HEADS * MLA_QK_PAD
    tile = max(t for t in range(LANE, MLA_UP_TILE_MAX + 1, LANE) if lp % t == 0)
    row = lambda bi, i: (bi, i, 0)
    tab = lambda bi, i: (i, 0)
    return pl.pallas_call(
        _mla_up_kernel,
        grid=(b, lp // tile),
        in_specs=[pl.BlockSpec((1, tile, MLA_COLS_PAD), row),
                  _const_spec((1, MLA_Q_LORA)), _const_spec((1, MLA_KV_LORA)),
                  _const_spec(wq.shape), _const_spec(wkv.shape),
                  pl.BlockSpec((tile, LANE), tab), pl.BlockSpec((tile, LANE), tab)],
        out_specs=[pl.BlockSpec((1, tile, qk_w), row),
                   pl.BlockSpec((1, tile, qk_w), row),
                   pl.BlockSpec((1, MLA_WIDTH, tile), lambda bi, i: (bi, 0, i))],
        out_shape=[jax.ShapeDtypeStruct((b, lp, qk_w), BF16),
                   jax.ShapeDtypeStruct((b, lp, qk_w), BF16),
                   jax.ShapeDtypeStruct((b, MLA_WIDTH, lp), BF16)],
        compiler_params=_cparams(("parallel", "parallel")),
        name="mla_up",
    )(pm, q_norm, kv_norm, wq, wkv, t_q, t_k)


def _rope_tables(lp):
    half = MLA_ROPE // 2
    inv = (np.float32(ROPE_THETA) ** (-np.arange(half, dtype=np.float32) / np.float32(half))).astype(np.float32)
    pos = np.maximum(np.arange(lp) - FIRST, 0).astype(np.float32)
    ang = (pos[:, None] * inv[None, :]).astype(np.float64)
    t_k = np.concatenate([np.cos(ang), np.cos(ang), np.sin(ang), np.sin(ang)], -1)
    return jnp.asarray(t_k * Q_SCALE, F32), jnp.asarray(t_k, F32)


def _chunk_id(row):
    frame = row - (FIRST + N_META)
    return jnp.where(frame < 0, 0, 1 + (frame >> (CHUNK.bit_length() - 1)))


def _flash_kernel(q_ref, k_ref, vt_ref, o_ref, m_sc, l_sc, acc_sc, *, n_end):
    i = pl.program_id(1)
    tq = q_ref.shape[1]
    lp = k_ref.shape[1]
    heads = range(MLA_HEADS)
    q = [q_ref[0, :, h * MLA_QK_PAD:(h + 1) * MLA_QK_PAD] for h in heads]
    m_sc[...] = jnp.full(m_sc.shape, -jnp.inf, F32)
    l_sc[...] = jnp.zeros(l_sc.shape, F32)
    acc_sc[...] = jnp.zeros(acc_sc.shape, F32)
    n_full = (i * tq) // FLASH_KEYS
    full_end = n_full * FLASH_KEYS
    q_cid = _chunk_id(i * tq + lax.broadcasted_iota(jnp.int32, (1, tq), 1))

    def chunk(start, width, mask):
        keys = pl.ds(start, width)
        s = [lax.dot_general(k_ref[0, keys, h * MLA_QK_PAD:(h + 1) * MLA_QK_PAD], q[h], _NT,
                             preferred_element_type=F32) for h in heads]
        if mask is not None:
            k_pos = start + lax.broadcasted_iota(jnp.int32, (width, 1), 0)
            visible = k_pos >= FIRST
            if mask == "causal":
                visible = visible & (k_pos >= full_end) & (k_pos < n_end) & (_chunk_id(k_pos) <= q_cid)
            s = [jnp.where(visible, s[h], -jnp.inf) for h in heads]
        m_prev = [m_sc[h] for h in heads]
        m_new = [jnp.maximum(m_prev[h], jnp.max(s[h], 0, keepdims=True)) for h in heads]
        p = [jnp.exp2(s[h] - m_new[h]) for h in heads]
        for h in heads:
            alpha = jnp.exp2(m_prev[h] - m_new[h])
            l_sc[h] = alpha * l_sc[h] + jnp.sum(p[h], 0, keepdims=True)
            acc_sc[h] = alpha * acc_sc[h] + jnp.dot(vt_ref[0, h * MLA_V:(h + 1) * MLA_V, keys],
                                                    p[h].astype(BF16), preferred_element_type=F32)
            m_sc[h] = m_new[h]

    def chunk_pair(j, carry):
        chunk(pl.multiple_of(j * (2 * FLASH_KEYS), 2 * FLASH_KEYS), 2 * FLASH_KEYS, None)
        return carry

    @pl.when(n_full >= 2)
    def _():
        chunk(0, 2 * FLASH_KEYS, "front")

    lax.fori_loop(1, n_full // 2, chunk_pair, 0)

    @pl.when(n_full % 2 == 1)
    def _():
        chunk(pl.multiple_of(full_end - FLASH_KEYS, FLASH_KEYS), FLASH_KEYS, "front")

    for offset in range(0, FLASH_KEYS, math.gcd(FLASH_KEYS, tq)):
        @pl.when(i * tq - full_end == offset)
        def _(width=offset + tq):
            chunk(pl.multiple_of(full_end, LANE), width, "causal")

    for h in heads:
        o_ref[0, :, h * MLA_V:(h + 1) * MLA_V] = (acc_sc[h] / l_sc[h]).T.astype(o_ref.dtype)


def _flash(q, k, vt, n_end):
    b, lp, _ = q.shape
    assert SEQ_TILE % LANE == 0 and FLASH_KEYS % LANE == 0 and SEQ_TILE % CHUNK == 0 and (FIRST + N_META) % CHUNK == 0
    assert FIRST + N_META <= min(SEQ_TILE, FLASH_KEYS)
    return pl.pallas_call(
        functools.partial(_flash_kernel, n_end=n_end),
        grid=(b, lp // SEQ_TILE),
        in_specs=[pl.BlockSpec((1, SEQ_TILE, MLA_HEADS * MLA_QK_PAD), lambda bi, i: (bi, i, 0)),
                  pl.BlockSpec((1, lp, MLA_HEADS * MLA_QK_PAD), lambda bi, i: (bi, 0, 0)),
                  pl.BlockSpec((1, MLA_WIDTH, lp), lambda bi, i: (bi, 0, 0))],
        out_specs=pl.BlockSpec((1, SEQ_TILE, MLA_WIDTH), lambda bi, i: (bi, i, 0)),
        out_shape=jax.ShapeDtypeStruct((b, lp, MLA_WIDTH), BF16),
        scratch_shapes=[pltpu.VMEM((MLA_HEADS, 1, SEQ_TILE), F32), pltpu.VMEM((MLA_HEADS, 1, SEQ_TILE), F32),
                        pltpu.VMEM((MLA_HEADS, MLA_V, SEQ_TILE), F32)],
        compiler_params=_cparams(("parallel", "arbitrary")),
        name="mla_flash",
    )(q, k, vt)


def _rwkv_prep_kernel(*refs, has_vres):
    (p_ref, prev_ref, mu_ref, w0_ref, w2_ref, a0_ref, a2_ref, g2_ref, kk_ref, ka_ref, rk_ref,
     bd_ref) = refs[:12]
    n_in = 16 if has_vres else 12
    r_out, k_out, v_out, kk_out, bb_out, lw_out, bonus_out, g_out = refs[n_in:]
    c = RWKV_WIDTH
    tm = p_ref.shape[1]
    first_tile = pl.program_id(1) == 0
    pos = pl.program_id(1) * tm + lax.broadcasted_iota(jnp.int32, (tm, 1), 0)
    p = jnp.where(pos >= FIRST, p_ref[0].astype(F32), 0.0)
    halo = jnp.where(first_tile, 0.0, prev_ref[0, PREV_ROWS - 1:PREV_ROWS, :].astype(F32))
    rolled = pltpu.roll(p, 1, 0)
    head = jnp.where(lax.broadcasted_iota(jnp.int32, (8, 1), 0) == 0, halo, rolled[:8])
    prev = jnp.concatenate([head, rolled[8:]], 0)
    ps = p + (prev - p) * mu_ref[...]
    r = ps[:, :c]
    k = ps[:, c:2 * c]
    v = ps[:, 2 * c:3 * c]
    xwa = ps[:, 3 * c:3 * c + W_LORA + A_LORA]
    xg = ps[:, 3 * c + W_LORA + A_LORA:]
    zw = w0_ref[...] + jnp.dot(jnp.tanh(xwa).astype(BF16), w2_ref[...], preferred_element_type=F32)
    nz = -zw
    softplus = jnp.maximum(nz, 0.0) + jnp.log(1.0 + jnp.exp(-jnp.abs(nz)))
    log_decay = -jnp.exp(-softplus - 0.5)
    if has_vres:
        vf_ref, v0_ref, v1_ref, v2_ref = refs[12:16]
        lo = jnp.dot(v.astype(BF16), v1_ref[...], preferred_element_type=F32)
        gate = _sigmoid(v0_ref[...] + jnp.dot(lo.astype(BF16), v2_ref[...], preferred_element_type=F32))
        v = v + (vf_ref[0].astype(F32) - v) * gate
    a = _sigmoid(a0_ref[...] + jnp.dot(xwa.astype(BF16), a2_ref[...], preferred_element_type=F32))
    g = jnp.dot(_sigmoid(xg).astype(BF16), g2_ref[...], preferred_element_type=F32)
    bd = bd_ref[...]
    kk = k * kk_ref[...]
    kk = kk * lax.rsqrt(jnp.maximum(_head_sum(kk * kk, bd), 1e-24))
    k = k * (1.0 + (a - 1.0) * ka_ref[...])
    bonus = _head_sum(r * k * rk_ref[...], bd) * v
    r_out[0] = r.astype(r_out.dtype)
    k_out[0] = k.astype(k_out.dtype)
    v_out[0] = v.astype(v_out.dtype)
    kk_out[0] = kk.astype(kk_out.dtype)
    bb_out[0] = (kk * a).astype(bb_out.dtype)
    lw_out[0] = log_decay
    bonus_out[0] = bonus.astype(bonus_out.dtype)
    g_out[0] = g.astype(g_out.dtype)


def _rwkv_prep(pr, prm, v_first):
    b, lp, cols = pr.shape
    has_vres = v_first is not None
    c = RWKV_WIDTH
    row = lambda bi, i: (bi, i, 0)
    prev = lambda bi, i: (bi, jnp.maximum(i * (SEQ_TILE // PREV_ROWS) - 1, 0), 0)
    names = ["mu", "w0", "w2", "a0", "a2", "g2", "k_k", "k_a", "r_k", "bd"]
    args = [pr, pr] + [prm[n] for n in names]
    in_specs = [pl.BlockSpec((1, SEQ_TILE, cols), row), pl.BlockSpec((1, PREV_ROWS, cols), prev)]
    in_specs += [_const_spec(prm[n].shape) for n in names]
    if has_vres:
        args += [v_first, prm["v0"], prm["v1"], prm["v2"]]
        in_specs += [pl.BlockSpec((1, SEQ_TILE, c), row)]
        in_specs += [_const_spec(prm[n].shape) for n in ("v0", "v1", "v2")]
    out_spec = pl.BlockSpec((1, SEQ_TILE, c), row)
    return pl.pallas_call(
        functools.partial(_rwkv_prep_kernel, has_vres=has_vres),
        grid=(b, lp // SEQ_TILE),
        in_specs=in_specs,
        out_specs=[out_spec] * 8,
        out_shape=[jax.ShapeDtypeStruct((b, lp, c), F32 if n == 5 else BF16) for n in range(8)],
        compiler_params=_cparams(("parallel", "parallel")),
        name="rwkv_prep",
    )(*args)


_NN = (((1,), (0,)), ((), ()))
_NT = (((1,), (1,)), ((), ()))


def _mm(a, b, dims):
    return lax.dot_general(a.astype(BF16), b.astype(BF16), dims, preferred_element_type=F32)


def _rwkv_scan_kernel(r_ref, k_ref, v_ref, kk_ref, bb_ref, lw_ref, y_ref, s_sc):
    @pl.when(pl.program_id(1) == 0)
    def _():
        s_sc[...] = jnp.zeros(s_sc.shape, F32)

    c = SCAN_CHUNK
    n = 2 * c
    n_chunks = r_ref.shape[1] // c
    ri = lax.broadcasted_iota(jnp.int32, (c, c), 0)
    ci = lax.broadcasted_iota(jnp.int32, (c, c), 1)
    tri = jnp.where(ri >= ci, 1.0, 0.0).astype(BF16)
    row = lax.broadcasted_iota(jnp.int32, (2 * n, 2 * n), 0)
    col = lax.broadcasted_iota(jnp.int32, (2 * n, 2 * n), 1)
    t_idx = row & (c - 1)
    s_idx = col & (c - 1)
    causal = (t_idx > s_idx) | ((t_idx == s_idx) & (row >= n))
    eye = jnp.where(lax.broadcasted_iota(jnp.int32, (n, n), 0) == lax.broadcasted_iota(jnp.int32, (n, n), 1),
                    1.0, 0.0).astype(F32)
    head0 = lax.broadcasted_iota(jnp.int32, (1, LANE), 1) < RWKV_HEAD

    def stack(x):
        return jnp.concatenate([jnp.where(head0, x, 0.0), jnp.where(head0, 0.0, x)], axis=0)

    ar, bk, v_t, w_end = [], [], [], []
    for ch in range(n_chunks):
        rows = slice(ch * c, (ch + 1) * c)
        lw = lw_ref[0, rows, :]
        lw_hi = lw.astype(BF16)
        lw_lo = (lw - lw_hi.astype(F32)).astype(BF16)
        cum = (jnp.dot(tri, lw_hi, preferred_element_type=F32)
               + jnp.dot(tri, lw_lo, preferred_element_type=F32))
        w_incl = jnp.exp(cum)
        w_inv = jnp.exp(-cum)
        a_hat = -kk_ref[0, rows, :].astype(F32) * jnp.exp(cum - lw)
        r_hat = r_ref[0, rows, :].astype(F32) * w_incl
        b_hat = bb_ref[0, rows, :].astype(F32) * w_inv
        k_hat = k_ref[0, rows, :].astype(F32) * w_inv
        v = v_ref[0, rows, :].astype(F32)
        for pr in range(RWKV_PAIRS):
            sl = slice(pr * LANE, (pr + 1) * LANE)
            ar.append(jnp.concatenate([stack(a_hat[:, sl]), stack(r_hat[:, sl])], 0).astype(BF16))
            bk.append(jnp.concatenate([stack(b_hat[:, sl]), stack(k_hat[:, sl])], 0).astype(BF16))
            v_t.append(stack(v[:, sl]).T.astype(BF16))
            w_end.append(w_incl[c - 1:c, sl])
    every = range(len(ar))
    gram = [jnp.where(causal, _mm(ar[g], bk[g], _NT), 0.0) for g in every]
    pw = [gram[g][:n, :n] for g in every]
    t_inv = [eye + pw[g] for g in every]
    pw = [_mm(pw[g], pw[g], _NN) for g in every]
    for _ in range(4):
        both = [_mm(pw[g], jnp.concatenate([pw[g], t_inv[g]], 1), _NN) for g in every]
        pw = [both[g][:, :n] for g in every]
        t_inv = [t_inv[g] + both[g][:, n:] for g in every]
    t_inv = [(t_inv[g] + _mm(pw[g], t_inv[g], _NN)).astype(BF16) for g in every]
    va = [_mm(v_t[g], gram[g][:n, n:], _NT) for g in every]
    q_bk = [gram[g][n:, :].astype(BF16) for g in every]

    state = [s_sc[pr] for pr in range(RWKV_PAIRS)]
    for ch in range(n_chunks):
        gs = [ch * RWKV_PAIRS + pr for pr in range(RWKV_PAIRS)]
        s_ar = [_mm(state[pr], ar[g], _NT) for pr, g in enumerate(gs)]
        u_t = [_mm(s_ar[pr][:, :n] + va[g], t_inv[g], _NT) for pr, g in enumerate(gs)]
        uv = [jnp.concatenate([u_t[pr].astype(BF16), v_t[g]], 1) for pr, g in enumerate(gs)]
        state = [(state[pr] + _mm(uv[pr], bk[g], _NN)) * w_end[g] for pr, g in enumerate(gs)]
        for pr, g in enumerate(gs):
            y = (s_ar[pr][:, n:] + _mm(uv[pr], q_bk[g], _NT)).T
            y_ref[0, ch * c:(ch + 1) * c, pr * LANE:(pr + 1) * LANE] = y[:c] + y[c:]
    for pr in range(RWKV_PAIRS):
        s_sc[pr] = state[pr]


def _rwkv_scan(r, k, v, kk, bb, lw):
    b, lp, c = r.shape
    rows = SCAN_STEP_CHUNKS * SCAN_CHUNK
    spec = pl.BlockSpec((1, rows, c), lambda bi, i: (bi, i, 0))
    return pl.pallas_call(
        _rwkv_scan_kernel,
        grid=(b, lp // rows),
        in_specs=[spec] * 6,
        out_specs=spec,
        out_shape=jax.ShapeDtypeStruct((b, lp, c), F32),
        scratch_shapes=[pltpu.VMEM((RWKV_PAIRS, LANE, LANE), F32)],
        compiler_params=_cparams(("parallel", "arbitrary")),
        name="rwkv_scan",
    )(r, k, v, kk, bb, lw)


def _outproj_kernel(ys_ref, bonus_ref, g_ref, ymla_ref, lnw_ref, lnb_ref, bd_ref, wo_ref, h_ref, o_ref):
    y = ys_ref[...]
    bd = bd_ref[...]
    inv_n = 1.0 / RWKV_HEAD
    d = y - _head_sum(y, bd) * inv_n
    var = _head_sum(d * d, bd) * inv_n
    yn = d * lax.rsqrt(var + GN_EPS) * lnw_ref[...] + lnb_ref[...]
    yr = ((yn + bonus_ref[...].astype(F32)) * g_ref[...].astype(F32)).astype(BF16)
    o_ref[...] = (h_ref[...]
                  + jnp.dot(ymla_ref[...], wo_ref[:MLA_WIDTH, :], preferred_element_type=F32)
                  + jnp.dot(yr, wo_ref[MLA_WIDTH:, :], preferred_element_type=F32))


def _outproj(ys, bonus, g, ymla, ln_w, ln_b, bd, wo, h):
    tp = h.shape[0]
    c = RWKV_WIDTH
    rc = pl.BlockSpec((ROW_TILE, c), lambda i: (i, 0))
    rd = pl.BlockSpec((ROW_TILE, D_MODEL), lambda i: (i, 0))
    return pl.pallas_call(
        _outproj_kernel,
        grid=(tp // ROW_TILE,),
        in_specs=[rc, rc, rc, rc, _const_spec((1, c)), _const_spec((1, c)), _const_spec(bd.shape),
                  _const_spec(wo.shape), rd],
        out_specs=rd,
        out_shape=jax.ShapeDtypeStruct((tp, D_MODEL), F32),
        compiler_params=_cparams(("parallel",)),
        name="outproj",
    )(ys, bonus, g, ymla, ln_w, ln_b, bd, wo, h)


def _ffn_chunks(u, wg_ref, wu_ref, wd_ref, idx, ff, chunk):
    acc = None
    for c0 in range(0, ff, chunk):
        sl = slice(c0, c0 + chunk)
        gate = jnp.dot(u, wg_ref[idx + (slice(None), sl)], preferred_element_type=F32)
        up = jnp.dot(u, wu_ref[idx + (slice(None), sl)], preferred_element_type=F32)
        act = (gate * _sigmoid(gate) * up).astype(BF16)
        part = jnp.dot(act, wd_ref[idx + (sl, slice(None))], preferred_element_type=F32)
        acc = part if acc is None else acc + part
    return acc


def _dense_ffn_kernel(h_ref, g_ref, wg_ref, wu_ref, wd_ref, o_ref, *, ff, chunk):
    h = h_ref[...]
    u = _rms(h, g_ref[...]).astype(BF16)
    o_ref[...] = h + _ffn_chunks(u, wg_ref, wu_ref, wd_ref, (), ff, chunk)


def _dense_ffn(h, g, wg, wu, wd):
    tp = h.shape[0]
    ff = wg.shape[1]
    rd = pl.BlockSpec((ROW_TILE, D_MODEL), lambda i: (i, 0))
    once = lambda shape: pl.BlockSpec(shape, lambda i: (0, 0), pipeline_mode=pl.Buffered(1))
    return pl.pallas_call(
        functools.partial(_dense_ffn_kernel, ff=ff, chunk=256),
        grid=(tp // ROW_TILE,),
        in_specs=[rd, _const_spec((1, D_MODEL)), once(wg.shape), once(wu.shape), once(wd.shape)],
        out_specs=rd,
        out_shape=jax.ShapeDtypeStruct((tp, D_MODEL), F32),
        compiler_params=_cparams(("parallel",)),
        name="dense_ffn",
    )(h, g, wg, wu, wd)


def _router_kernel(h_ref, g_ref, wr_ref, br_ref, u_ref, meta_ref, meta_t_ref, cnt_ref, carry_sc, *, n_end):
    @pl.when((pl.program_id(0) == 0) & (pl.program_id(1) == 0))
    def _():
        carry_sc[...] = jnp.zeros(carry_sc.shape, F32)

    tm = h_ref.shape[1]
    u = _rms(h_ref[0], g_ref[...])
    _to_row_tiles(u_ref, u)
    u_hi, u_lo = _split(u)
    w_hi, w_lo = _split(wr_ref[...])
    logits = (jnp.dot(u_hi, w_hi, preferred_element_type=F32) + jnp.dot(u_hi, w_lo, preferred_element_type=F32)
              + jnp.dot(u_lo, w_hi, preferred_element_type=F32) + br_ref[...])
    lane = lax.broadcasted_iota(jnp.int32, (tm, LANE), 1).astype(F32)
    logits = jnp.where(lane < N_EXPERTS, logits, -jnp.inf)
    top0 = jnp.max(logits, -1, keepdims=True)
    e0 = jnp.min(jnp.where(logits == top0, lane, float(LANE)), -1, keepdims=True)
    rest = jnp.where(lane == e0, -jnp.inf, logits)
    top1 = jnp.max(rest, -1, keepdims=True)
    e1 = jnp.min(jnp.where(rest == top1, lane, float(LANE)), -1, keepdims=True)
    ex = jnp.exp(top1 - top0)
    g0 = 1.0 / (1.0 + ex)
    g1 = ex / (1.0 + ex)
    pos = pl.program_id(1) * tm + lax.broadcasted_iota(jnp.int32, (tm, 1), 0)
    valid = (pos >= FIRST) & (pos < n_end)
    oh0 = jnp.where((lane == e0) & valid, 1.0, 0.0)
    oh1 = jnp.where((lane == e1) & valid, 1.0, 0.0)
    ri = lax.broadcasted_iota(jnp.int32, (tm, tm), 0)
    ci = lax.broadcasted_iota(jnp.int32, (tm, tm), 1)
    before = jnp.where(ri > ci, 1.0, 0.0).astype(BF16)
    both = oh0 + oh1
    seen = carry_sc[...] + jnp.dot(before, both.astype(BF16), preferred_element_type=F32)
    rank0 = jnp.sum(jnp.where(lane == e0, seen, 0.0), -1, keepdims=True)
    rank1 = jnp.sum(jnp.where(lane == e1, seen, 0.0), -1, keepdims=True)
    carry_sc[...] = carry_sc[...] + jnp.sum(both, 0, keepdims=True)
    cnt_ref[...] = jnp.broadcast_to(carry_sc[...], cnt_ref.shape)
    vf = jnp.where(valid, 1.0, 0.0)
    meta = jnp.zeros((tm, LANE), F32)
    for field, value in ((F_E0, e0), (F_E1, e1), (F_RANK0, rank0), (F_RANK1, rank1),
                         (F_GATE0, g0 * vf), (F_GATE1, g1 * vf), (F_VALID, vf)):
        meta = jnp.where(lane == field, value, meta)
    meta_ref[0] = meta
    meta_t_ref[...] = meta.T[:N_FIELDS]


def _router(h, g, wr, br, n_end):
    b, lp, _ = h.shape
    row = lambda bi, i: (bi, i, 0)
    return pl.pallas_call(
        functools.partial(_router_kernel, n_end=n_end),
        grid=(b, lp // SEQ_TILE),
        in_specs=[pl.BlockSpec((1, SEQ_TILE, D_MODEL), row), _const_spec((1, D_MODEL)),
                  _const_spec(wr.shape), _const_spec(br.shape)],
        out_specs=[pl.BlockSpec((SEQ_TILE * ROW_TILES, LANE), lambda bi, i: (bi * (lp // SEQ_TILE) + i, 0)),
                   pl.BlockSpec((1, SEQ_TILE, LANE), row),
                   pl.BlockSpec((N_FIELDS, SEQ_TILE), lambda bi, i: (0, bi * (lp // SEQ_TILE) + i)),
                   _const_spec((8, LANE))],
        out_shape=[jax.ShapeDtypeStruct((b * lp * ROW_TILES, LANE), F32), jax.ShapeDtypeStruct((b, lp, LANE), F32),
                   jax.ShapeDtypeStruct((N_FIELDS, b * lp), F32), jax.ShapeDtypeStruct((8, LANE), F32)],
        scratch_shapes=[pltpu.VMEM((1, LANE), F32)],
        compiler_params=_cparams(("arbitrary", "arbitrary")),
        name="moe_router",
    )(h, g, wr, br)


def _dispatch_kernel(zb_ref, s0_ref, s1_ref, u_ref, xb_ref, zero_sc, sem):
    tm = u_ref.shape[0] // ROW_TILES
    block_rows = MOE_BLOCK * ROW_TILES

    @pl.when(pl.program_id(0) == 0)
    def _():
        zero_sc[...] = jnp.zeros(zero_sc.shape, F32)

        def fill(j):
            start = pl.multiple_of(zb_ref[j] * block_rows, block_rows)
            return pltpu.make_async_copy(zero_sc, xb_ref.at[pl.ds(start, block_rows), :], sem)

        for j in range(zb_ref.shape[0]):
            @pl.when(zb_ref[j] >= 0)
            def _(j=j):
                fill(j).start()
        for j in range(zb_ref.shape[0]):
            @pl.when(zb_ref[j] >= 0)
            def _(j=j):
                fill(j).wait()

    def copies(rw):
        return [pltpu.make_async_copy(_row_tile(u_ref, rw), _row_tile(xb_ref, s_ref[0, 0, rw]), sem)
                for s_ref in (s0_ref, s1_ref)]

    def start(j, carry):
        for kq in range(DMA_UNROLL):
            for thread, cp in enumerate(copies(j * DMA_UNROLL + kq)):
                cp.start(priority=thread)
        return carry

    lax.fori_loop(0, tm // DMA_UNROLL, start, 0)
    for _ in (s0_ref, s1_ref):
        pltpu.make_async_copy(u_ref, xb_ref.at[pl.ds(0, tm * ROW_TILES), :], sem).wait()


def _dispatch(zero_blocks, slot0, slot1, u, n_rows, tm):
    tp = u.shape[0] // ROW_TILES
    sspec = pl.BlockSpec((1, 1, tm), lambda i, zb: (i, 0, 0), memory_space=pltpu.SMEM)
    return pl.pallas_call(
        _dispatch_kernel,
        grid_spec=pltpu.PrefetchScalarGridSpec(
            num_scalar_prefetch=1,
            grid=(tp // tm,),
            in_specs=[sspec, sspec, pl.BlockSpec((tm * ROW_TILES, LANE), lambda i, zb: (i, 0))],
            out_specs=pl.BlockSpec(memory_space=pl.ANY),
            scratch_shapes=[pltpu.VMEM((MOE_BLOCK * ROW_TILES, LANE), F32), pltpu.SemaphoreType.DMA(())]),
        out_shape=jax.ShapeDtypeStruct((n_rows * ROW_TILES, LANE), F32),
        compiler_params=_cparams(("arbitrary",)),
        name="moe_dispatch",
    )(zero_blocks, slot0.reshape(tp // tm, 1, tm), slot1.reshape(tp // tm, 1, tm), u)


def _expert_kernel(be_ref, fill_ref, x_ref, wg_ref, wu_ref, wd_ref, o_ref, *, ff, chunk):
    del be_ref
    fill = fill_ref[pl.program_id(0)]

    @pl.when(fill > 0)
    def _():
        x = _from_row_tiles(x_ref, MOE_BLOCK).astype(BF16)
        _to_row_tiles(o_ref, _ffn_chunks(x, wg_ref, wu_ref, wd_ref, (0,), ff, chunk))

    @pl.when(fill <= 0)
    def _():
        o_ref[...] = jnp.zeros(o_ref.shape, F32)


def _experts(block_e, block_fill, xb, wg, wu, wd):
    n_slots = block_e.shape[0] * MOE_BLOCK
    ff = wg.shape[2]
    rows = pl.BlockSpec((MOE_BLOCK * ROW_TILES, LANE), lambda i, be, fill: (i, 0))
    return pl.pallas_call(
        functools.partial(_expert_kernel, ff=ff, chunk=ff // 2),
        grid_spec=pltpu.PrefetchScalarGridSpec(
            num_scalar_prefetch=2,
            grid=(n_slots // MOE_BLOCK,),
            in_specs=[rows,
                      pl.BlockSpec((1, D_MODEL, ff), lambda i, be, fill: (be[i], 0, 0)),
                      pl.BlockSpec((1, D_MODEL, ff), lambda i, be, fill: (be[i], 0, 0)),
                      pl.BlockSpec((1, ff, D_MODEL), lambda i, be, fill: (be[i], 0, 0))],
            out_specs=rows),
        out_shape=jax.ShapeDtypeStruct((n_slots * ROW_TILES, LANE), F32),
        compiler_params=_cparams(("arbitrary",), EXPERT_VMEM_LIMIT),
        name="moe_experts",
    )(block_e, block_fill, xb, wg, wu, wd)


def _combine_kernel(s0_ref, s1_ref, n0_ref, n1_ref, h_ref, gates_ref, fn_ref, yb_ref, o_ref, y0_sc, y1_sc, sems, *,
                    per_row, n_frame_tiles):
    t = pl.program_id(0)
    tm = h_ref.shape[0]
    cur = t % 2

    def copies(rw, a_ref, b_ref, buf):
        return [pltpu.make_async_copy(_row_tile(yb_ref, s_ref[0, 0, rw]), _row_tile(y_sc.at[buf], rw), sems.at[buf])
                for s_ref, y_sc in ((a_ref, y0_sc), (b_ref, y1_sc))]

    def issue(a_ref, b_ref, buf):
        def body(j, carry):
            for kq in range(DMA_UNROLL):
                for thread, cp in enumerate(copies(j * DMA_UNROLL + kq, a_ref, b_ref, buf)):
                    cp.start(priority=thread)
            return carry
        lax.fori_loop(0, tm // DMA_UNROLL, body, 0)

    @pl.when(t == 0)
    def _():
        issue(s0_ref, s1_ref, 0)

    @pl.when(t + 1 < pl.num_programs(0))
    def _():
        issue(n0_ref, n1_ref, 1 - cur)

    for y_sc in (y0_sc, y1_sc):
        pltpu.make_async_copy(yb_ref.at[pl.ds(0, tm * ROW_TILES), :], y_sc.at[cur], sems.at[cur]).wait()
    tile = t % per_row

    @pl.when((tile >= 1) & (tile <= n_frame_tiles))
    def _():
        gates = gates_ref[...]
        f = (_from_row_tiles(y0_sc.at[cur], tm) * gates[:, F_GATE0:F_GATE0 + 1]
             + _from_row_tiles(y1_sc.at[cur], tm) * gates[:, F_GATE1:F_GATE1 + 1])
        o_ref[0] = _rms(h_ref[...] + f, fn_ref[...])


def _combine(slot0, slot1, h, meta, fn, yb, tm, b, seq):
    tp = h.shape[0]
    n_tiles = tp // tm
    per_row = n_tiles // b
    assert (FIRST + N_META) == tm and seq % tm == 0
    sspec = pl.BlockSpec((1, 1, tm), lambda i: (i, 0, 0), memory_space=pltpu.SMEM)
    nspec = pl.BlockSpec((1, 1, tm), lambda i: (jnp.minimum(i + 1, n_tiles - 1), 0, 0), memory_space=pltpu.SMEM)
    rd = pl.BlockSpec((tm, D_MODEL), lambda i: (i, 0))
    s0 = slot0.reshape(n_tiles, 1, tm)
    s1 = slot1.reshape(n_tiles, 1, tm)
    return pl.pallas_call(
        functools.partial(_combine_kernel, per_row=per_row, n_frame_tiles=seq // tm),
        grid=(n_tiles,),
        in_specs=[sspec, sspec, nspec, nspec, rd, pl.BlockSpec((tm, LANE), lambda i: (i, 0)),
                  _const_spec((1, D_MODEL)), pl.BlockSpec(memory_space=pl.ANY)],
        out_specs=pl.BlockSpec((1, tm, D_MODEL), lambda i: (i // per_row, jnp.clip(i % per_row - 1, 0, seq // tm - 1), 0)),
        out_shape=jax.ShapeDtypeStruct((b, seq, D_MODEL), F32),
        scratch_shapes=[pltpu.VMEM((2, tm * ROW_TILES, LANE), F32), pltpu.VMEM((2, tm * ROW_TILES, LANE), F32),
                        pltpu.SemaphoreType.DMA((2,))],
        compiler_params=_cparams(("arbitrary",)),
        name="moe_combine",
    )(s0, s1, s0, s1, h, meta, fn, yb)


def _final_norm_kernel(h_ref, g_ref, o_ref):
    o_ref[...] = _rms(h_ref[...], g_ref[...])


def _final_norm(h, g):
    tp = h.shape[0]
    rd = pl.BlockSpec((ROW_TILE, D_MODEL), lambda i: (i, 0))
    return pl.pallas_call(
        _final_norm_kernel,
        grid=(tp // ROW_TILE,),
        in_specs=[rd, _const_spec((1, D_MODEL))],
        out_specs=rd,
        out_shape=jax.ShapeDtypeStruct((tp, D_MODEL), F32),
        compiler_params=_cparams(("parallel",)),
        name="final_norm",
    )(h, g)


def _rot_cols(w):
    half = MLA_ROPE // 2
    return jnp.concatenate([-w[..., half:], w[..., :half]], -1)


def _prep_inproj(w_in):
    m_q = w_in[:, :MLA_Q_LORA + MLA_KV_LORA]
    k_rope = w_in[:, MLA_Q_LORA + MLA_KV_LORA:MLA_Q_LORA + MLA_KV_LORA + MLA_ROPE]
    w_m = jnp.concatenate([m_q, k_rope, _rot_cols(k_rope)], -1)
    w_r = w_in[:, MLA_Q_LORA + MLA_KV_LORA + MLA_ROPE:]
    return w_r.astype(BF16), w_m.astype(BF16)


def _prep_wq(w_uq):
    w = w_uq.reshape(MLA_Q_LORA, MLA_HEADS, MLA_NOPE + MLA_ROPE)
    rope = w[..., MLA_NOPE:]
    return jnp.concatenate([w, _rot_cols(rope)], -1).reshape(MLA_Q_LORA, MLA_HEADS * MLA_QK_PAD).astype(BF16)


def _prep_wkv(w_ukv):
    w = w_ukv.reshape(MLA_KV_LORA, MLA_HEADS, MLA_NOPE + MLA_V)
    k_nope = w[..., :MLA_NOPE].reshape(MLA_KV_LORA, MLA_HEADS * MLA_NOPE)
    v = w[..., MLA_NOPE:].reshape(MLA_KV_LORA, MLA_HEADS * MLA_V)
    return jnp.concatenate([k_nope, v], -1).astype(BF16)


def _row(x):
    return x.reshape(1, -1).astype(F32)


def _block_diag_ones():
    idx = np.arange(RWKV_WIDTH) // RWKV_HEAD
    return jnp.asarray(idx[:, None] == idx[None, :], BF16)


def kernel(x, meta_tokens, attn_norm, ffn_norm, final_norm, w_in, w_out, mla_q_norm, mla_kv_norm, mla_w_uq, mla_w_ukv, rwkv_mu, rwkv_w0, rwkv_w2, rwkv_a0, rwkv_a2, rwkv_g2, rwkv_k_k, rwkv_k_a, rwkv_r_k, rwkv_ln_w, rwkv_ln_b, rwkv_v0, rwkv_v1, rwkv_v2, ffn_w_gate, ffn_w_up, ffn_w_down, moe_router, moe_router_bias, moe_w_gate, moe_w_up, moe_w_down):
    b, seq, d = x.shape
    depth = attn_norm.shape[0]
    n_end = FIRST + N_META + seq
    lp = -(-n_end // SEQ_TILE) * SEQ_TILE
    tp = b * lp
    assert d == D_MODEL and tp % ROW_TILE == 0 and lp % SCAN_CHUNK == 0

    tabs = _rope_tables(lp)
    bd = _block_diag_ones()
    zeros_wa = jnp.zeros((W_LORA, RWKV_WIDTH), F32)
    v_first = None
    for i in range(depth):
        w_r, w_m = _prep_inproj(w_in[i])
        if i == 0:
            h, pr, pm = _embed_inproj(x, meta_tokens, lp, _row(attn_norm[i]), w_r, w_m)
            h = h.reshape(tp, d)
        else:
            pr, pm = _norm_inproj(h, _row(attn_norm[i]), w_r, w_m)

        q, k, v = _mla_up(pm.reshape(b, lp, -1), _row(mla_q_norm[i]), _row(mla_kv_norm[i]),
                          _prep_wq(mla_w_uq[i]), _prep_wkv(mla_w_ukv[i]), tabs)
        y_mla = _flash(q, k, v, n_end)

        prm = {
            "mu": _row(rwkv_mu[i]), "w0": _row(rwkv_w0[i]), "a0": _row(rwkv_a0[i]),
            "w2": jnp.concatenate([rwkv_w2[i], zeros_wa], 0).astype(BF16),
            "a2": jnp.concatenate([zeros_wa, rwkv_a2[i]], 0).astype(BF16),
            "g2": rwkv_g2[i].astype(BF16),
            "k_k": _row(rwkv_k_k[i]), "k_a": _row(rwkv_k_a[i]), "r_k": _row(rwkv_r_k[i]), "bd": bd,
        }
        if i > 0:
            prm["v0"] = _row(rwkv_v0[i - 1])
            prm["v1"] = jnp.pad(rwkv_v1[i - 1], ((0, 0), (0, LANE - V_LORA))).astype(BF16)
            prm["v2"] = jnp.pad(rwkv_v2[i - 1], ((0, LANE - V_LORA), (0, 0))).astype(BF16)
        r_, k_, v_, kk_, bb_, lw_, bonus, gate = _rwkv_prep(pr.reshape(b, lp, -1), prm, v_first)
        if i == 0:
            v_first = v_
        ys = _rwkv_scan(r_, k_, v_, kk_, bb_, lw_)

        flat = lambda t: t.reshape(tp, -1)
        h = _outproj(flat(ys), flat(bonus), flat(gate), flat(y_mla), _row(rwkv_ln_w[i]), _row(rwkv_ln_b[i]),
                     bd, w_out[i].astype(BF16), h)

        j = i // 2
        last = i == depth - 1
        if i % 2 == 0:
            h = _dense_ffn(h, _row(ffn_norm[i]), ffn_w_gate[j].astype(BF16), ffn_w_up[j].astype(BF16),
                           ffn_w_down[j].astype(BF16))
            if last:
                h = _final_norm(h, _row(final_norm)).reshape(b, lp, d)[:, FIRST + N_META:n_end]
        else:
            if not last:
                raise NotImplementedError("an MoE layer that is not the last layer")
            h = _moe(h, b, lp, n_end, _row(ffn_norm[i]), moe_router[j], moe_router_bias[j],
                     moe_w_gate[j], moe_w_up[j], moe_w_down[j], _row(final_norm))
    return h


def _moe(h, b, lp, n_end, g, router, router_bias, w_gate, w_up, w_down, final_g):
    tp = b * lp
    wr = jnp.pad(router, ((0, 0), (0, LANE - N_EXPERTS))).astype(F32)
    br = jnp.pad(router_bias, (0, LANE - N_EXPERTS)).reshape(1, LANE).astype(F32)
    u, meta, meta_t, cnt = _router(h.reshape(b, lp, D_MODEL), g, wr, br, n_end)
    meta = meta.reshape(tp, LANE)

    counts = cnt[0, :N_EXPERTS].astype(jnp.int32)
    padded = (counts + MOE_BLOCK - 1) // MOE_BLOCK * MOE_BLOCK
    pend = jnp.cumsum(padded)
    pstart = pend - padded
    n_assign = b * (n_end - FIRST) * TOP_K
    n_blocks = (n_assign + N_EXPERTS * (MOE_BLOCK - 1)) // MOE_BLOCK + 1
    n_slots = n_blocks * MOE_BLOCK
    block_start = jnp.arange(n_blocks, dtype=jnp.int32) * MOE_BLOCK
    block_e = jnp.minimum(jnp.sum((pend[None, :] <= block_start[:, None]).astype(jnp.int32), -1),
                          N_EXPERTS - 1)
    e0, e1, rank0, rank1 = (meta_t[n].astype(jnp.int32) for n in (F_E0, F_E1, F_RANK0, F_RANK1))
    valid = meta_t[F_VALID] > 0.5
    slot0 = pstart[e0] + rank0
    slot1 = pstart[e1] + rank1
    spare = n_slots + jnp.arange(tp, dtype=jnp.int32) % LANE
    d0 = jnp.where(valid, slot0, spare)
    d1 = jnp.where(valid, slot1, spare + LANE)
    c0 = jnp.where(valid, slot0, 0)
    c1 = jnp.where(valid, slot1, 0)

    last_block = jnp.where(padded > 0, pend // MOE_BLOCK - 1, -1)
    min_used = -(-n_assign // MOE_BLOCK)
    after = jnp.arange(min_used, n_blocks + 1, dtype=jnp.int32)
    after = jnp.where(after >= pend[N_EXPERTS - 1] // MOE_BLOCK, after, -1)
    zero_blocks = jnp.concatenate([last_block, after]).astype(jnp.int32)
    assert 2 * LANE == MOE_BLOCK
    xb = _dispatch(zero_blocks, d0, d1, u, n_slots + 2 * LANE, LANE)
    block_fill = jnp.clip(pstart[block_e] + counts[block_e] - block_start, 0, MOE_BLOCK).astype(jnp.int32)
    yb = _experts(block_e, block_fill, xb, w_gate.astype(BF16), w_up.astype(BF16), w_down.astype(BF16))
    return _combine(c0, c1, h, meta, final_g, yb, LANE, b, n_end - FIRST - N_META)
```

```python
import functools
import math

import numpy as np
import jax
import jax.numpy as jnp
from jax import lax
from jax.experimental import pallas as pl
from jax.experimental.pallas import tpu as pltpu

F32 = jnp.float32
BF16 = jnp.bfloat16

D_MODEL = 1024
CHUNK = 64
N_META = 16
RMS_EPS = 1e-6

MLA_HEADS = 4
MLA_NOPE = 128
MLA_ROPE = 64
MLA_V = 128
MLA_Q_LORA = 512
MLA_KV_LORA = 256
ROPE_THETA = 10000.0
MLA_WIDTH = MLA_HEADS * MLA_V
MLA_QK_PAD = 256
Q_SCALE = float((MLA_NOPE + MLA_ROPE) ** -0.5 * np.log2(np.e))
MLA_COLS_PAD = MLA_Q_LORA + MLA_KV_LORA + 2 * MLA_ROPE

RWKV_WIDTH = 512
RWKV_HEAD = 64
RWKV_HEADS = RWKV_WIDTH // RWKV_HEAD
RWKV_PAIRS = RWKV_HEADS // 2
W_LORA = 64
A_LORA = 64
V_LORA = 32
G_LORA = 128
GN_EPS = 64e-5
RWKV_COLS = 3 * RWKV_WIDTH + W_LORA + A_LORA + G_LORA
SCAN_CHUNK = 64
SCAN_STEP_CHUNKS = 11

N_EXPERTS = 8
TOP_K = 2
MOE_BLOCK = 256

LANE = 128
FRONT_PAD = -N_META % LANE
FIRST = FRONT_PAD
SEQ_TILE = 384
FLASH_KEYS = 256
MLA_UP_TILE_MAX = 1536
ROW_TILE = 1024
PREV_ROWS = 16
ROW_TILES = D_MODEL // LANE
DMA_UNROLL = 8
VMEM_LIMIT = 56 * 1024 * 1024
EXPERT_VMEM_LIMIT = 60 * 1024 * 1024
F_E0, F_E1, F_RANK0, F_RANK1, F_GATE0, F_GATE1, F_VALID, N_FIELDS = range(8)


def _cparams(sem, vmem=VMEM_LIMIT):
    return pltpu.CompilerParams(dimension_semantics=sem, vmem_limit_bytes=vmem)


def _rms(x, g):
    return x * lax.rsqrt(jnp.mean(x * x, -1, keepdims=True) + RMS_EPS) * g


def _sigmoid(x):
    return 1.0 / (1.0 + jnp.exp(-x))


def _split(x):
    hi = x.astype(BF16)
    return hi, (x - hi.astype(F32)).astype(BF16)


def _head_sum(x, bd):
    return jnp.dot(x.astype(BF16), bd, preferred_element_type=F32)


def _to_row_tiles(ref, x):
    for s in range(ROW_TILES):
        ref[pl.ds(s, x.shape[0], stride=ROW_TILES), :] = x[:, s * LANE:(s + 1) * LANE]


def _from_row_tiles(ref, rows):
    return jnp.concatenate([ref[pl.ds(s, rows, stride=ROW_TILES), :] for s in range(ROW_TILES)], axis=1)


def _row_tile(ref, r):
    return ref.at[pl.ds(pl.multiple_of(r * ROW_TILES, ROW_TILES), ROW_TILES), :]


def _const_spec(shape):
    nd = len(shape)
    return pl.BlockSpec(shape, lambda *_: (0,) * nd)


def _norm_inproj_kernel(h_ref, g_ref, wr_ref, wm_ref, pr_ref, pm_ref):
    u = _rms(h_ref[...], g_ref[...]).astype(BF16)
    pr_ref[...] = jnp.dot(u, wr_ref[...], preferred_element_type=F32).astype(pr_ref.dtype)
    pm_ref[...] = jnp.dot(u, wm_ref[...], preferred_element_type=F32).astype(pm_ref.dtype)


def _norm_inproj(h, g, w_r, w_m):
    tp = h.shape[0]
    return pl.pallas_call(
        _norm_inproj_kernel,
        grid=(tp // ROW_TILE,),
        in_specs=[pl.BlockSpec((ROW_TILE, D_MODEL), lambda i: (i, 0)),
                  _const_spec((1, D_MODEL)),
                  _const_spec(w_r.shape), _const_spec(w_m.shape)],
        out_specs=[pl.BlockSpec((ROW_TILE, w_r.shape[1]), lambda i: (i, 0)),
                   pl.BlockSpec((ROW_TILE, w_m.shape[1]), lambda i: (i, 0))],
        out_shape=[jax.ShapeDtypeStruct((tp, w_r.shape[1]), BF16),
                   jax.ShapeDtypeStruct((tp, w_m.shape[1]), BF16)],
        compiler_params=_cparams(("parallel",)),
        name="norm_inproj",
    )(h, g, w_r, w_m)


def _embed_inproj_kernel(*refs, n_end):
    n_sub = SEQ_TILE // LANE
    x_refs = refs[:n_sub]
    meta_ref, g_ref, wr_ref, wm_ref, h_ref, pr_ref, pm_ref = refs[n_sub:]
    i = pl.program_id(1)
    front = jnp.concatenate([jnp.zeros((FRONT_PAD, D_MODEL), F32), meta_ref[...]], 0)
    parts = [jnp.where(i == 0, front, x_refs[0][0])] + [r[0] for r in x_refs[1:]]
    h = jnp.concatenate(parts, 0)
    pos = i * SEQ_TILE + lax.broadcasted_iota(jnp.int32, (SEQ_TILE, 1), 0)
    h = jnp.where(pos < n_end, h, 0.0)
    h_ref[0] = h
    u = _rms(h, g_ref[...]).astype(BF16)
    pr_ref[0] = jnp.dot(u, wr_ref[...], preferred_element_type=F32).astype(pr_ref.dtype)
    pm_ref[0] = jnp.dot(u, wm_ref[...], preferred_element_type=F32).astype(pm_ref.dtype)


def _embed_inproj(x, meta_tokens, lp, g, w_r, w_m):
    b, seq, _ = x.shape
    n_sub = SEQ_TILE // LANE
    assert FRONT_PAD + N_META == LANE and seq % LANE == 0
    last = seq // LANE - 1
    x_specs = [pl.BlockSpec((1, LANE, D_MODEL), functools.partial(
        lambda bi, i, k: (bi, jnp.clip(i * n_sub + k - 1, 0, last), 0), k=k)) for k in range(n_sub)]
    row = lambda bi, i: (bi, i, 0)
    return pl.pallas_call(
        functools.partial(_embed_inproj_kernel, n_end=FIRST + N_META + seq),
        grid=(b, lp // SEQ_TILE),
        in_specs=x_specs + [_const_spec((N_META, D_MODEL)), _const_spec((1, D_MODEL)),
                            _const_spec(w_r.shape), _const_spec(w_m.shape)],
        out_specs=[pl.BlockSpec((1, SEQ_TILE, D_MODEL), row), pl.BlockSpec((1, SEQ_TILE, w_r.shape[1]), row),
                   pl.BlockSpec((1, SEQ_TILE, w_m.shape[1]), row)],
        out_shape=[jax.ShapeDtypeStruct((b, lp, D_MODEL), F32), jax.ShapeDtypeStruct((b, lp, w_r.shape[1]), BF16),
                   jax.ShapeDtypeStruct((b, lp, w_m.shape[1]), BF16)],
        compiler_params=_cparams(("parallel", "parallel")),
        name="embed_inproj",
    )(*([x] * n_sub), meta_tokens.astype(F32), g, w_r, w_m)


def _mla_up_kernel(pm_ref, qn_ref, kvn_ref, wq_ref, wkv_ref, tq_ref, tk_ref, q_ref, k_ref, vt_ref):
    pm = pm_ref[0].astype(F32)
    c_q = pm[:, :MLA_Q_LORA]
    c_kv = pm[:, MLA_Q_LORA:MLA_Q_LORA + MLA_KV_LORA]
    k_r = pm[:, MLA_Q_LORA + MLA_KV_LORA:]
    q = jnp.dot(_rms(c_q, qn_ref[...]).astype(BF16), wq_ref[...], preferred_element_type=F32)
    kv = jnp.dot(_rms(c_kv, kvn_ref[...]).astype(BF16), wkv_ref[...], preferred_element_type=F32)
    rope_lanes = lax.broadcasted_iota(jnp.int32, (1, LANE), 1) < MLA_ROPE
    t_k = tk_ref[...]
    ck = jnp.where(rope_lanes, t_k, 0.0)
    sk = jnp.where(rope_lanes, pltpu.roll(t_k, MLA_ROPE, 1), 0.0)
    k_rope = (k_r * ck + pltpu.roll(k_r, MLA_ROPE, 1) * sk).astype(BF16)
    t_q = tq_ref[...]
    cq = jnp.concatenate([jnp.full(t_q.shape, Q_SCALE, F32), jnp.where(rope_lanes, t_q, 0.0)], 1)
    sq = jnp.concatenate([jnp.zeros(t_q.shape, F32), jnp.where(rope_lanes, pltpu.roll(t_q, MLA_ROPE, 1), 0.0)], 1)
    for hd in range(MLA_HEADS):
        qh = q[:, hd * MLA_QK_PAD:(hd + 1) * MLA_QK_PAD]
        qh = qh * cq + pltpu.roll(qh, MLA_QK_PAD - MLA_ROPE, 1) * sq
        q_ref[0, :, hd * MLA_QK_PAD:(hd + 1) * MLA_QK_PAD] = qh.astype(BF16)
        k_ref[0, :, hd * MLA_QK_PAD:hd * MLA_QK_PAD + MLA_NOPE] = (
            kv[:, hd * MLA_NOPE:(hd + 1) * MLA_NOPE].astype(BF16))
        k_ref[0, :, hd * MLA_QK_PAD + MLA_NOPE:(hd + 1) * MLA_QK_PAD] = k_rope
    vt_ref[0] = kv[:, MLA_HEADS * MLA_NOPE:].T.astype(BF16)


def _mla_up(pm, q_norm, kv_norm, wq, wkv, tabs):
    b, lp, _ = pm.shape
    t_q, t_k = tabs
    qk_w = MLA_HEADS * MLA_QK_PAD
    tile = max(t for t in range(LANE, MLA_UP_TILE_MAX + 1, LANE) if lp % t == 0)
    row = lambda bi, i: (bi, i, 0)
    tab = lambda bi, i: (i, 0)
    return pl.pallas_call(
        _mla_up_kernel,
        grid=(b, lp // tile),
        in_specs=[pl.BlockSpec((1, tile, MLA_COLS_PAD), row),
                  _const_spec((1, MLA_Q_LORA)), _const_spec((1, MLA_KV_LORA)),
                  _const_spec(wq.shape), _const_spec(wkv.shape),
                  pl.BlockSpec((tile, LANE), tab), pl.BlockSpec((tile, LANE), tab)],
        out_specs=[pl.BlockSpec((1, tile, qk_w), row),
                   pl.BlockSpec((1, tile, qk_w), row),
                   pl.BlockSpec((1, MLA_WIDTH, tile), lambda bi, i: (bi, 0, i))],
        out_shape=[jax.ShapeDtypeStruct((b, lp, qk_w), BF16),
                   jax.ShapeDtypeStruct((b, lp, qk_w), BF16),
                   jax.ShapeDtypeStruct((b, MLA_WIDTH, lp), BF16)],
        compiler_params=_cparams(("parallel", "parallel")),
        name="mla_up",
    )(pm, q_norm, kv_norm, wq, wkv, t_q, t_k)


def _rope_tables(lp):
    half = MLA_ROPE // 2
    inv = (np.float32(ROPE_THETA) ** (-np.arange(half, dtype=np.float32) / np.float32(half))).astype(np.float32)
    pos = np.maximum(np.arange(lp) - FIRST, 0).astype(np.float32)
    ang = (pos[:, None] * inv[None, :]).astype(np.float64)
    t_k = np.concatenate([np.cos(ang), np.cos(ang), np.sin(ang), np.sin(ang)], -1)
    return jnp.asarray(t_k * Q_SCALE, F32), jnp.asarray(t_k, F32)


def _chunk_id(row):
    frame = row - (FIRST + N_META)
    return jnp.where(frame < 0, 0, 1 + (frame >> (CHUNK.bit_length() - 1)))


def _flash_kernel(q_ref, k_ref, vt_ref, o_ref, m_sc, l_sc, acc_sc, *, n_end):
    i = pl.program_id(1)
    tq = q_ref.shape[1]
    lp = k_ref.shape[1]
    heads = range(MLA_HEADS)
    q = [q_ref[0, :, h * MLA_QK_PAD:(h + 1) * MLA_QK_PAD] for h in heads]
    m_sc[...] = jnp.full(m_sc.shape, -jnp.inf, F32)
    l_sc[...] = jnp.zeros(l_sc.shape, F32)
    acc_sc[...] = jnp.zeros(acc_sc.shape, F32)
    n_full = (i * tq) // FLASH_KEYS
    full_end = n_full * FLASH_KEYS
    q_cid = _chunk_id(i * tq + lax.broadcasted_iota(jnp.int32, (1, tq), 1))

    def chunk(start, width, mask):
        keys = pl.ds(start, width)
        s = [lax.dot_general(k_ref[0, keys, h * MLA_QK_PAD:(h + 1) * MLA_QK_PAD], q[h], _NT,
                             preferred_element_type=F32) for h in heads]
        if mask is not None:
            k_pos = start + lax.broadcasted_iota(jnp.int32, (width, 1), 0)
            visible = k_pos >= FIRST
            if mask == "causal":
                visible = visible & (k_pos >= full_end) & (k_pos < n_end) & (_chunk_id(k_pos) <= q_cid)
            s = [jnp.where(visible, s[h], -jnp.inf) for h in heads]
        m_prev = [m_sc[h] for h in heads]
        m_new = [jnp.maximum(m_prev[h], jnp.max(s[h], 0, keepdims=True)) for h in heads]
        p = [jnp.exp2(s[h] - m_new[h]) for h in heads]
        for h in heads:
            alpha = jnp.exp2(m_prev[h] - m_new[h])
            l_sc[h] = alpha * l_sc[h] + jnp.sum(p[h], 0, keepdims=True)
            acc_sc[h] = alpha * acc_sc[h] + jnp.dot(vt_ref[0, h * MLA_V:(h + 1) * MLA_V, keys],
                                                    p[h].astype(BF16), preferred_element_type=F32)
            m_sc[h] = m_new[h]

    def chunk_pair(j, carry):
        chunk(pl.multiple_of(j * (2 * FLASH_KEYS), 2 * FLASH_KEYS), 2 * FLASH_KEYS, None)
        return carry

    @pl.when(n_full >= 2)
    def _():
        chunk(0, 2 * FLASH_KEYS, "front")

    lax.fori_loop(1, n_full // 2, chunk_pair, 0)

    @pl.when(n_full % 2 == 1)
    def _():
        chunk(pl.multiple_of(full_end - FLASH_KEYS, FLASH_KEYS), FLASH_KEYS, "front")

    for offset in range(0, FLASH_KEYS, math.gcd(FLASH_KEYS, tq)):
        @pl.when(i * tq - full_end == offset)
        def _(width=offset + tq):
            chunk(pl.multiple_of(full_end, LANE), width, "causal")

    for h in heads:
        o_ref[0, :, h * MLA_V:(h + 1) * MLA_V] = (acc_sc[h] / l_sc[h]).T.astype(o_ref.dtype)


def _flash(q, k, vt, n_end):
    b, lp, _ = q.shape
    assert SEQ_TILE % LANE == 0 and FLASH_KEYS % LANE == 0 and SEQ_TILE % CHUNK == 0 and (FIRST + N_META) % CHUNK == 0
    assert FIRST + N_META <= min(SEQ_TILE, FLASH_KEYS)
    return pl.pallas_call(
        functools.partial(_flash_kernel, n_end=n_end),
        grid=(b, lp // SEQ_TILE),
        in_specs=[pl.BlockSpec((1, SEQ_TILE, MLA_HEADS * MLA_QK_PAD), lambda bi, i: (bi, i, 0)),
                  pl.BlockSpec((1, lp, MLA_HEADS * MLA_QK_PAD), lambda bi, i: (bi, 0, 0)),
                  pl.BlockSpec((1, MLA_WIDTH, lp), lambda bi, i: (bi, 0, 0))],
        out_specs=pl.BlockSpec((1, SEQ_TILE, MLA_WIDTH), lambda bi, i: (bi, i, 0)),
        out_shape=jax.ShapeDtypeStruct((b, lp, MLA_WIDTH), BF16),
        scratch_shapes=[pltpu.VMEM((MLA_HEADS, 1, SEQ_TILE), F32), pltpu.VMEM((MLA_HEADS, 1, SEQ_TILE), F32),
                        pltpu.VMEM((MLA_HEADS, MLA_V, SEQ_TILE), F32)],
        compiler_params=_cparams(("parallel", "arbitrary")),
        name="mla_flash",
    )(q, k, vt)


def _rwkv_prep_kernel(*refs, has_vres):
    (p_ref, prev_ref, mu_ref, w0_ref, w2_ref, a0_ref, a2_ref, g2_ref, kk_ref, ka_ref, rk_ref,
     bd_ref) = refs[:12]
    n_in = 16 if has_vres else 12
    r_out, k_out, v_out, kk_out, bb_out, lw_out, bonus_out, g_out = refs[n_in:]
    c = RWKV_WIDTH
    tm = p_ref.shape[1]
    first_tile = pl.program_id(1) == 0
    pos = pl.program_id(1) * tm + lax.broadcasted_iota(jnp.int32, (tm, 1), 0)
    p = jnp.where(pos >= FIRST, p_ref[0].astype(F32), 0.0)
    halo = jnp.where(first_tile, 0.0, prev_ref[0, PREV_ROWS - 1:PREV_ROWS, :].astype(F32))
    rolled = pltpu.roll(p, 1, 0)
    head = jnp.where(lax.broadcasted_iota(jnp.int32, (8, 1), 0) == 0, halo, rolled[:8])
    prev = jnp.concatenate([head, rolled[8:]], 0)
    ps = p + (prev - p) * mu_ref[...]
    r = ps[:, :c]
    k = ps[:, c:2 * c]
    v = ps[:, 2 * c:3 * c]
    xwa = ps[:, 3 * c:3 * c + W_LORA + A_LORA]
    xg = ps[:, 3 * c + W_LORA + A_LORA:]
    zw = w0_ref[...] + jnp.dot(jnp.tanh(xwa).astype(BF16), w2_ref[...], preferred_element_type=F32)
    nz = -zw
    softplus = jnp.maximum(nz, 0.0) + jnp.log(1.0 + jnp.exp(-jnp.abs(nz)))
    log_decay = -jnp.exp(-softplus - 0.5)
    if has_vres:
        vf_ref, v0_ref, v1_ref, v2_ref = refs[12:16]
        lo = jnp.dot(v.astype(BF16), v1_ref[...], preferred_element_type=F32)
        gate = _sigmoid(v0_ref[...] + jnp.dot(lo.astype(BF16), v2_ref[...], preferred_element_type=F32))
        v = v + (vf_ref[0].astype(F32) - v) * gate
    a = _sigmoid(a0_ref[...] + jnp.dot(xwa.astype(BF16), a2_ref[...], preferred_element_type=F32))
    g = jnp.dot(_sigmoid(xg).astype(BF16), g2_ref[...], preferred_element_type=F32)
    bd = bd_ref[...]
    kk = k * kk_ref[...]
    kk = kk * lax.rsqrt(jnp.maximum(_head_sum(kk * kk, bd), 1e-24))
    k = k * (1.0 + (a - 1.0) * ka_ref[...])
    bonus = _head_sum(r * k * rk_ref[...], bd) * v
    r_out[0] = r.astype(r_out.dtype)
    k_out[0] = k.astype(k_out.dtype)
    v_out[0] = v.astype(v_out.dtype)
    kk_out[0] = kk.astype(kk_out.dtype)
    bb_out[0] = (kk * a).astype(bb_out.dtype)
    lw_out[0] = log_decay
    bonus_out[0] = bonus.astype(bonus_out.dtype)
    g_out[0] = g.astype(g_out.dtype)


def _rwkv_prep(pr, prm, v_first):
    b, lp, cols = pr.shape
    has_vres = v_first is not None
    c = RWKV_WIDTH
    row = lambda bi, i: (bi, i, 0)
    prev = lambda bi, i: (bi, jnp.maximum(i * (SEQ_TILE // PREV_ROWS) - 1, 0), 0)
    names = ["mu", "w0", "w2", "a0", "a2", "g2", "k_k", "k_a", "r_k", "bd"]
    args = [pr, pr] + [prm[n] for n in names]
    in_specs = [pl.BlockSpec((1, SEQ_TILE, cols), row), pl.BlockSpec((1, PREV_ROWS, cols), prev)]
    in_specs += [_const_spec(prm[n].shape) for n in names]
    if has_vres:
        args += [v_first, prm["v0"], prm["v1"], prm["v2"]]
        in_specs += [pl.BlockSpec((1, SEQ_TILE, c), row)]
        in_specs += [_const_spec(prm[n].shape) for n in ("v0", "v1", "v2")]
    out_spec = pl.BlockSpec((1, SEQ_TILE, c), row)
    return pl.pallas_call(
        functools.partial(_rwkv_prep_kernel, has_vres=has_vres),
        grid=(b, lp // SEQ_TILE),
        in_specs=in_specs,
        out_specs=[out_spec] * 8,
        out_shape=[jax.ShapeDtypeStruct((b, lp, c), F32 if n == 5 else BF16) for n in range(8)],
        compiler_params=_cparams(("parallel", "parallel")),
        name="rwkv_prep",
    )(*args)


_NN = (((1,), (0,)), ((), ()))
_NT = (((1,), (1,)), ((), ()))


def _mm(a, b, dims):
    return lax.dot_general(a.astype(BF16), b.astype(BF16), dims, preferred_element_type=F32)


def _rwkv_scan_kernel(r_ref, k_ref, v_ref, kk_ref, bb_ref, lw_ref, y_ref, s_sc):
    @pl.when(pl.program_id(1) == 0)
    def _():
        s_sc[...] = jnp.zeros(s_sc.shape, F32)

    c = SCAN_CHUNK
    n = 2 * c
    n_chunks = r_ref.shape[1] // c
    ri = lax.broadcasted_iota(jnp.int32, (c, c), 0)
    ci = lax.broadcasted_iota(jnp.int32, (c, c), 1)
    tri = jnp.where(ri >= ci, 1.0, 0.0).astype(BF16)
    row = lax.broadcasted_iota(jnp.int32, (2 * n, 2 * n), 0)
    col = lax.broadcasted_iota(jnp.int32, (2 * n, 2 * n), 1)
    t_idx = row & (c - 1)
    s_idx = col & (c - 1)
    causal = (t_idx > s_idx) | ((t_idx == s_idx) & (row >= n))
    eye = jnp.where(lax.broadcasted_iota(jnp.int32, (n, n), 0) == lax.broadcasted_iota(jnp.int32, (n, n), 1),
                    1.0, 0.0).astype(F32)
    head0 = lax.broadcasted_iota(jnp.int32, (1, LANE), 1) < RWKV_HEAD

    def stack(x):
        return jnp.concatenate([jnp.where(head0, x, 0.0), jnp.where(head0, 0.0, x)], axis=0)

    ar, bk, v_t, w_end = [], [], [], []
    for ch in range(n_chunks):
        rows = slice(ch * c, (ch + 1) * c)
        lw = lw_ref[0, rows, :]
        lw_hi = lw.astype(BF16)
        lw_lo = (lw - lw_hi.astype(F32)).astype(BF16)
        cum = (jnp.dot(tri, lw_hi, preferred_element_type=F32)
               + jnp.dot(tri, lw_lo, preferred_element_type=F32))
        w_incl = jnp.exp(cum)
        w_inv = jnp.exp(-cum)
        a_hat = -kk_ref[0, rows, :].astype(F32) * jnp.exp(cum - lw)
        r_hat = r_ref[0, rows, :].astype(F32) * w_incl
        b_hat = bb_ref[0, rows, :].astype(F32) * w_inv
        k_hat = k_ref[0, rows, :].astype(F32) * w_inv
        v = v_ref[0, rows, :].astype(F32)
        for pr in range(RWKV_PAIRS):
            sl = slice(pr * LANE, (pr + 1) * LANE)
            ar.append(jnp.concatenate([stack(a_hat[:, sl]), stack(r_hat[:, sl])], 0).astype(BF16))
            bk.append(jnp.concatenate([stack(b_hat[:, sl]), stack(k_hat[:, sl])], 0).astype(BF16))
            v_t.append(stack(v[:, sl]).T.astype(BF16))
            w_end.append(w_incl[c - 1:c, sl])
    every = range(len(ar))
    gram = [jnp.where(causal, _mm(ar[g], bk[g], _NT), 0.0) for g in every]
    pw = [gram[g][:n, :n] for g in every]
    t_inv = [eye + pw[g] for g in every]
    pw = [_mm(pw[g], pw[g], _NN) for g in every]
    for _ in range(4):
        both = [_mm(pw[g], jnp.concatenate([pw[g], t_inv[g]], 1), _NN) for g in every]
        pw = [both[g][:, :n] for g in every]
        t_inv = [t_inv[g] + both[g][:, n:] for g in every]
    t_inv = [(t_inv[g] + _mm(pw[g], t_inv[g], _NN)).astype(BF16) for g in every]
    va = [_mm(v_t[g], gram[g][:n, n:], _NT) for g in every]
    out_bk = [jnp.concatenate([gram[g][n:, :].T.astype(BF16), bk[g]], 1) for g in every]

    state = [s_sc[pr] for pr in range(RWKV_PAIRS)]
    for ch in range(n_chunks):
        gs = [ch * RWKV_PAIRS + pr for pr in range(RWKV_PAIRS)]
        s_ar = [_mm(state[pr], ar[g], _NT) for pr, g in enumerate(gs)]
        u_t = [_mm(s_ar[pr][:, :n] + va[g], t_inv[g], _NT) for pr, g in enumerate(gs)]
        uv = [jnp.concatenate([u_t[pr].astype(BF16), v_t[g]], 1) for pr, g in enumerate(gs)]
        both = [_mm(uv[pr], out_bk[g], _NN) for pr, g in enumerate(gs)]
        state = [(state[pr] + both[pr][:, n:]) * w_end[g] for pr, g in enumerate(gs)]
        for pr in range(RWKV_PAIRS):
            y = (s_ar[pr][:, n:] + both[pr][:, :n]).T
            y_ref[0, ch * c:(ch + 1) * c, pr * LANE:(pr + 1) * LANE] = y[:c] + y[c:]
    for pr in range(RWKV_PAIRS):
        s_sc[pr] = state[pr]


def _rwkv_scan(r, k, v, kk, bb, lw):
    b, lp, c = r.shape
    rows = SCAN_STEP_CHUNKS * SCAN_CHUNK
    spec = pl.BlockSpec((1, rows, c), lambda bi, i: (bi, i, 0))
    return pl.pallas_call(
        _rwkv_scan_kernel,
        grid=(b, lp // rows),
        in_specs=[spec] * 6,
        out_specs=spec,
        out_shape=jax.ShapeDtypeStruct((b, lp, c), F32),
        scratch_shapes=[pltpu.VMEM((RWKV_PAIRS, LANE, LANE), F32)],
        compiler_params=_cparams(("parallel", "arbitrary")),
        name="rwkv_scan",
    )(r, k, v, kk, bb, lw)


def _outproj_kernel(ys_ref, bonus_ref, g_ref, ymla_ref, lnw_ref, lnb_ref, bd_ref, wo_ref, h_ref, o_ref):
    y = ys_ref[...]
    bd = bd_ref[...]
    inv_n = 1.0 / RWKV_HEAD
    d = y - _head_sum(y, bd) * inv_n
    var = _head_sum(d * d, bd) * inv_n
    yn = d * lax.rsqrt(var + GN_EPS) * lnw_ref[...] + lnb_ref[...]
    yr = ((yn + bonus_ref[...].astype(F32)) * g_ref[...].astype(F32)).astype(BF16)
    o_ref[...] = (h_ref[...]
                  + jnp.dot(ymla_ref[...], wo_ref[:MLA_WIDTH, :], preferred_element_type=F32)
                  + jnp.dot(yr, wo_ref[MLA_WIDTH:, :], preferred_element_type=F32))


def _outproj(ys, bonus, g, ymla, ln_w, ln_b, bd, wo, h):
    tp = h.shape[0]
    c = RWKV_WIDTH
    rc = pl.BlockSpec((ROW_TILE, c), lambda i: (i, 0))
    rd = pl.BlockSpec((ROW_TILE, D_MODEL), lambda i: (i, 0))
    return pl.pallas_call(
        _outproj_kernel,
        grid=(tp // ROW_TILE,),
        in_specs=[rc, rc, rc, rc, _const_spec((1, c)), _const_spec((1, c)), _const_spec(bd.shape),
                  _const_spec(wo.shape), rd],
        out_specs=rd,
        out_shape=jax.ShapeDtypeStruct((tp, D_MODEL), F32),
        compiler_params=_cparams(("parallel",)),
        name="outproj",
    )(ys, bonus, g, ymla, ln_w, ln_b, bd, wo, h)


def _ffn_chunks(u, wg_ref, wu_ref, wd_ref, idx, ff, chunk):
    acc = None
    for c0 in range(0, ff, chunk):
        sl = slice(c0, c0 + chunk)
        gate = jnp.dot(u, wg_ref[idx + (slice(None), sl)], preferred_element_type=F32)
        up = jnp.dot(u, wu_ref[idx + (slice(None), sl)], preferred_element_type=F32)
        act = (gate * _sigmoid(gate) * up).astype(BF16)
        part = jnp.dot(act, wd_ref[idx + (sl, slice(None))], preferred_element_type=F32)
        acc = part if acc is None else acc + part
    return acc


def _dense_ffn_kernel(h_ref, g_ref, wg_ref, wu_ref, wd_ref, o_ref, *, ff, chunk):
    h = h_ref[...]
    u = _rms(h, g_ref[...]).astype(BF16)
    o_ref[...] = h + _ffn_chunks(u, wg_ref, wu_ref, wd_ref, (), ff, chunk)


def _dense_ffn(h, g, wg, wu, wd):
    tp = h.shape[0]
    ff = wg.shape[1]
    rd = pl.BlockSpec((ROW_TILE, D_MODEL), lambda i: (i, 0))
    once = lambda shape: pl.BlockSpec(shape, lambda i: (0, 0), pipeline_mode=pl.Buffered(1))
    return pl.pallas_call(
        functools.partial(_dense_ffn_kernel, ff=ff, chunk=256),
        grid=(tp // ROW_TILE,),
        in_specs=[rd, _const_spec((1, D_MODEL)), once(wg.shape), once(wu.shape), once(wd.shape)],
        out_specs=rd,
        out_shape=jax.ShapeDtypeStruct((tp, D_MODEL), F32),
        compiler_params=_cparams(("parallel",)),
        name="dense_ffn",
    )(h, g, wg, wu, wd)


def _router_kernel(h_ref, g_ref, wr_ref, br_ref, u_ref, meta_ref, meta_t_ref, cnt_ref, carry_sc, *, n_end):
    @pl.when((pl.program_id(0) == 0) & (pl.program_id(1) == 0))
    def _():
        carry_sc[...] = jnp.zeros(carry_sc.shape, F32)

    tm = h_ref.shape[1]
    u = _rms(h_ref[0], g_ref[...])
    _to_row_tiles(u_ref, u)
    u_hi, u_lo = _split(u)
    w_hi, w_lo = _split(wr_ref[...])
    logits = (jnp.dot(u_hi, w_hi, preferred_element_type=F32) + jnp.dot(u_hi, w_lo, preferred_element_type=F32)
              + jnp.dot(u_lo, w_hi, preferred_element_type=F32) + br_ref[...])
    lane = lax.broadcasted_iota(jnp.int32, (tm, LANE), 1).astype(F32)
    logits = jnp.where(lane < N_EXPERTS, logits, -jnp.inf)
    top0 = jnp.max(logits, -1, keepdims=True)
    e0 = jnp.min(jnp.where(logits == top0, lane, float(LANE)), -1, keepdims=True)
    rest = jnp.where(lane == e0, -jnp.inf, logits)
    top1 = jnp.max(rest, -1, keepdims=True)
    e1 = jnp.min(jnp.where(rest == top1, lane, float(LANE)), -1, keepdims=True)
    ex = jnp.exp(top1 - top0)
    g0 = 1.0 / (1.0 + ex)
    g1 = ex / (1.0 + ex)
    pos = pl.program_id(1) * tm + lax.broadcasted_iota(jnp.int32, (tm, 1), 0)
    valid = (pos >= FIRST) & (pos < n_end)
    oh0 = jnp.where((lane == e0) & valid, 1.0, 0.0)
    oh1 = jnp.where((lane == e1) & valid, 1.0, 0.0)
    ri = lax.broadcasted_iota(jnp.int32, (tm, tm), 0)
    ci = lax.broadcasted_iota(jnp.int32, (tm, tm), 1)
    before = jnp.where(ri > ci, 1.0, 0.0).astype(BF16)
    both = oh0 + oh1
    seen = carry_sc[...] + jnp.dot(before, both.astype(BF16), preferred_element_type=F32)
    rank0 = jnp.sum(jnp.where(lane == e0, seen, 0.0), -1, keepdims=True)
    rank1 = jnp.sum(jnp.where(lane == e1, seen, 0.0), -1, keepdims=True)
    carry_sc[...] = carry_sc[...] + jnp.sum(both, 0, keepdims=True)
    cnt_ref[...] = jnp.broadcast_to(carry_sc[...], cnt_ref.shape)
    vf = jnp.where(valid, 1.0, 0.0)
    meta = jnp.zeros((tm, LANE), F32)
    for field, value in ((F_E0, e0), (F_E1, e1), (F_RANK0, rank0), (F_RANK1, rank1),
                         (F_GATE0, g0 * vf), (F_GATE1, g1 * vf), (F_VALID, vf)):
        meta = jnp.where(lane == field, value, meta)
    meta_ref[0] = meta
    meta_t_ref[...] = meta.T[:N_FIELDS]


def _router(h, g, wr, br, n_end):
    b, lp, _ = h.shape
    row = lambda bi, i: (bi, i, 0)
    return pl.pallas_call(
        functools.partial(_router_kernel, n_end=n_end),
        grid=(b, lp // SEQ_TILE),
        in_specs=[pl.BlockSpec((1, SEQ_TILE, D_MODEL), row), _const_spec((1, D_MODEL)),
                  _const_spec(wr.shape), _const_spec(br.shape)],
        out_specs=[pl.BlockSpec((SEQ_TILE * ROW_TILES, LANE), lambda bi, i: (bi * (lp // SEQ_TILE) + i, 0)),
                   pl.BlockSpec((1, SEQ_TILE, LANE), row),
                   pl.BlockSpec((N_FIELDS, SEQ_TILE), lambda bi, i: (0, bi * (lp // SEQ_TILE) + i)),
                   _const_spec((8, LANE))],
        out_shape=[jax.ShapeDtypeStruct((b * lp * ROW_TILES, LANE), F32), jax.ShapeDtypeStruct((b, lp, LANE), F32),
                   jax.ShapeDtypeStruct((N_FIELDS, b * lp), F32), jax.ShapeDtypeStruct((8, LANE), F32)],
        scratch_shapes=[pltpu.VMEM((1, LANE), F32)],
        compiler_params=_cparams(("arbitrary", "arbitrary")),
        name="moe_router",
    )(h, g, wr, br)


def _dispatch_kernel(zb_ref, s0_ref, s1_ref, u_ref, xb_ref, zero_sc, sem):
    tm = u_ref.shape[0] // ROW_TILES
    block_rows = MOE_BLOCK * ROW_TILES

    @pl.when(pl.program_id(0) == 0)
    def _():
        zero_sc[...] = jnp.zeros(zero_sc.shape, F32)

        def fill(j):
            start = pl.multiple_of(zb_ref[j] * block_rows, block_rows)
            return pltpu.make_async_copy(zero_sc, xb_ref.at[pl.ds(start, block_rows), :], sem)

        for j in range(zb_ref.shape[0]):
            @pl.when(zb_ref[j] >= 0)
            def _(j=j):
                fill(j).start()
        for j in range(zb_ref.shape[0]):
            @pl.when(zb_ref[j] >= 0)
            def _(j=j):
                fill(j).wait()

    def copies(rw):
        return [pltpu.make_async_copy(_row_tile(u_ref, rw), _row_tile(xb_ref, s_ref[0, 0, rw]), sem)
                for s_ref in (s0_ref, s1_ref)]

    def start(j, carry):
        for kq in range(DMA_UNROLL):
            for thread, cp in enumerate(copies(j * DMA_UNROLL + kq)):
                cp.start(priority=thread)
        return carry

    lax.fori_loop(0, tm // DMA_UNROLL, start, 0)
    for _ in (s0_ref, s1_ref):
        pltpu.make_async_copy(u_ref, xb_ref.at[pl.ds(0, tm * ROW_TILES), :], sem).wait()


def _dispatch(zero_blocks, slot0, slot1, u, n_rows, tm):
    tp = u.shape[0] // ROW_TILES
    sspec = pl.BlockSpec((1, 1, tm), lambda i, zb: (i, 0, 0), memory_space=pltpu.SMEM)
    return pl.pallas_call(
        _dispatch_kernel,
        grid_spec=pltpu.PrefetchScalarGridSpec(
            num_scalar_prefetch=1,
            grid=(tp // tm,),
            in_specs=[sspec, sspec, pl.BlockSpec((tm * ROW_TILES, LANE), lambda i, zb: (i, 0))],
            out_specs=pl.BlockSpec(memory_space=pl.ANY),
            scratch_shapes=[pltpu.VMEM((MOE_BLOCK * ROW_TILES, LANE), F32), pltpu.SemaphoreType.DMA(())]),
        out_shape=jax.ShapeDtypeStruct((n_rows * ROW_TILES, LANE), F32),
        compiler_params=_cparams(("arbitrary",)),
        name="moe_dispatch",
    )(zero_blocks, slot0.reshape(tp // tm, 1, tm), slot1.reshape(tp // tm, 1, tm), u)


def _expert_kernel(be_ref, fill_ref, x_ref, wg_ref, wu_ref, wd_ref, o_ref, *, ff, chunk):
    del be_ref
    fill = fill_ref[pl.program_id(0)]

    @pl.when(fill > 0)
    def _():
        x = _from_row_tiles(x_ref, MOE_BLOCK).astype(BF16)
        _to_row_tiles(o_ref, _ffn_chunks(x, wg_ref, wu_ref, wd_ref, (0,), ff, chunk))

    @pl.when(fill <= 0)
    def _():
        o_ref[...] = jnp.zeros(o_ref.shape, F32)


def _experts(block_e, block_fill, xb, wg, wu, wd):
    n_slots = block_e.shape[0] * MOE_BLOCK
    ff = wg.shape[2]
    rows = pl.BlockSpec((MOE_BLOCK * ROW_TILES, LANE), lambda i, be, fill: (i, 0))
    return pl.pallas_call(
        functools.partial(_expert_kernel, ff=ff, chunk=ff // 2),
        grid_spec=pltpu.PrefetchScalarGridSpec(
            num_scalar_prefetch=2,
            grid=(n_slots // MOE_BLOCK,),
            in_specs=[rows,
                      pl.BlockSpec((1, D_MODEL, ff), lambda i, be, fill: (be[i], 0, 0)),
                      pl.BlockSpec((1, D_MODEL, ff), lambda i, be, fill: (be[i], 0, 0)),
                      pl.BlockSpec((1, ff, D_MODEL), lambda i, be, fill: (be[i], 0, 0))],
            out_specs=rows),
        out_shape=jax.ShapeDtypeStruct((n_slots * ROW_TILES, LANE), F32),
        compiler_params=_cparams(("arbitrary",), EXPERT_VMEM_LIMIT),
        name="moe_experts",
    )(block_e, block_fill, xb, wg, wu, wd)


def _combine_kernel(s0_ref, s1_ref, n0_ref, n1_ref, h_ref, gates_ref, fn_ref, yb_ref, o_ref, y0_sc, y1_sc, sems, *,
                    per_row, n_frame_tiles):
    t = pl.program_id(0)
    tm = h_ref.shape[0]
    cur = t % 2

    def copies(rw, a_ref, b_ref, buf):
        return [pltpu.make_async_copy(_row_tile(yb_ref, s_ref[0, 0, rw]), _row_tile(y_sc.at[buf], rw), sems.at[buf])
                for s_ref, y_sc in ((a_ref, y0_sc), (b_ref, y1_sc))]

    def issue(a_ref, b_ref, buf):
        def body(j, carry):
            for kq in range(DMA_UNROLL):
                for thread, cp in enumerate(copies(j * DMA_UNROLL + kq, a_ref, b_ref, buf)):
                    cp.start(priority=thread)
            return carry
        lax.fori_loop(0, tm // DMA_UNROLL, body, 0)

    @pl.when(t == 0)
    def _():
        issue(s0_ref, s1_ref, 0)

    @pl.when(t + 1 < pl.num_programs(0))
    def _():
        issue(n0_ref, n1_ref, 1 - cur)

    for y_sc in (y0_sc, y1_sc):
        pltpu.make_async_copy(yb_ref.at[pl.ds(0, tm * ROW_TILES), :], y_sc.at[cur], sems.at[cur]).wait()
    tile = t % per_row

    @pl.when((tile >= 1) & (tile <= n_frame_tiles))
    def _():
        gates = gates_ref[...]
        f = (_from_row_tiles(y0_sc.at[cur], tm) * gates[:, F_GATE0:F_GATE0 + 1]
             + _from_row_tiles(y1_sc.at[cur], tm) * gates[:, F_GATE1:F_GATE1 + 1])
        o_ref[0] = _rms(h_ref[...] + f, fn_ref[...])


def _combine(slot0, slot1, h, meta, fn, yb, tm, b, seq):
    tp = h.shape[0]
    n_tiles = tp // tm
    per_row = n_tiles // b
    assert (FIRST + N_META) == tm and seq % tm == 0
    sspec = pl.BlockSpec((1, 1, tm), lambda i: (i, 0, 0), memory_space=pltpu.SMEM)
    nspec = pl.BlockSpec((1, 1, tm), lambda i: (jnp.minimum(i + 1, n_tiles - 1), 0, 0), memory_space=pltpu.SMEM)
    rd = pl.BlockSpec((tm, D_MODEL), lambda i: (i, 0))
    s0 = slot0.reshape(n_tiles, 1, tm)
    s1 = slot1.reshape(n_tiles, 1, tm)
    return pl.pallas_call(
        functools.partial(_combine_kernel, per_row=per_row, n_frame_tiles=seq // tm),
        grid=(n_tiles,),
        in_specs=[sspec, sspec, nspec, nspec, rd, pl.BlockSpec((tm, LANE), lambda i: (i, 0)),
                  _const_spec((1, D_MODEL)), pl.BlockSpec(memory_space=pl.ANY)],
        out_specs=pl.BlockSpec((1, tm, D_MODEL), lambda i: (i // per_row, jnp.clip(i % per_row - 1, 0, seq // tm - 1), 0)),
        out_shape=jax.ShapeDtypeStruct((b, seq, D_MODEL), F32),
        scratch_shapes=[pltpu.VMEM((2, tm * ROW_TILES, LANE), F32), pltpu.VMEM((2, tm * ROW_TILES, LANE), F32),
                        pltpu.SemaphoreType.DMA((2,))],
        compiler_params=_cparams(("arbitrary",)),
        name="moe_combine",
    )(s0, s1, s0, s1, h, meta, fn, yb)


def _final_norm_kernel(h_ref, g_ref, o_ref):
    o_ref[...] = _rms(h_ref[...], g_ref[...])


def _final_norm(h, g):
    tp = h.shape[0]
    rd = pl.BlockSpec((ROW_TILE, D_MODEL), lambda i: (i, 0))
    return pl.pallas_call(
        _final_norm_kernel,
        grid=(tp // ROW_TILE,),
        in_specs=[rd, _const_spec((1, D_MODEL))],
        out_specs=rd,
        out_shape=jax.ShapeDtypeStruct((tp, D_MODEL), F32),
        compiler_params=_cparams(("parallel",)),
        name="final_norm",
    )(h, g)


def _rot_cols(w):
    half = MLA_ROPE // 2
    return jnp.concatenate([-w[..., half:], w[..., :half]], -1)


def _prep_inproj(w_in):
    m_q = w_in[:, :MLA_Q_LORA + MLA_KV_LORA]
    k_rope = w_in[:, MLA_Q_LORA + MLA_KV_LORA:MLA_Q_LORA + MLA_KV_LORA + MLA_ROPE]
    w_m = jnp.concatenate([m_q, k_rope, _rot_cols(k_rope)], -1)
    w_r = w_in[:, MLA_Q_LORA + MLA_KV_LORA + MLA_ROPE:]
    return w_r.astype(BF16), w_m.astype(BF16)


def _prep_wq(w_uq):
    w = w_uq.reshape(MLA_Q_LORA, MLA_HEADS, MLA_NOPE + MLA_ROPE)
    rope = w[..., MLA_NOPE:]
    return jnp.concatenate([w, _rot_cols(rope)], -1).reshape(MLA_Q_LORA, MLA_HEADS * MLA_QK_PAD).astype(BF16)


def _prep_wkv(w_ukv):
    w = w_ukv.reshape(MLA_KV_LORA, MLA_HEADS, MLA_NOPE + MLA_V)
    k_nope = w[..., :MLA_NOPE].reshape(MLA_KV_LORA, MLA_HEADS * MLA_NOPE)
    v = w[..., MLA_NOPE:].reshape(MLA_KV_LORA, MLA_HEADS * MLA_V)
    return jnp.concatenate([k_nope, v], -1).astype(BF16)


def _row(x):
    return x.reshape(1, -1).astype(F32)


def _block_diag_ones():
    idx = np.arange(RWKV_WIDTH) // RWKV_HEAD
    return jnp.asarray(idx[:, None] == idx[None, :], BF16)


def kernel(x, meta_tokens, attn_norm, ffn_norm, final_norm, w_in, w_out, mla_q_norm, mla_kv_norm, mla_w_uq, mla_w_ukv, rwkv_mu, rwkv_w0, rwkv_w2, rwkv_a0, rwkv_a2, rwkv_g2, rwkv_k_k, rwkv_k_a, rwkv_r_k, rwkv_ln_w, rwkv_ln_b, rwkv_v0, rwkv_v1, rwkv_v2, ffn_w_gate, ffn_w_up, ffn_w_down, moe_router, moe_router_bias, moe_w_gate, moe_w_up, moe_w_down):
    b, seq, d = x.shape
    depth = attn_norm.shape[0]
    n_end = FIRST + N_META + seq
    lp = -(-n_end // SEQ_TILE) * SEQ_TILE
    tp = b * lp
    assert d == D_MODEL and tp % ROW_TILE == 0 and lp % (SCAN_STEP_CHUNKS * SCAN_CHUNK) == 0

    tabs = _rope_tables(lp)
    bd = _block_diag_ones()
    zeros_wa = jnp.zeros((W_LORA, RWKV_WIDTH), F32)
    v_first = None
    for i in range(depth):
        w_r, w_m = _prep_inproj(w_in[i])
        if i == 0:
            h, pr, pm = _embed_inproj(x, meta_tokens, lp, _row(attn_norm[i]), w_r, w_m)
            h = h.reshape(tp, d)
        else:
            pr, pm = _norm_inproj(h, _row(attn_norm[i]), w_r, w_m)

        q, k, v = _mla_up(pm.reshape(b, lp, -1), _row(mla_q_norm[i]), _row(mla_kv_norm[i]),
                          _prep_wq(mla_w_uq[i]), _prep_wkv(mla_w_ukv[i]), tabs)
        y_mla = _flash(q, k, v, n_end)

        prm = {
            "mu": _row(rwkv_mu[i]), "w0": _row(rwkv_w0[i]), "a0": _row(rwkv_a0[i]),
            "w2": jnp.concatenate([rwkv_w2[i], zeros_wa], 0).astype(BF16),
            "a2": jnp.concatenate([zeros_wa, rwkv_a2[i]], 0).astype(BF16),
            "g2": rwkv_g2[i].astype(BF16),
            "k_k": _row(rwkv_k_k[i]), "k_a": _row(rwkv_k_a[i]), "r_k": _row(rwkv_r_k[i]), "bd": bd,
        }
        if i > 0:
            prm["v0"] = _row(rwkv_v0[i - 1])
            prm["v1"] = jnp.pad(rwkv_v1[i - 1], ((0, 0), (0, LANE - V_LORA))).astype(BF16)
            prm["v2"] = jnp.pad(rwkv_v2[i - 1], ((0, LANE - V_LORA), (0, 0))).astype(BF16)
        r_, k_, v_, kk_, bb_, lw_, bonus, gate = _rwkv_prep(pr.reshape(b, lp, -1), prm, v_first)
        if i == 0:
            v_first = v_
        ys = _rwkv_scan(r_, k_, v_, kk_, bb_, lw_)

        flat = lambda t: t.reshape(tp, -1)
        h = _outproj(flat(ys), flat(bonus), flat(gate), flat(y_mla), _row(rwkv_ln_w[i]), _row(rwkv_ln_b[i]),
                     bd, w_out[i].astype(BF16), h)

        j = i // 2
        last = i == depth - 1
        if i % 2 == 0:
            h = _dense_ffn(h, _row(ffn_norm[i]), ffn_w_gate[j].astype(BF16), ffn_w_up[j].astype(BF16),
                           ffn_w_down[j].astype(BF16))
            if last:
                h = _final_norm(h, _row(final_norm)).reshape(b, lp, d)[:, FIRST + N_META:n_end]
        else:
            if not last:
                raise NotImplementedError("an MoE layer that is not the last layer")
            h = _moe(h, b, lp, n_end, _row(ffn_norm[i]), moe_router[j], moe_router_bias[j],
                     moe_w_gate[j], moe_w_up[j], moe_w_down[j], _row(final_norm))
    return h


def _moe(h, b, lp, n_end, g, router, router_bias, w_gate, w_up, w_down, final_g):
    tp = b * lp
    wr = jnp.pad(router, ((0, 0), (0, LANE - N_EXPERTS))).astype(F32)
    br = jnp.pad(router_bias, (0, LANE - N_EXPERTS)).reshape(1, LANE).astype(F32)
    u, meta, meta_t, cnt = _router(h.reshape(b, lp, D_MODEL), g, wr, br, n_end)
    meta = meta.reshape(tp, LANE)

    counts = cnt[0, :N_EXPERTS].astype(jnp.int32)
    padded = (counts + MOE_BLOCK - 1) // MOE_BLOCK * MOE_BLOCK
    pend = jnp.cumsum(padded)
    pstart = pend - padded
    n_assign = b * (n_end - FIRST) * TOP_K
    n_blocks = (n_assign + N_EXPERTS * (MOE_BLOCK - 1)) // MOE_BLOCK + 1
    n_slots = n_blocks * MOE_BLOCK
    block_start = jnp.arange(n_blocks, dtype=jnp.int32) * MOE_BLOCK
    block_e = jnp.minimum(jnp.sum((pend[None, :] <= block_start[:, None]).astype(jnp.int32), -1),
                          N_EXPERTS - 1)
    e0, e1, rank0, rank1 = (meta_t[n].astype(jnp.int32) for n in (F_E0, F_E1, F_RANK0, F_RANK1))
    valid = meta_t[F_VALID] > 0.5
    slot0 = pstart[e0] + rank0
    slot1 = pstart[e1] + rank1
    spare = n_slots + jnp.arange(tp, dtype=jnp.int32) % LANE
    d0 = jnp.where(valid, slot0, spare)
    d1 = jnp.where(valid, slot1, spare + LANE)
    c0 = jnp.where(valid, slot0, 0)
    c1 = jnp.where(valid, slot1, 0)

    last_block = jnp.where(padded > 0, pend // MOE_BLOCK - 1, -1)
    min_used = -(-n_assign // MOE_BLOCK)
    after = jnp.arange(min_used, n_blocks + 1, dtype=jnp.int32)
    after = jnp.where(after >= pend[N_EXPERTS - 1] // MOE_BLOCK, after, -1)
    zero_blocks = jnp.concatenate([last_block, after]).astype(jnp.int32)
    assert 2 * LANE == MOE_BLOCK
    xb = _dispatch(zero_blocks, d0, d1, u, n_slots + 2 * LANE, LANE)
    block_fill = jnp.clip(pstart[block_e] + counts[block_e] - block_start, 0, MOE_BLOCK).astype(jnp.int32)
    yb = _experts(block_e, block_fill, xb, w_gate.astype(BF16), w_up.astype(BF16), w_down.astype(BF16))
    return _combine(c0, c1, h, meta, final_g, yb, LANE, b, n_end - FIRST - N_META)
```
